```python
import jax
import jax.numpy as jnp
from jax import lax
import numpy as np


D_MODEL = 2048
BATCH = 8
SEQ = 8192
DEPTH = 4

CTX_LEN = 256
GRID_W = 64
MIX_WIDTH = D_MODEL
RET_WIDTH = MIX_WIDTH // 2
CONV_WIDTH = MIX_WIDTH - RET_WIDTH
RET_HEAD_DIM = 128
RET_HEADS = RET_WIDTH // RET_HEAD_DIM
RET_CHUNK = 128
CONV_K = 31
FFN_K = 3
D_FF = ((8 * D_MODEL // 3 + 255) // 256) * 256
IN_WIDTH = 4 * RET_WIDTH + 2 * CONV_WIDTH
N_MOD = 6
ROPE_THETA = 10000.0
EPS = 1e-6

kernel_name = "hybrid_retention_conformer_dit"


def rmsnorm(x, g):
    xf = x.astype(jnp.float32)
    xf = xf * lax.rsqrt(jnp.mean(xf * xf, axis=-1, keepdims=True) + EPS)
    return (xf * g.astype(jnp.float32)).astype(x.dtype)


def layernorm(x, g, b):
    xf = x.astype(jnp.float32)
    mu = jnp.mean(xf, axis=-1, keepdims=True)
    var = jnp.mean(jnp.square(xf - mu), axis=-1, keepdims=True)
    y = (xf - mu) * lax.rsqrt(var + EPS) * g.astype(jnp.float32) + b.astype(jnp.float32)
    return y.astype(x.dtype)


def modulate(h, shift, scale):
    return h * (1 + scale) + shift


def rope_tables(pos, dim):
    inv_freq = 1.0 / (ROPE_THETA ** (jnp.arange(0, dim // 2, dtype=jnp.float32) / (dim // 2)))
    ang = pos.astype(jnp.float32)[:, None] * inv_freq[None, :]
    return jnp.cos(ang)[:, None, :], jnp.sin(ang)[:, None, :]


def apply_rope(x, cos, sin):
    half = x.shape[-1] // 2
    x1, x2 = x[..., :half], x[..., half:]
    cos = cos.astype(x.dtype)
    sin = sin.astype(x.dtype)
    return jnp.concatenate([x1 * cos - x2 * sin, x1 * sin + x2 * cos], axis=-1)


def rope_2d(x, row_cs, col_cs):
    half = x.shape[-1] // 2
    return jnp.concatenate([apply_rope(x[..., :half], *row_cs), apply_rope(x[..., half:], *col_cs)], axis=-1)


def dwconv1d(x, w, b):
    y = lax.conv_general_dilated(x, w[:, None, :].astype(x.dtype), window_strides=(1,), padding='SAME',
                                 dimension_numbers=('NWC', 'WIO', 'NWC'), feature_group_count=x.shape[-1])
    return y + b.astype(x.dtype)


def dwconv2d(x, w, b):
    y = lax.conv_general_dilated(x, w[:, :, None, :].astype(x.dtype), window_strides=(1, 1), padding='SAME',
                                 dimension_numbers=('NHWC', 'HWIO', 'NHWC'), feature_group_count=x.shape[-1])
    return y + b.astype(x.dtype)


def retention_chunkwise(q, k, v, log_gamma, s0):
    bsz, nh, length, dh = q.shape
    n_chunks = length // RET_CHUNK
    q = q.astype(jnp.float32)
    k = k.astype(jnp.float32) * (dh ** -0.5)
    v = v.astype(jnp.float32)
    to_chunks = lambda t: jnp.moveaxis(t.reshape(bsz, nh, n_chunks, RET_CHUNK, dh), 2, 0)
    idx = jnp.arange(RET_CHUNK, dtype=jnp.float32)
    diff = idx[:, None] - idx[None, :]
    lg = log_gamma[:, None, None]
    inner_decay = jnp.where(diff >= 0, jnp.exp(lg * jnp.maximum(diff, 0.0)), 0.0)
    q_decay = jnp.exp(log_gamma[:, None] * (idx + 1.0))[:, :, None]
    k_decay = jnp.exp(log_gamma[:, None] * (RET_CHUNK - 1.0 - idx))[:, :, None]
    chunk_decay = jnp.exp(log_gamma * RET_CHUNK)[:, None, None]

    def step(s, qkv):
        qc, kc, vc = qkv
        scores = jnp.einsum('bhnd,bhmd->bhnm', qc, kc) * inner_decay
        o = jnp.einsum('bhnm,bhmd->bhnd', scores, vc) + jnp.einsum('bhnd,bhde->bhne', qc, s) * q_decay
        s = s * chunk_decay + jnp.einsum('bhmd,bhme->bhde', kc * k_decay, vc)
        return s, o

    s_final, o = lax.scan(step, s0.astype(jnp.float32), (to_chunks(q), to_chunks(k), to_chunks(v)))
    o = jnp.moveaxis(o, 0, 2).reshape(bsz, nh, length, dh)
    return o, s_final


def bidir_retention(q, k, v, lg_f, lg_b, s0_f, s0_b):
    o_f, s_f = retention_chunkwise(q, k, v, lg_f, s0_f)
    rev = lambda t: jnp.flip(t, axis=2)
    o_b, s_b = retention_chunkwise(rev(q), rev(k), rev(v), lg_b, s0_b)
    return o_f + rev(o_b), s_f, s_b


def retention_output(o, g):
    o = o * lax.rsqrt(jnp.mean(o * o, axis=-1, keepdims=True) + EPS)
    bsz, nh, length, dh = o.shape
    o = jnp.transpose(o, (0, 2, 1, 3)).reshape(bsz, length, nh * dh)
    return (jax.nn.silu(g.astype(jnp.float32)) * o).astype(g.dtype)


def to_heads(t):
    bsz, length, _ = t.shape
    return t.reshape(bsz, length, RET_HEADS, RET_HEAD_DIM)


def heads_first(t):
    return jnp.transpose(t, (0, 2, 1, 3))


def conformer_conv(a, b, w, bias, ln_g, ln_b):
    u = a * jax.nn.sigmoid(b)
    u = dwconv1d(u, w, bias)
    return jax.nn.silu(layernorm(u, ln_g, ln_b))


def conv_ffn(h, w_up, w_down, conv):
    gate, val = jnp.split(h @ w_up, 2, axis=-1)
    return (jax.nn.silu(conv(gate)) * val) @ w_down


def split_mixer_proj(p):
    return jnp.split(p, [RET_WIDTH, 2 * RET_WIDTH, 3 * RET_WIDTH, 4 * RET_WIDTH, 4 * RET_WIDTH + CONV_WIDTH], axis=-1)


def _fwd_setup_inputs(seed: int = 0) -> dict:
    key = jax.random.key(seed)
    ks = jax.random.split(key, 24)
    f32 = jnp.float32
    nrm = lambda k, shape, s: s * jax.random.normal(k, shape, f32)
    decay_base = jnp.log(jnp.exp2(5.0 + jnp.arange(RET_HEADS, dtype=f32)) - 1.0)
    return {
        'x': nrm(ks[0], (BATCH, SEQ, D_MODEL), 1.0),
        'c': nrm(ks[1], (BATCH, D_MODEL), 1.0),
        'ctx': nrm(ks[2], (BATCH, CTX_LEN, D_MODEL), 1.0),
        'c_ctx': nrm(ks[3], (D_MODEL,), 1.0),
        'w_mod': nrm(ks[4], (DEPTH, D_MODEL, N_MOD * D_MODEL), 0.5 * D_MODEL ** -0.5),
        'b_mod': nrm(ks[5], (DEPTH, N_MOD * D_MODEL), 0.02),
        'norm1_g': 1.0 + nrm(ks[6], (DEPTH, D_MODEL), 0.02),
        'norm2_g': 1.0 + nrm(ks[7], (DEPTH, D_MODEL), 0.02),
        'w_in': nrm(ks[8], (DEPTH, D_MODEL, IN_WIDTH), D_MODEL ** -0.5),
        'ret_decay_f': decay_base + nrm(ks[9], (DEPTH, RET_HEADS), 0.1),
        'ret_decay_b': decay_base + nrm(ks[10], (DEPTH, RET_HEADS), 0.1),
        'conv_dw_w': nrm(ks[11], (DEPTH, CONV_K, CONV_WIDTH), CONV_K ** -0.5),
        'conv_dw_b': nrm(ks[12], (DEPTH, CONV_WIDTH), 0.02),
        'conv_ln_g': 1.0 + nrm(ks[13], (DEPTH, CONV_WIDTH), 0.02),
        'conv_ln_b': nrm(ks[14], (DEPTH, CONV_WIDTH), 0.02),
        'w_out': nrm(ks[15], (DEPTH, MIX_WIDTH, D_MODEL), MIX_WIDTH ** -0.5),
        'ffn_w_up': nrm(ks[16], (DEPTH, D_MODEL, 2 * D_FF), D_MODEL ** -0.5),
        'ffn_dw_w': nrm(ks[17], (DEPTH, FFN_K, FFN_K, D_FF), 1.0 / FFN_K),
        'ffn_dw_b': nrm(ks[18], (DEPTH, D_FF), 0.02),
        'ffn_w_down': nrm(ks[19], (DEPTH, D_FF, D_MODEL), D_FF ** -0.5),
        'final_norm_g': 1.0 + nrm(ks[20], (D_MODEL,), 0.02),
    }


def _fwd_reference(x, c, ctx, c_ctx, w_mod, b_mod, norm1_g, norm2_g, w_in, ret_decay_f, ret_decay_b,
              conv_dw_w, conv_dw_b, conv_ln_g, conv_ln_b, w_out, ffn_w_up, ffn_dw_w, ffn_dw_b,
              ffn_w_down, final_norm_g):
    bsz, length, _ = x.shape
    rows = length // GRID_W
    t = jnp.arange(length)
    row_cs = rope_tables(t // GRID_W, RET_HEAD_DIM // 2)
    col_cs = rope_tables(t % GRID_W, RET_HEAD_DIM // 2)
    silu_c = jax.nn.silu(c)
    silu_cc = jax.nn.silu(c_ctx)[None, :]
    zero_state = jnp.zeros((bsz, RET_HEADS, RET_HEAD_DIM, RET_HEAD_DIM), jnp.float32)
    xc = ctx
    for l in range(DEPTH):
        last = l == DEPTH - 1
        mod = (silu_c @ w_mod[l] + b_mod[l])[:, None, :]
        mod_c = (silu_cc @ w_mod[l] + b_mod[l])[:, None, :]
        sh1, sc1, g1, sh2, sc2, g2 = jnp.split(mod, N_MOD, axis=-1)
        csh1, csc1, cg1, csh2, csc2, cg2 = jnp.split(mod_c, N_MOD, axis=-1)

        hx = modulate(rmsnorm(x, norm1_g[l]), sh1, sc1)
        hc = modulate(rmsnorm(xc, norm1_g[l]), csh1, csc1)
        q, k, v, g, a, bg = split_mixer_proj(hx @ w_in[l])
        cq, ck, cv, cgt, ca, cb = split_mixer_proj(hc @ w_in[l])
        lg_f = jax.nn.log_sigmoid(ret_decay_f[l].astype(jnp.float32))
        lg_b = jax.nn.log_sigmoid(ret_decay_b[l].astype(jnp.float32))

        co, s_f, s_b = bidir_retention(heads_first(to_heads(cq)), heads_first(to_heads(ck)),
                                       heads_first(to_heads(cv)), lg_f, lg_b, zero_state, zero_state)
        q = heads_first(rope_2d(to_heads(q), row_cs, col_cs))
        k = heads_first(rope_2d(to_heads(k), row_cs, col_cs))
        o, _, _ = bidir_retention(q, k, heads_first(to_heads(v)), lg_f, lg_b, s_f, s_b)

        mix = jnp.concatenate([
            retention_output(o, g),
            conformer_conv(a, bg, conv_dw_w[l], conv_dw_b[l], conv_ln_g[l], conv_ln_b[l]),
        ], axis=-1)
        x = x + g1 * (mix @ w_out[l])

        lat_conv = lambda u: dwconv2d(u.reshape(bsz, rows, GRID_W, D_FF), ffn_dw_w[l], ffn_dw_b[l]).reshape(bsz, length, D_FF)
        x = x + g2 * conv_ffn(modulate(rmsnorm(x, norm2_g[l]), sh2, sc2), ffn_w_up[l], ffn_w_down[l], lat_conv)

        if not last:
            cmix = jnp.concatenate([
                retention_output(co, cgt),
                conformer_conv(ca, cb, conv_dw_w[l], conv_dw_b[l], conv_ln_g[l], conv_ln_b[l]),
            ], axis=-1)
            xc = xc + cg1 * (cmix @ w_out[l])
            ctx_conv = lambda u: dwconv1d(u, ffn_dw_w[l][FFN_K // 2], ffn_dw_b[l])
            xc = xc + cg2 * conv_ffn(modulate(rmsnorm(xc, norm2_g[l]), csh2, csc2), ffn_w_up[l], ffn_w_down[l], ctx_conv)

    return rmsnorm(x, final_norm_g)


import jax as _jax
import jax.numpy as _jnp

TWIN_FORMAT = 'train_step'
FWD_PARAMS = ['x', 'c', 'ctx', 'c_ctx', 'w_mod', 'b_mod', 'norm1_g', 'norm2_g', 'w_in', 'ret_decay_f', 'ret_decay_b', 'conv_dw_w', 'conv_dw_b', 'conv_ln_g', 'conv_ln_b', 'w_out', 'ffn_w_up', 'ffn_dw_w', 'ffn_dw_b', 'ffn_w_down', 'final_norm_g']
TWIN_WEIGHTS = ['c_ctx', 'w_mod', 'b_mod', 'norm1_g', 'norm2_g', 'w_in', 'ret_decay_f', 'ret_decay_b', 'conv_dw_w', 'conv_dw_b', 'conv_ln_g', 'conv_ln_b', 'w_out', 'ffn_w_up', 'ffn_dw_w', 'ffn_dw_b', 'ffn_w_down', 'final_norm_g']
TWIN_DIFF_INPUT = 'x'
TWIN_INPUTS = ['x', 'c', 'ctx', 'c_ctx', 'w_mod', 'b_mod', 'norm1_g', 'norm2_g', 'w_in', 'ret_decay_f', 'ret_decay_b', 'conv_dw_w', 'conv_dw_b', 'conv_ln_g', 'conv_ln_b', 'w_out', 'ffn_w_up', 'ffn_dw_w', 'ffn_dw_b', 'ffn_w_down', 'final_norm_g', 'loss_target', 'm_c_ctx', 'm_w_mod', 'm_b_mod', 'm_norm1_g', 'm_norm2_g', 'm_w_in', 'm_ret_decay_f', 'm_ret_decay_b', 'm_conv_dw_w', 'm_conv_dw_b', 'm_conv_ln_g', 'm_conv_ln_b', 'm_w_out', 'm_ffn_w_up', 'm_ffn_dw_w', 'm_ffn_dw_b', 'm_ffn_w_down', 'm_final_norm_g', 'v_c_ctx', 'v_w_mod', 'v_b_mod', 'v_norm1_g', 'v_norm2_g', 'v_w_in', 'v_ret_decay_f', 'v_ret_decay_b', 'v_conv_dw_w', 'v_conv_dw_b', 'v_conv_ln_g', 'v_conv_ln_b', 'v_w_out', 'v_ffn_w_up', 'v_ffn_dw_w', 'v_ffn_dw_b', 'v_ffn_w_down', 'v_final_norm_g']
TWIN_OUTPUTS = ['loss', 'grad_x', 'grad_c_ctx', 'grad_w_mod', 'grad_b_mod', 'grad_norm1_g', 'grad_norm2_g', 'grad_w_in', 'grad_ret_decay_f', 'grad_ret_decay_b', 'grad_conv_dw_w', 'grad_conv_dw_b', 'grad_conv_ln_g', 'grad_conv_ln_b', 'grad_w_out', 'grad_ffn_w_up', 'grad_ffn_dw_w', 'grad_ffn_dw_b', 'grad_ffn_w_down', 'grad_final_norm_g', 'delta_c_ctx', 'delta_w_mod', 'delta_b_mod', 'delta_norm1_g', 'delta_norm2_g', 'delta_w_in', 'delta_ret_decay_f', 'delta_ret_decay_b', 'delta_conv_dw_w', 'delta_conv_dw_b', 'delta_conv_ln_g', 'delta_conv_ln_b', 'delta_w_out', 'delta_ffn_w_up', 'delta_ffn_dw_w', 'delta_ffn_dw_b', 'delta_ffn_w_down', 'delta_final_norm_g', 'new_m_c_ctx', 'new_m_w_mod', 'new_m_b_mod', 'new_m_norm1_g', 'new_m_norm2_g', 'new_m_w_in', 'new_m_ret_decay_f', 'new_m_ret_decay_b', 'new_m_conv_dw_w', 'new_m_conv_dw_b', 'new_m_conv_ln_g', 'new_m_conv_ln_b', 'new_m_w_out', 'new_m_ffn_w_up', 'new_m_ffn_dw_w', 'new_m_ffn_dw_b', 'new_m_ffn_w_down', 'new_m_final_norm_g', 'new_v_c_ctx', 'new_v_w_mod', 'new_v_b_mod', 'new_v_norm1_g', 'new_v_norm2_g', 'new_v_w_in', 'new_v_ret_decay_f', 'new_v_ret_decay_b', 'new_v_conv_dw_w', 'new_v_conv_dw_b', 'new_v_conv_ln_g', 'new_v_conv_ln_b', 'new_v_w_out', 'new_v_ffn_w_up', 'new_v_ffn_dw_w', 'new_v_ffn_dw_b', 'new_v_ffn_w_down', 'new_v_final_norm_g']
TWIN_LEAF_KINDS = {'loss': 'loss', 'grad_x': 'grad_x', 'grad_c_ctx': 'grad_w', 'grad_w_mod': 'grad_w', 'grad_b_mod': 'grad_w', 'grad_norm1_g': 'grad_w', 'grad_norm2_g': 'grad_w', 'grad_w_in': 'grad_w', 'grad_ret_decay_f': 'grad_w', 'grad_ret_decay_b': 'grad_w', 'grad_conv_dw_w': 'grad_w', 'grad_conv_dw_b': 'grad_w', 'grad_conv_ln_g': 'grad_w', 'grad_conv_ln_b': 'grad_w', 'grad_w_out': 'grad_w', 'grad_ffn_w_up': 'grad_w', 'grad_ffn_dw_w': 'grad_w', 'grad_ffn_dw_b': 'grad_w', 'grad_ffn_w_down': 'grad_w', 'grad_final_norm_g': 'grad_w', 'delta_c_ctx': 'delta_w', 'delta_w_mod': 'delta_w', 'delta_b_mod': 'delta_w', 'delta_norm1_g': 'delta_w', 'delta_norm2_g': 'delta_w', 'delta_w_in': 'delta_w', 'delta_ret_decay_f': 'delta_w', 'delta_ret_decay_b': 'delta_w', 'delta_conv_dw_w': 'delta_w', 'delta_conv_dw_b': 'delta_w', 'delta_conv_ln_g': 'delta_w', 'delta_conv_ln_b': 'delta_w', 'delta_w_out': 'delta_w', 'delta_ffn_w_up': 'delta_w', 'delta_ffn_dw_w': 'delta_w', 'delta_ffn_dw_b': 'delta_w', 'delta_ffn_w_down': 'delta_w', 'delta_final_norm_g': 'delta_w', 'new_m_c_ctx': 'new_m', 'new_m_w_mod': 'new_m', 'new_m_b_mod': 'new_m', 'new_m_norm1_g': 'new_m', 'new_m_norm2_g': 'new_m', 'new_m_w_in': 'new_m', 'new_m_ret_decay_f': 'new_m', 'new_m_ret_decay_b': 'new_m', 'new_m_conv_dw_w': 'new_m', 'new_m_conv_dw_b': 'new_m', 'new_m_conv_ln_g': 'new_m', 'new_m_conv_ln_b': 'new_m', 'new_m_w_out': 'new_m', 'new_m_ffn_w_up': 'new_m', 'new_m_ffn_dw_w': 'new_m', 'new_m_ffn_dw_b': 'new_m', 'new_m_ffn_w_down': 'new_m', 'new_m_final_norm_g': 'new_m', 'new_v_c_ctx': 'new_v', 'new_v_w_mod': 'new_v', 'new_v_b_mod': 'new_v', 'new_v_norm1_g': 'new_v', 'new_v_norm2_g': 'new_v', 'new_v_w_in': 'new_v', 'new_v_ret_decay_f': 'new_v', 'new_v_ret_decay_b': 'new_v', 'new_v_conv_dw_w': 'new_v', 'new_v_conv_dw_b': 'new_v', 'new_v_conv_ln_g': 'new_v', 'new_v_conv_ln_b': 'new_v', 'new_v_w_out': 'new_v', 'new_v_ffn_w_up': 'new_v', 'new_v_ffn_dw_w': 'new_v', 'new_v_ffn_dw_b': 'new_v', 'new_v_ffn_w_down': 'new_v', 'new_v_final_norm_g': 'new_v'}


def _forward(args):
    return _fwd_reference(*[args[k] for k in FWD_PARAMS])


def _output_shape():
    def fwd():
        inp = _fwd_setup_inputs(0)
        return _fwd_reference(*[inp[k] for k in FWD_PARAMS])
    out = _jax.eval_shape(fwd)
    return out.shape, out.dtype

N_MICROBATCH = 1
ADAM_LR = 0.001
ADAM_B1 = 0.9
ADAM_B2 = 0.999
ADAM_EPS = 1e-08
ADAM_WD = 0.01
ADAM_STEP = 10
PER_EXAMPLE_BATCH_AXIS = {'x': 0, 'c': 0, 'ctx': 0, 'loss_target': 0}
SHARED_INPUTS = []
_WEIGHT_DTYPES = {'c_ctx': _jnp.float32, 'w_mod': _jnp.float32, 'b_mod': _jnp.float32, 'norm1_g': _jnp.float32, 'norm2_g': _jnp.float32, 'w_in': _jnp.float32, 'ret_decay_f': _jnp.float32, 'ret_decay_b': _jnp.float32, 'conv_dw_w': _jnp.float32, 'conv_dw_b': _jnp.float32, 'conv_ln_g': _jnp.float32, 'conv_ln_b': _jnp.float32, 'w_out': _jnp.float32, 'ffn_w_up': _jnp.float32, 'ffn_dw_w': _jnp.float32, 'ffn_dw_b': _jnp.float32, 'ffn_w_down': _jnp.float32, 'final_norm_g': _jnp.float32}
MOMENT_SCALE = {'c_ctx': 2.810799e-02, 'w_mod': 4.073566e-02, 'b_mod': 6.944106e-02, 'norm1_g': 3.725870e-02, 'norm2_g': 3.947309e-02, 'w_in': 2.518456e-02, 'ret_decay_f': 1.315204e-01, 'ret_decay_b': 6.181970e-02, 'conv_dw_w': 2.356993e-02, 'conv_dw_b': 4.347944e-02, 'conv_ln_g': 2.778034e-02, 'conv_ln_b': 2.612626e-02, 'w_out': 2.423640e-02, 'ffn_w_up': 1.747115e-02, 'ffn_dw_w': 1.763426e-02, 'ffn_dw_b': 1.501796e-02, 'ffn_w_down': 2.860729e-02, 'final_norm_g': 3.200875e+01}


def _to_microbatches(a, axis):
    t = _jnp.moveaxis(a, axis, 0)
    t = t.reshape((N_MICROBATCH, t.shape[0] // N_MICROBATCH) + t.shape[1:])
    return _jnp.moveaxis(t, 1, axis + 1)


def setup_inputs(seed: int = 0) -> dict:
    inp = _fwd_setup_inputs(seed)
    key = _jax.random.fold_in(_jax.random.key(seed), 7919)
    shape, _ = _output_shape()
    out = dict(inp)
    out["loss_target"] = _jax.random.normal(_jax.random.fold_in(key, 0), shape, _jnp.float32)
    for i, name in enumerate(TWIN_WEIGHTS):
        w = inp[name].astype(_jnp.float32)
        if MOMENT_SCALE is None:
            s = _jnp.sqrt(_jnp.mean(_jnp.square(w)) + 1e-30)
        else:
            s = MOMENT_SCALE[name]
        km, kv = _jax.random.split(_jax.random.fold_in(key, i + 1))
        out[name] = w
        out["m_" + name] = s * _jax.random.normal(km, w.shape, _jnp.float32)
        out["v_" + name] = (s * s) * _jax.random.uniform(kv, w.shape, _jnp.float32, 0.5, 1.5)
    if N_MICROBATCH > 1:
        for name, axis in PER_EXAMPLE_BATCH_AXIS.items():
            out[name] = _to_microbatches(out[name], axis)
    return {'x': out['x'], 'c': out['c'], 'ctx': out['ctx'], 'c_ctx': out['c_ctx'], 'w_mod': out['w_mod'], 'b_mod': out['b_mod'], 'norm1_g': out['norm1_g'], 'norm2_g': out['norm2_g'], 'w_in': out['w_in'], 'ret_decay_f': out['ret_decay_f'], 'ret_decay_b': out['ret_decay_b'], 'conv_dw_w': out['conv_dw_w'], 'conv_dw_b': out['conv_dw_b'], 'conv_ln_g': out['conv_ln_g'], 'conv_ln_b': out['conv_ln_b'], 'w_out': out['w_out'], 'ffn_w_up': out['ffn_w_up'], 'ffn_dw_w': out['ffn_dw_w'], 'ffn_dw_b': out['ffn_dw_b'], 'ffn_w_down': out['ffn_w_down'], 'final_norm_g': out['final_norm_g'], 'loss_target': out['loss_target'], 'm_c_ctx': out['m_c_ctx'], 'm_w_mod': out['m_w_mod'], 'm_b_mod': out['m_b_mod'], 'm_norm1_g': out['m_norm1_g'], 'm_norm2_g': out['m_norm2_g'], 'm_w_in': out['m_w_in'], 'm_ret_decay_f': out['m_ret_decay_f'], 'm_ret_decay_b': out['m_ret_decay_b'], 'm_conv_dw_w': out['m_conv_dw_w'], 'm_conv_dw_b': out['m_conv_dw_b'], 'm_conv_ln_g': out['m_conv_ln_g'], 'm_conv_ln_b': out['m_conv_ln_b'], 'm_w_out': out['m_w_out'], 'm_ffn_w_up': out['m_ffn_w_up'], 'm_ffn_dw_w': out['m_ffn_dw_w'], 'm_ffn_dw_b': out['m_ffn_dw_b'], 'm_ffn_w_down': out['m_ffn_w_down'], 'm_final_norm_g': out['m_final_norm_g'], 'v_c_ctx': out['v_c_ctx'], 'v_w_mod': out['v_w_mod'], 'v_b_mod': out['v_b_mod'], 'v_norm1_g': out['v_norm1_g'], 'v_norm2_g': out['v_norm2_g'], 'v_w_in': out['v_w_in'], 'v_ret_decay_f': out['v_ret_decay_f'], 'v_ret_decay_b': out['v_ret_decay_b'], 'v_conv_dw_w': out['v_conv_dw_w'], 'v_conv_dw_b': out['v_conv_dw_b'], 'v_conv_ln_g': out['v_conv_ln_g'], 'v_conv_ln_b': out['v_conv_ln_b'], 'v_w_out': out['v_w_out'], 'v_ffn_w_up': out['v_ffn_w_up'], 'v_ffn_dw_w': out['v_ffn_dw_w'], 'v_ffn_dw_b': out['v_ffn_dw_b'], 'v_ffn_w_down': out['v_ffn_w_down'], 'v_final_norm_g': out['v_final_norm_g']}


def _loss(weights, diff, rest, loss_target):
    with _jax.named_scope("forward"):
        args = {**rest, TWIN_DIFF_INPUT: diff, **{k: w.astype(_WEIGHT_DTYPES[k]) for k, w in weights.items()}}
        y = _forward(args)
    with _jax.named_scope("loss_head"):
        err = _jnp.square(y.astype(_jnp.float32) - loss_target)
        return 0.5 * _jnp.sum(_jnp.mean(err, axis=-1)) if err.ndim else 0.5 * err


def _adamw(w, g, m, v):
    m = ADAM_B1 * m + (1.0 - ADAM_B1) * g
    v = ADAM_B2 * v + (1.0 - ADAM_B2) * _jnp.square(g)
    m_hat = m / (1.0 - ADAM_B1 ** ADAM_STEP)
    v_hat = v / (1.0 - ADAM_B2 ** ADAM_STEP)
    delta = -ADAM_LR * (m_hat / (_jnp.sqrt(v_hat) + ADAM_EPS) + ADAM_WD * w)
    return delta, m, v


def reference(x, c, ctx, c_ctx, w_mod, b_mod, norm1_g, norm2_g, w_in, ret_decay_f, ret_decay_b, conv_dw_w, conv_dw_b, conv_ln_g, conv_ln_b, w_out, ffn_w_up, ffn_dw_w, ffn_dw_b, ffn_w_down, final_norm_g, loss_target, m_c_ctx, m_w_mod, m_b_mod, m_norm1_g, m_norm2_g, m_w_in, m_ret_decay_f, m_ret_decay_b, m_conv_dw_w, m_conv_dw_b, m_conv_ln_g, m_conv_ln_b, m_w_out, m_ffn_w_up, m_ffn_dw_w, m_ffn_dw_b, m_ffn_w_down, m_final_norm_g, v_c_ctx, v_w_mod, v_b_mod, v_norm1_g, v_norm2_g, v_w_in, v_ret_decay_f, v_ret_decay_b, v_conv_dw_w, v_conv_dw_b, v_conv_ln_g, v_conv_ln_b, v_w_out, v_ffn_w_up, v_ffn_dw_w, v_ffn_dw_b, v_ffn_w_down, v_final_norm_g):
    given = dict(x=x, c=c, ctx=ctx, c_ctx=c_ctx, w_mod=w_mod, b_mod=b_mod, norm1_g=norm1_g, norm2_g=norm2_g, w_in=w_in, ret_decay_f=ret_decay_f, ret_decay_b=ret_decay_b, conv_dw_w=conv_dw_w, conv_dw_b=conv_dw_b, conv_ln_g=conv_ln_g, conv_ln_b=conv_ln_b, w_out=w_out, ffn_w_up=ffn_w_up, ffn_dw_w=ffn_dw_w, ffn_dw_b=ffn_dw_b, ffn_w_down=ffn_w_down, final_norm_g=final_norm_g, loss_target=loss_target, m_c_ctx=m_c_ctx, m_w_mod=m_w_mod, m_b_mod=m_b_mod, m_norm1_g=m_norm1_g, m_norm2_g=m_norm2_g, m_w_in=m_w_in, m_ret_decay_f=m_ret_decay_f, m_ret_decay_b=m_ret_decay_b, m_conv_dw_w=m_conv_dw_w, m_conv_dw_b=m_conv_dw_b, m_conv_ln_g=m_conv_ln_g, m_conv_ln_b=m_conv_ln_b, m_w_out=m_w_out, m_ffn_w_up=m_ffn_w_up, m_ffn_dw_w=m_ffn_dw_w, m_ffn_dw_b=m_ffn_dw_b, m_ffn_w_down=m_ffn_w_down, m_final_norm_g=m_final_norm_g, v_c_ctx=v_c_ctx, v_w_mod=v_w_mod, v_b_mod=v_b_mod, v_norm1_g=v_norm1_g, v_norm2_g=v_norm2_g, v_w_in=v_w_in, v_ret_decay_f=v_ret_decay_f, v_ret_decay_b=v_ret_decay_b, v_conv_dw_w=v_conv_dw_w, v_conv_dw_b=v_conv_dw_b, v_conv_ln_g=v_conv_ln_g, v_conv_ln_b=v_conv_ln_b, v_w_out=v_w_out, v_ffn_w_up=v_ffn_w_up, v_ffn_dw_w=v_ffn_dw_w, v_ffn_dw_b=v_ffn_dw_b, v_ffn_w_down=v_ffn_w_down, v_final_norm_g=v_final_norm_g)
    weights = {n: given[n] for n in TWIN_WEIGHTS}
    shared = {n: given[n] for n in SHARED_INPUTS}
    per_example = {n: given[n] for n in ['x', 'c', 'ctx']}
    grad_fn = _jax.value_and_grad(_loss, argnums=(0, 1))

    def one_microbatch(ex, loss_target):
        ex = dict(ex)
        diff = ex.pop(TWIN_DIFF_INPUT)
        return grad_fn(weights, diff, {**shared, **ex}, loss_target)

    if N_MICROBATCH == 1:
        loss, (grad_w, grad_x) = one_microbatch(per_example, given["loss_target"])
    else:
        def body(carry, xs):
            loss_sum, grad_sum = carry
            l_k, (gw_k, gx_k) = one_microbatch(xs[0], xs[1])
            with _jax.named_scope("update"):
                return (loss_sum + l_k, _jax.tree.map(_jnp.add, grad_sum, gw_k)), gx_k

        init = (_jnp.zeros((), _jnp.float32), _jax.tree.map(_jnp.zeros_like, weights))
        (loss, grad_w), grad_x = _jax.lax.scan(body, init, (per_example, given["loss_target"]))
    with _jax.named_scope("update"):
        delta_w, new_m, new_v = {}, {}, {}
        for n in TWIN_WEIGHTS:
            delta_w[n], new_m[n], new_v[n] = _adamw(weights[n], grad_w[n], given["m_" + n], given["v_" + n])
    return (loss, grad_x, *[grad_w[n] for n in TWIN_WEIGHTS], *[delta_w[n] for n in TWIN_WEIGHTS],
            *[new_m[n] for n in TWIN_WEIGHTS], *[new_v[n] for n in TWIN_WEIGHTS])
```

```python
import jax
import jax.numpy as jnp
from jax import lax
from jax.experimental import pallas as pl
from jax.experimental.pallas import tpu as pltpu

F32 = jnp.float32
BF16 = jnp.bfloat16
EPS = 1e-6
DH = 128
RC = 128
GRID_W = 64
ROPE_THETA = 10000.0
N_MOD = 6
R = 256
HALO = 16
CONV_K = 31
FOFF = 8 + GRID_W
SUB = 64
LANES = 256
VMEM_LIMIT = 48 * 1024 * 1024
MESH = pl.DeviceIdType.MESH
ADAM_LR, ADAM_B1, ADAM_B2, ADAM_EPS, ADAM_WD, ADAM_STEP = 0.001, 0.9, 0.999, 1e-08, 0.01, 10


def _cp(n):
    return pltpu.CompilerParams(dimension_semantics=("arbitrary",) * n, vmem_limit_bytes=VMEM_LIMIT)


def _cp0():
    return pltpu.CompilerParams(vmem_limit_bytes=VMEM_LIMIT)


def _sig(v):
    return 1.0 / (1.0 + jnp.exp(-v))


def _dot(a, b):
    return jnp.dot(a, b, preferred_element_type=F32)


def _dot_nt(a, b):
    return lax.dot_general(a, b, (((1,), (1,)), ((), ())), preferred_element_type=F32)


def _dot_tn(a, b):
    return lax.dot_general(a, b, (((0,), (0,)), ((), ())), preferred_element_type=F32)


def _lane_tile(n, cap):
    t = (min(n, cap) // 128) * 128
    while t >= 128:
        if n % t == 0:
            return t
        t -= 128
    raise ValueError(f"no lane tile for {n}")


def _row_tile(rows, cols, max_elems):
    if rows * cols <= max_elems:
        return rows
    t = (min(rows, max(8, max_elems // cols)) // 8) * 8
    while t >= 8:
        if rows % t == 0:
            return t
        t -= 8
    raise ValueError(f"no row tile for {rows}x{cols}")


def mm_nn(a, b, shard, out_dtype, name):
    M, K = a.shape
    S, Kb, Nb = b.shape
    N = S * Nb if shard == 'n' else Nb
    assert K == (Kb if shard == 'n' else S * Kb)
    tm, tn, tk = _lane_tile(M, 768), _lane_tile(Nb, 1536), _lane_tile(Kb, 2048)
    nk = K // tk
    if shard == 'n':
        tps = Nb // tn
        b_map = lambda i, j, k: (j // tps, k, j % tps)
    else:
        kps = Kb // tk
        b_map = lambda i, j, k: (k // kps, k % kps, j)

    def body(a_ref, b_ref, o_ref, acc_ref):
        k = pl.program_id(2)

        @pl.when(k == 0)
        def _():
            acc_ref[...] = jnp.zeros_like(acc_ref)

        acc_ref[...] += _dot(a_ref[...], b_ref[...])

        @pl.when(k == nk - 1)
        def _():
            o_ref[...] = acc_ref[...].astype(o_ref.dtype)

    return pl.pallas_call(
        body, grid=(M // tm, N // tn, nk),
        in_specs=[pl.BlockSpec((tm, tk), lambda i, j, k: (i, k)), pl.BlockSpec((None, tk, tn), b_map)],
        out_specs=pl.BlockSpec((tm, tn), lambda i, j, k: (i, j)),
        out_shape=jax.ShapeDtypeStruct((M, N), out_dtype),
        scratch_shapes=[pltpu.VMEM((tm, tn), F32)], compiler_params=_cp(3), name=name)(a, b)


def mm_nt(a, b, shard, out_dtype, name):
    M, N = a.shape
    S, Kb, Nb = b.shape
    K = Kb if shard == 'n' else S * Kb
    assert N == (S * Nb if shard == 'n' else Nb)
    tm, tko, tnr = _lane_tile(M, 768), _lane_tile(Kb, 1536), _lane_tile(Nb, 1536)
    nr = N // tnr
    if shard == 'n':
        tps = Nb // tnr
        b_map = lambda i, j, r: (r // tps, j, r % tps)
    else:
        kps = Kb // tko
        b_map = lambda i, j, r: (j // kps, j % kps, r)

    def body(a_ref, b_ref, o_ref, acc_ref):
        r = pl.program_id(2)

        @pl.when(r == 0)
        def _():
            acc_ref[...] = jnp.zeros_like(acc_ref)

        acc_ref[...] += _dot_nt(a_ref[...], b_ref[...])

        @pl.when(r == nr - 1)
        def _():
            o_ref[...] = acc_ref[...].astype(o_ref.dtype)

    return pl.pallas_call(
        body, grid=(M // tm, K // tko, nr),
        in_specs=[pl.BlockSpec((tm, tnr), lambda i, j, r: (i, r)), pl.BlockSpec((None, tko, tnr), b_map)],
        out_specs=pl.BlockSpec((tm, tko), lambda i, j, r: (i, j)),
        out_shape=jax.ShapeDtypeStruct((M, K), out_dtype),
        scratch_shapes=[pltpu.VMEM((tm, tko), F32)], compiler_params=_cp(3), name=name)(a, b)


def mm_tn(a, c, S, shard, name):
    M, K = a.shape
    N = c.shape[1]
    Kb, Nb = (K, N // S) if shard == 'n' else (K // S, N)
    tm, tk, tn = _lane_tile(M, 768), _lane_tile(Kb, 1536), _lane_tile(Nb, 1536)
    nm = M // tm
    if shard == 'n':
        tps = Nb // tn
        o_map = lambda i, j, m: (j // tps, i, j % tps)
    else:
        kps = Kb // tk
        o_map = lambda i, j, m: (i // kps, i % kps, j)

    def body(a_ref, c_ref, o_ref, acc_ref):
        m = pl.program_id(2)

        @pl.when(m == 0)
        def _():
            acc_ref[...] = jnp.zeros_like(acc_ref)

        acc_ref[...] += _dot_tn(a_ref[...], c_ref[...])

        @pl.when(m == nm - 1)
        def _():
            o_ref[...] = acc_ref[...]

    return pl.pallas_call(
        body, grid=(K // tk, N // tn, nm),
        in_specs=[pl.BlockSpec((tm, tk), lambda i, j, m: (m, i)), pl.BlockSpec((tm, tn), lambda i, j, m: (m, j))],
        out_specs=pl.BlockSpec((None, tk, tn), o_map),
        out_shape=jax.ShapeDtypeStruct((S, Kb, Nb), F32),
        scratch_shapes=[pltpu.VMEM((tk, tn), F32)], compiler_params=_cp(3), name=name)(a, c)


def _mod_spec(D, ncr):
    return pl.BlockSpec((None, N_MOD, D), lambda i: (jnp.where(i < ncr, 0, 1), 0, 0))


def norm_mod(x, g, modv, i_sh, i_sc, ncr, name):
    T, D = x.shape

    def body(x_ref, g_ref, m_ref, h_ref):
        xv = x_ref[...]
        r = lax.rsqrt(jnp.mean(xv * xv, axis=-1, keepdims=True) + EPS)
        n = xv * r * g_ref[...]
        h_ref[...] = (n * (1.0 + m_ref[i_sc:i_sc + 1, :]) + m_ref[i_sh:i_sh + 1, :]).astype(BF16)

    return pl.pallas_call(
        body, grid=(T // R,),
        in_specs=[pl.BlockSpec((R, D), lambda i: (i, 0)), pl.BlockSpec((1, D), lambda i: (0, 0)), _mod_spec(D, ncr)],
        out_specs=pl.BlockSpec((R, D), lambda i: (i, 0)),
        out_shape=jax.ShapeDtypeStruct((T, D), BF16), compiler_params=_cp(1), name=name)(x, g, modv)


def norm_mod_bwd(x, g, modv, dh, dres, i_sc, ncr, name):
    T, D = x.shape

    def body(x_ref, g_ref, m_ref, dh_ref, dr_ref, dx_ref, acc_ref):
        i = pl.program_id(0)

        @pl.when(i == 0)
        def _():
            acc_ref[...] = jnp.zeros_like(acc_ref)

        xv = x_ref[...]
        r = lax.rsqrt(jnp.mean(xv * xv, axis=-1, keepdims=True) + EPS)
        xh = xv * r
        gv = g_ref[...]
        dhv = dh_ref[...]
        dn = dhv * (1.0 + m_ref[i_sc:i_sc + 1, :])
        s_sh = jnp.sum(dhv, axis=0, keepdims=True)
        s_sc = jnp.sum(dhv * (xh * gv), axis=0, keepdims=True)
        acc_ref[4:5, :] += jnp.sum(dn * xh, axis=0, keepdims=True)
        dxh = dn * gv
        dx_ref[...] = dr_ref[...] + r * (dxh - xh * jnp.mean(dxh * xh, axis=-1, keepdims=True))

        @pl.when(i < ncr)
        def _():
            acc_ref[0:1, :] += s_sh
            acc_ref[1:2, :] += s_sc

        @pl.when(i >= ncr)
        def _():
            acc_ref[2:3, :] += s_sh
            acc_ref[3:4, :] += s_sc

    row = pl.BlockSpec((R, D), lambda i: (i, 0))
    return pl.pallas_call(
        body, grid=(T // R,),
        in_specs=[row, pl.BlockSpec((1, D), lambda i: (0, 0)), _mod_spec(D, ncr), row, row],
        out_specs=[row, pl.BlockSpec((8, D), lambda i: (0, 0))],
        out_shape=[jax.ShapeDtypeStruct((T, D), F32), jax.ShapeDtypeStruct((8, D), F32)],
        compiler_params=_cp(1), name=name)(x, g, modv, dh, dres)


def gate_res(x, y, modv, i_g, ncr, name):
    T, D = x.shape

    def body(x_ref, y_ref, m_ref, o_ref):
        o_ref[...] = x_ref[...] + m_ref[i_g:i_g + 1, :] * y_ref[...]

    row = pl.BlockSpec((R, D), lambda i: (i, 0))
    return pl.pallas_call(
        body, grid=(T // R,), in_specs=[row, row, _mod_spec(D, ncr)], out_specs=row,
        out_shape=jax.ShapeDtypeStruct((T, D), F32), compiler_params=_cp(1), name=name)(x, y, modv)


def gate_res_bwd(dx, y, modv, i_g, ncr, name):
    T, D = dx.shape

    def body(dx_ref, y_ref, m_ref, dy_ref, acc_ref):
        i = pl.program_id(0)

        @pl.when(i == 0)
        def _():
            acc_ref[...] = jnp.zeros_like(acc_ref)

        dxv = dx_ref[...]
        dy_ref[...] = (m_ref[i_g:i_g + 1, :] * dxv).astype(BF16)
        s = jnp.sum(dxv * y_ref[...], axis=0, keepdims=True)

        @pl.when(i < ncr)
        def _():
            acc_ref[0:1, :] += s

        @pl.when(i >= ncr)
        def _():
            acc_ref[1:2, :] += s

    row = pl.BlockSpec((R, D), lambda i: (i, 0))
    return pl.pallas_call(
        body, grid=(T // R,), in_specs=[row, row, _mod_spec(D, ncr)],
        out_specs=[row, pl.BlockSpec((8, D), lambda i: (0, 0))],
        out_shape=[jax.ShapeDtypeStruct((T, D), BF16), jax.ShapeDtypeStruct((8, D), F32)],
        compiler_params=_cp(1), name=name)(dx, y, modv)


def loss_head(x, target, g, ncr, name):
    T, D = x.shape

    def body(x_ref, t_ref, g_ref, dx_ref, acc_ref):
        i = pl.program_id(0)

        @pl.when(i == 0)
        def _():
            acc_ref[...] = jnp.zeros_like(acc_ref)

        @pl.when(i < ncr)
        def _():
            dx_ref[...] = jnp.zeros_like(dx_ref)

        @pl.when(i >= ncr)
        def _():
            xv = x_ref[...]
            r = lax.rsqrt(jnp.mean(xv * xv, axis=-1, keepdims=True) + EPS)
            xh = xv * r
            gv = g_ref[...]
            e = xh * gv - t_ref[...]
            acc_ref[1:2, :] += jnp.sum(e * e, axis=0, keepdims=True)
            dy = e * (1.0 / D)
            acc_ref[0:1, :] += jnp.sum(dy * xh, axis=0, keepdims=True)
            dxh = dy * gv
            dx_ref[...] = r * (dxh - xh * jnp.mean(dxh * xh, axis=-1, keepdims=True))

    row = pl.BlockSpec((R, D), lambda i: (i, 0))
    return pl.pallas_call(
        body, grid=(T // R,),
        in_specs=[row, pl.BlockSpec((R, D), lambda i: (jnp.maximum(i - ncr, 0), 0)), pl.BlockSpec((1, D), lambda i: (0, 0))],
        out_specs=[row, pl.BlockSpec((8, D), lambda i: (0, 0))],
        out_shape=[jax.ShapeDtypeStruct((T, D), F32), jax.ShapeDtypeStruct((8, D), F32)],
        compiler_params=_cp(1), name=name)(x, target, g)


def _conv31_specs(T, cbk, a0, g0):
    nh = R // HALO
    pv = lambda i: jnp.maximum(i * nh - 1, 0)
    nx = lambda i: jnp.minimum((i + 1) * nh, T // HALO - 1)
    return pv, nx


def conv31_fwd(p, w32, b, RW, CW, ncr, name):
    T = p.shape[0]
    nT, cbk = T // R, LANES
    a0, g0 = 4 * RW // cbk, (4 * RW + CW) // cbk
    pv, nx = _conv31_specs(T, cbk, a0, g0)

    def body(a, g, ap, gp, an, gn, w, bb, uc, ext):
        i = pl.program_id(0)
        has_prev = jnp.logical_and(i != 0, i != ncr)
        has_next = jnp.logical_and(i != ncr - 1, i != nT - 1)
        glu = lambda u, v: u.astype(F32) * _sig(v.astype(F32))
        ext[0:HALO, :] = jnp.where(has_prev, glu(ap[...], gp[...]), 0.0)
        ext[HALO:HALO + R, :] = glu(a[...], g[...])
        ext[HALO + R:, :] = jnp.where(has_next, glu(an[...], gn[...]), 0.0)
        for r0 in range(0, R, SUB):
            acc = jnp.broadcast_to(bb[...], (SUB, cbk))
            for k in range(CONV_K):
                acc = acc + w[k:k + 1, :] * ext[pl.ds(r0 + 1 + k, SUB), :]
            uc[r0:r0 + SUB, :] = acc

    cur = lambda c0: pl.BlockSpec((R, cbk), lambda i, j: (i, c0 + j))
    hp = lambda c0: pl.BlockSpec((HALO, cbk), lambda i, j: (pv(i), c0 + j))
    hn = lambda c0: pl.BlockSpec((HALO, cbk), lambda i, j: (nx(i), c0 + j))
    return pl.pallas_call(
        body, grid=(nT, CW // cbk),
        in_specs=[cur(a0), cur(g0), hp(a0), hp(g0), hn(a0), hn(g0),
                  pl.BlockSpec((32, cbk), lambda i, j: (0, j)), pl.BlockSpec((1, cbk), lambda i, j: (0, j))],
        out_specs=pl.BlockSpec((R, cbk), lambda i, j: (i, j)),
        out_shape=jax.ShapeDtypeStruct((T, CW), F32),
        scratch_shapes=[pltpu.VMEM((R + 2 * HALO, cbk), F32)], compiler_params=_cp(2), name=name)(p, p, p, p, p, p, w32, b)


def conv31_bwd(duc, p, w32, RW, CW, ncr, name):
    T = p.shape[0]
    nT, cbk = T // R, LANES
    a0, g0 = 4 * RW // cbk, (4 * RW + CW) // cbk
    pv, nx = _conv31_specs(T, cbk, a0, g0)

    def body(d, dp_, dn_, a, g, ap, gp, an, gn, w, da, dg, accw, uext, dext):
        i = pl.program_id(1)

        @pl.when(i == 0)
        def _():
            accw[...] = jnp.zeros_like(accw)

        has_prev = jnp.logical_and(i != 0, i != ncr)
        has_next = jnp.logical_and(i != ncr - 1, i != nT - 1)
        glu = lambda u, v: u.astype(F32) * _sig(v.astype(F32))
        uext[0:HALO, :] = jnp.where(has_prev, glu(ap[...], gp[...]), 0.0)
        uext[HALO:HALO + R, :] = glu(a[...], g[...])
        uext[HALO + R:, :] = jnp.where(has_next, glu(an[...], gn[...]), 0.0)
        dext[0:HALO, :] = jnp.where(has_prev, dp_[...], 0.0)
        dext[HALO:HALO + R, :] = d[...]
        dext[HALO + R:, :] = jnp.where(has_next, dn_[...], 0.0)
        for r0 in range(0, R, SUB):
            du = jnp.zeros((SUB, cbk), F32)
            for k in range(CONV_K):
                du = du + w[k:k + 1, :] * dext[pl.ds(r0 + HALO + 15 - k, SUB), :]
            av = a[r0:r0 + SUB, :].astype(F32)
            sg = _sig(g[r0:r0 + SUB, :].astype(F32))
            da[r0:r0 + SUB, :] = (du * sg).astype(BF16)
            dg[r0:r0 + SUB, :] = (du * av * sg * (1.0 - sg)).astype(BF16)
            dcur = d[r0:r0 + SUB, :]
            for k in range(CONV_K):
                accw[k:k + 1, :] += jnp.sum(dcur * uext[pl.ds(r0 + 1 + k, SUB), :], axis=0, keepdims=True)
            accw[31:32, :] += jnp.sum(dcur, axis=0, keepdims=True)

    cur = lambda c0: pl.BlockSpec((R, cbk), lambda j, i: (i, c0 + j))
    hp = lambda c0: pl.BlockSpec((HALO, cbk), lambda j, i: (pv(i), c0 + j))
    hn = lambda c0: pl.BlockSpec((HALO, cbk), lambda j, i: (nx(i), c0 + j))
    out = pl.BlockSpec((R, cbk), lambda j, i: (i, j))
    return pl.pallas_call(
        body, grid=(CW // cbk, nT),
        in_specs=[cur(0), hp(0), hn(0), cur(a0), cur(g0), hp(a0), hp(g0), hn(a0), hn(g0),
                  pl.BlockSpec((32, cbk), lambda j, i: (0, j))],
        out_specs=[out, out, pl.BlockSpec((32, cbk), lambda j, i: (0, j))],
        out_shape=[jax.ShapeDtypeStruct((T, CW), BF16), jax.ShapeDtypeStruct((T, CW), BF16),
                   jax.ShapeDtypeStruct((32, CW), F32)],
        scratch_shapes=[pltpu.VMEM((R + 2 * HALO, cbk), F32), pltpu.VMEM((R + 2 * HALO, cbk), F32)],
        compiler_params=_cp(2), name=name)(duc, duc, duc, p, p, p, p, p, p, w32)


def _rope(v, cosv, sinv, first):
    swapped = jnp.where(first, pltpu.roll(v, 96, 1), pltpu.roll(v, 32, 1))
    return v * cosv + swapped * sinv


def _unrope(v, cosv, sinv, first):
    z = v * sinv
    return v * cosv + jnp.where(first, pltpu.roll(z, 96, 1), pltpu.roll(z, 32, 1))


def _decay_tables(lg_ref, H, DM, QD, KD, CD):
    n = lax.broadcasted_iota(jnp.int32, (RC, DH), 0).astype(F32)
    m = lax.broadcasted_iota(jnp.int32, (RC, DH), 1).astype(F32)
    for d in range(2):
        for h in range(H):
            i = d * H + h
            lg = lg_ref[i:i + 1, :]
            diff = (n - m) if d == 0 else (m - n)
            DM[i] = jnp.where(diff >= 0, jnp.exp(lg * jnp.maximum(diff, 0.0)), 0.0)
            QD[i] = jnp.exp(lg * ((n + 1.0) if d == 0 else (RC - n)))
            KD[i] = jnp.exp(lg * ((RC - 1.0 - n) if d == 0 else n))
            CD[i] = jnp.exp(lg * float(RC)) + jnp.zeros((RC, DH), F32)


def _chunk_orders(NC, ncc):
    cf = lambda s: s
    cb = lambda s: jnp.where(s < ncc, ncc - 1 - s, NC - 1 - (s - ncc))
    return cf, cb


def ret_fwd(p, cosT, sinT, lgt, H, ncc, name):
    T = p.shape[0]
    RW, NC = H * DH, T // RC
    cf, cb = _chunk_orders(NC, ncc)
    scale = DH ** -0.5

    def body(qf, kf, vf, qb, kb, vb, cosf, sinf, cosb, sinb, lg_ref, of_ref, ob_ref, sf_ref, sb_ref, S, DM, QD, KD, CD):
        s = pl.program_id(0)

        @pl.when(s == 0)
        def _():
            S[...] = jnp.zeros_like(S)
            _decay_tables(lg_ref, H, DM, QD, KD, CD)

        first = (lax.broadcasted_iota(jnp.int32, (RC, DH), 1) % 64) < 32
        for d, (q_ref, k_ref, v_ref, c_ref, s_ref, o_ref, st_ref) in enumerate(
                ((qf, kf, vf, cosf, sinf, of_ref, sf_ref), (qb, kb, vb, cosb, sinb, ob_ref, sb_ref))):
            cosv, sinv = c_ref[...], s_ref[...]
            for h in range(H):
                hs, i = slice(h * DH, (h + 1) * DH), d * H + h
                q16 = _rope(q_ref[:, hs].astype(F32), cosv, sinv, first).astype(BF16)
                k = _rope(k_ref[:, hs].astype(F32), cosv, sinv, first) * scale
                k16 = k.astype(BF16)
                v = v_ref[:, hs]
                s_in = S[i]
                s16 = s_in.astype(BF16)
                st_ref[h] = s16
                sc = _dot_nt(q16, k16) * DM[i]
                o_ref[:, hs] = _dot(sc.astype(BF16), v) + _dot(q16, s16) * QD[i]
                S[i] = s_in * CD[i] + _dot_tn((k * KD[i]).astype(BF16), v)

    pspec = lambda col, cm: pl.BlockSpec((RC, RW), lambda s: (cm(s), col))
    tspec = lambda cm: pl.BlockSpec((RC, DH), lambda s: (cm(s), 0))
    ospec = lambda cm: pl.BlockSpec((RC, RW), lambda s: (cm(s), 0))
    stspec = pl.BlockSpec((None, H, DH, DH), lambda s: (s, 0, 0, 0))
    tab = pltpu.VMEM((2 * H, RC, DH), F32)
    return pl.pallas_call(
        body, grid=(NC,),
        in_specs=[pspec(0, cf), pspec(1, cf), pspec(2, cf), pspec(0, cb), pspec(1, cb), pspec(2, cb),
                  tspec(cf), tspec(cf), tspec(cb), tspec(cb), pl.BlockSpec((2 * H, DH), lambda s: (0, 0))],
        out_specs=[ospec(cf), ospec(cb), stspec, stspec],
        out_shape=[jax.ShapeDtypeStruct((T, RW), F32), jax.ShapeDtypeStruct((T, RW), F32),
                   jax.ShapeDtypeStruct((NC, H, DH, DH), BF16), jax.ShapeDtypeStruct((NC, H, DH, DH), BF16)],
        scratch_shapes=[tab, tab, tab, tab, tab], compiler_params=_cp(1), name=name)(
            p, p, p, p, p, p, cosT, sinT, cosT, sinT, lgt)


def ret_bwd(p, do, cosT, sinT, lgt, stf, stb, H, ncc, name):
    T = p.shape[0]
    RW, NC = H * DH, T // RC
    cf0, cb0 = _chunk_orders(NC, ncc)
    cf = lambda sp: cf0(NC - 1 - sp)
    cb = lambda sp: cb0(NC - 1 - sp)
    scale = DH ** -0.5

    def body(qf, kf, vf, qb, kb, vb, dof, dob, cosf, sinf, cosb, sinb, lg_ref, stf_ref, stb_ref,
             dqf, dkf, dvf, dqb, dkb, dvb, glg, dS, DM, QD, KD, CD):
        sp = pl.program_id(0)

        @pl.when(sp == 0)
        def _():
            dS[...] = jnp.zeros_like(dS)
            glg[...] = jnp.zeros_like(glg)
            _decay_tables(lg_ref, H, DM, QD, KD, CD)

        first = (lax.broadcasted_iota(jnp.int32, (RC, DH), 1) % 64) < 32
        n = lax.broadcasted_iota(jnp.int32, (RC, DH), 0).astype(F32)
        m = lax.broadcasted_iota(jnp.int32, (RC, DH), 1).astype(F32)
        for d, (q_ref, k_ref, v_ref, do_ref, c_ref, s_ref, st_ref, dq_ref, dk_ref, dv_ref) in enumerate(
                ((qf, kf, vf, dof, cosf, sinf, stf_ref, dqf, dkf, dvf),
                 (qb, kb, vb, dob, cosb, sinb, stb_ref, dqb, dkb, dvb))):
            cosv, sinv = c_ref[...], s_ref[...]
            diff = (n - m) if d == 0 else (m - n)
            posq = (n + 1.0) if d == 0 else (RC - n)
            posk = (RC - 1.0 - n) if d == 0 else n
            for h in range(H):
                hs, i = slice(h * DH, (h + 1) * DH), d * H + h
                q = _rope(q_ref[:, hs].astype(F32), cosv, sinv, first)
                k = _rope(k_ref[:, hs].astype(F32), cosv, sinv, first) * scale
                q16, k16 = q.astype(BF16), k.astype(BF16)
                v = v_ref[:, hs]
                s_in = st_ref[h]
                ds_out = dS[i]
                ds16 = ds_out.astype(BF16)
                do16 = do_ref[:, hs]
                doq = (do16.astype(F32) * QD[i]).astype(BF16)
                a = _dot_nt(q16, k16) * DM[i]
                da_raw = _dot_nt(do16, v)
                da16 = (da_raw * DM[i]).astype(BF16)
                dq_state = _dot_nt(doq, s_in)
                dk_state = _dot_nt(v, ds16) * KD[i]
                dqr = _dot(da16, k16) + dq_state
                dkr = _dot_tn(da16, q16) + dk_state
                dv_ref[:, hs] = _dot_tn(a.astype(BF16), do16) + _dot((k * KD[i]).astype(BF16), ds16)
                dS[i] = ds_out * CD[i] + _dot_tn(q16, doq)
                glg[i] += (da_raw * a * diff + posq * q * dq_state + posk * k * dk_state
                           + float(RC) * CD[i] * ds_out * s_in.astype(F32))
                dq_ref[:, hs] = _unrope(dqr, cosv, sinv, first)
                dk_ref[:, hs] = _unrope(dkr, cosv, sinv, first) * scale

    pspec = lambda col, cm: pl.BlockSpec((RC, RW), lambda s: (cm(s), col))
    tspec = lambda cm: pl.BlockSpec((RC, DH), lambda s: (cm(s), 0))
    ospec = lambda cm: pl.BlockSpec((RC, RW), lambda s: (cm(s), 0))
    stspec = pl.BlockSpec((None, H, DH, DH), lambda s: (NC - 1 - s, 0, 0, 0))
    tab = pltpu.VMEM((2 * H, RC, DH), F32)
    big = jax.ShapeDtypeStruct((T, RW), F32)
    return pl.pallas_call(
        body, grid=(NC,),
        in_specs=[pspec(0, cf), pspec(1, cf), pspec(2, cf), pspec(0, cb), pspec(1, cb), pspec(2, cb),
                  ospec(cf), ospec(cb), tspec(cf), tspec(cf), tspec(cb), tspec(cb),
                  pl.BlockSpec((2 * H, DH), lambda s: (0, 0)), stspec, stspec],
        out_specs=[ospec(cf), ospec(cf), ospec(cf), ospec(cb), ospec(cb), ospec(cb),
                   pl.BlockSpec((2 * H, RC, DH), lambda s: (0, 0, 0))],
        out_shape=[big, big, big, big, big, big, jax.ShapeDtypeStruct((2 * H, RC, DH), F32)],
        scratch_shapes=[tab, tab, tab, tab, tab], compiler_params=_cp(1), name=name)(
            p, p, p, p, p, p, do, do, cosT, sinT, cosT, sinT, lgt, stf, stb)


def mix_fwd(o_f, o_b, p, uc, lng, lnb, H, name):
    T, RW = o_f.shape
    CW = uc.shape[1]

    def body(of_ref, ob_ref, g_ref, uc_ref, lg_ref, lb_ref, out_ref):
        for h in range(H):
            hs = slice(h * DH, (h + 1) * DH)
            o = of_ref[:, hs] + ob_ref[:, hs]
            on = o * lax.rsqrt(jnp.mean(o * o, axis=-1, keepdims=True) + EPS)
            gv = g_ref[:, hs].astype(F32)
            out_ref[:, hs] = (gv * _sig(gv) * on).astype(BF16)
        u = uc_ref[...]
        mu = jnp.mean(u, axis=-1, keepdims=True)
        var = jnp.mean(jnp.square(u - mu), axis=-1, keepdims=True)
        z = (u - mu) * lax.rsqrt(var + EPS) * lg_ref[...] + lb_ref[...]
        out_ref[:, RW:] = (z * _sig(z)).astype(BF16)

    rw = pl.BlockSpec((R, RW), lambda i: (i, 0))
    vec = pl.BlockSpec((1, CW), lambda i: (0, 0))
    return pl.pallas_call(
        body, grid=(T // R,),
        in_specs=[rw, rw, pl.BlockSpec((R, RW), lambda i: (i, 3)), pl.BlockSpec((R, CW), lambda i: (i, 0)), vec, vec],
        out_specs=pl.BlockSpec((R, RW + CW), lambda i: (i, 0)),
        out_shape=jax.ShapeDtypeStruct((T, RW + CW), BF16), compiler_params=_cp(1), name=name)(o_f, o_b, p, uc, lng, lnb)


def mix_bwd(dmix, o_f, o_b, p, uc, lng, lnb, H, name):
    T, RW = o_f.shape
    CW = uc.shape[1]

    def body(dm_ref, of_ref, ob_ref, g_ref, uc_ref, lg_ref, lb_ref, do_ref, dg_ref, duc_ref, acc_ref):
        i = pl.program_id(0)

        @pl.when(i == 0)
        def _():
            acc_ref[...] = jnp.zeros_like(acc_ref)

        for h in range(H):
            hs = slice(h * DH, (h + 1) * DH)
            o = of_ref[:, hs] + ob_ref[:, hs]
            r = lax.rsqrt(jnp.mean(o * o, axis=-1, keepdims=True) + EPS)
            on = o * r
            gv = g_ref[:, hs].astype(F32)
            sg = _sig(gv)
            dmr = dm_ref[:, hs].astype(F32)
            dg_ref[:, hs] = (dmr * on * (sg * (1.0 + gv * (1.0 - sg)))).astype(BF16)
            don = dmr * (gv * sg)
            do_ref[:, hs] = (r * (don - on * jnp.mean(don * on, axis=-1, keepdims=True))).astype(BF16)
        u = uc_ref[...]
        mu = jnp.mean(u, axis=-1, keepdims=True)
        rs = lax.rsqrt(jnp.mean(jnp.square(u - mu), axis=-1, keepdims=True) + EPS)
        zh = (u - mu) * rs
        lg = lg_ref[...]
        z = zh * lg + lb_ref[...]
        sz = _sig(z)
        dz = dm_ref[:, RW:].astype(F32) * (sz * (1.0 + z * (1.0 - sz)))
        acc_ref[0:1, :] += jnp.sum(dz * zh, axis=0, keepdims=True)
        acc_ref[1:2, :] += jnp.sum(dz, axis=0, keepdims=True)
        dzh = dz * lg
        duc_ref[...] = rs * (dzh - jnp.mean(dzh, axis=-1, keepdims=True)
                             - zh * jnp.mean(dzh * zh, axis=-1, keepdims=True))

    rw = pl.BlockSpec((R, RW), lambda i: (i, 0))
    cw = pl.BlockSpec((R, CW), lambda i: (i, 0))
    vec = pl.BlockSpec((1, CW), lambda i: (0, 0))
    return pl.pallas_call(
        body, grid=(T // R,),
        in_specs=[pl.BlockSpec((R, RW + CW), lambda i: (i, 0)), rw, rw, pl.BlockSpec((R, RW), lambda i: (i, 3)), cw, vec, vec],
        out_specs=[rw, rw, cw, pl.BlockSpec((8, CW), lambda i: (0, 0))],
        out_shape=[jax.ShapeDtypeStruct((T, RW), BF16), jax.ShapeDtypeStruct((T, RW), BF16),
                   jax.ShapeDtypeStruct((T, CW), F32), jax.ShapeDtypeStruct((8, CW), F32)],
        compiler_params=_cp(1), name=name)(dmix, o_f, o_b, p, uc, lng, lnb)


def assemble_dp(dqf, dqb, dkf, dkb, dvf, dvb, dg, da, dbg, name):
    T, RW = dqf.shape
    CW = da.shape[1]

    def body(qf, qb, kf, kb, vf, vb, g, a, b, out):
        out[:, 0:RW] = (qf[...] + qb[...]).astype(BF16)
        out[:, RW:2 * RW] = (kf[...] + kb[...]).astype(BF16)
        out[:, 2 * RW:3 * RW] = (vf[...] + vb[...]).astype(BF16)
        out[:, 3 * RW:4 * RW] = g[...]
        out[:, 4 * RW:4 * RW + CW] = a[...]
        out[:, 4 * RW + CW:] = b[...]

    rw = pl.BlockSpec((R, RW), lambda i: (i, 0))
    cw = pl.BlockSpec((R, CW), lambda i: (i, 0))
    W = 4 * RW + 2 * CW
    return pl.pallas_call(
        body, grid=(T // R,), in_specs=[rw] * 7 + [cw, cw], out_specs=pl.BlockSpec((R, W), lambda i: (i, 0)),
        out_shape=jax.ShapeDtypeStruct((T, W), BF16), compiler_params=_cp(1), name=name)(
            dqf, dqb, dkf, dkb, dvf, dvb, dg, da, dbg)


def _fill_ext(ext, cur, prv, nxt, has_prev, has_next):
    cb = ext.shape[1]
    ext[0:8, :] = jnp.zeros((8, cb), F32)
    ext[FOFF + R + GRID_W:, :] = jnp.zeros((8, cb), F32)
    ext[8:FOFF, :] = jnp.where(has_prev, prv[...].astype(F32), 0.0)
    ext[FOFF:FOFF + R, :] = cur[...].astype(F32)
    ext[FOFF + R:FOFF + R + GRID_W, :] = jnp.where(has_next, nxt[...].astype(F32), 0.0)


def _taps(is_ctx):
    return [(dr, dc) for dr in ((0,) if is_ctx else (-1, 0, 1)) for dc in (-1, 0, 1)]


def _tap_src(ext, r0, c0, off, shift, col, is_ctx):
    src = ext[pl.ds(FOFF + r0 + off, SUB), pl.ds(c0, LANES)]
    if not is_ctx and shift == -1:
        src = jnp.where(col >= 1, src, 0.0)
    elif not is_ctx and shift == 1:
        src = jnp.where(col <= GRID_W - 2, src, 0.0)
    return src


def _conv9(ext, w, r0, c0, col, is_ctx, flip):
    acc = jnp.zeros((SUB, LANES), F32)
    for dr, dc in _taps(is_ctx):
        widx = (dr + 1) * 3 + dc + 1
        off = GRID_W * dr + dc
        src = _tap_src(ext, r0, c0, -off if flip else off, -dc if flip else dc, col, is_ctx)
        acc = acc + w[widx:widx + 1, pl.ds(c0, LANES)] * src
    return acc


def _ffn_specs(T, cb, order):
    nq = R // GRID_W
    pv = lambda i: jnp.maximum(i * nq - 1, 0)
    nx = lambda i: jnp.minimum((i + 1) * nq, T // GRID_W - 1)
    if order == 'ij':
        mk = lambda blk, rf, c0: pl.BlockSpec(blk, lambda i, j: (rf(i), c0 + j))
    else:
        mk = lambda blk, rf, c0: pl.BlockSpec(blk, lambda j, i: (rf(i), c0 + j))
    cur = lambda c0: mk((R, cb), lambda i: i, c0)
    hp = lambda c0: mk((GRID_W, cb), pv, c0)
    hn = lambda c0: mk((GRID_W, cb), nx, c0)
    vec = lambda rows: mk((rows, cb), lambda i: 0, 0)
    return cur, hp, hn, vec


def _ffn_flags(i, ncr, nT):
    return jnp.logical_and(i > ncr, True), jnp.logical_and(i >= ncr, i != nT - 1)


def ffn_act(up, w16, b, CF, ncr, name):
    T = up.shape[0]
    nT, cb = T // R, 2 * LANES
    ncb = CF // cb
    cur, hp, hn, vec = _ffn_specs(T, cb, 'ij')

    def body(g, v, gp, gn, w, bb, out, ext):
        i = pl.program_id(0)
        has_prev, has_next = _ffn_flags(i, ncr, nT)
        _fill_ext(ext, g, gp, gn, has_prev, has_next)
        col = lax.broadcasted_iota(jnp.int32, (SUB, LANES), 0)

        def run(is_ctx):
            for r0 in range(0, R, SUB):
                for c0 in range(0, cb, LANES):
                    gc = _conv9(ext, w, r0, c0, col, is_ctx, False) + bb[:, pl.ds(c0, LANES)]
                    val = v[r0:r0 + SUB, c0:c0 + LANES].astype(F32)
                    out[r0:r0 + SUB, c0:c0 + LANES] = (gc * _sig(gc) * val).astype(BF16)

        pl.when(i < ncr)(lambda: run(True))
        pl.when(i >= ncr)(lambda: run(False))

    return pl.pallas_call(
        body, grid=(nT, ncb), in_specs=[cur(0), cur(ncb), hp(0), hn(0), vec(16), vec(1)], out_specs=cur(0),
        out_shape=jax.ShapeDtypeStruct((T, CF), BF16),
        scratch_shapes=[pltpu.VMEM((R + 2 * GRID_W + 16, cb), F32)], compiler_params=_cp(2), name=name)(
            up, up, up, up, w16, b)


def ffn_act_bwd1(up, dact, w16, b, CF, ncr, name):
    T = up.shape[0]
    nT, cb = T // R, 2 * LANES
    ncb = CF // cb
    cur, hp, hn, vec = _ffn_specs(T, cb, 'ji')

    def body(g, v, gp, gn, da, w, bb, dgc, dup, accb, ext):
        i = pl.program_id(1)

        @pl.when(i == 0)
        def _():
            accb[...] = jnp.zeros_like(accb)

        has_prev, has_next = _ffn_flags(i, ncr, nT)
        _fill_ext(ext, g, gp, gn, has_prev, has_next)
        col = lax.broadcasted_iota(jnp.int32, (SUB, LANES), 0)

        def run(is_ctx):
            for r0 in range(0, R, SUB):
                for c0 in range(0, cb, LANES):
                    gc = _conv9(ext, w, r0, c0, col, is_ctx, False) + bb[:, pl.ds(c0, LANES)]
                    sg = _sig(gc)
                    val = v[r0:r0 + SUB, c0:c0 + LANES].astype(F32)
                    dav = da[r0:r0 + SUB, c0:c0 + LANES].astype(F32)
                    dup[r0:r0 + SUB, c0:c0 + LANES] = (dav * gc * sg).astype(BF16)
                    d = dav * val * (sg * (1.0 + gc * (1.0 - sg)))
                    dgc[r0:r0 + SUB, c0:c0 + LANES] = d.astype(BF16)
                    accb[0:1, pl.ds(c0, LANES)] += jnp.sum(d, axis=0, keepdims=True)

        pl.when(i < ncr)(lambda: run(True))
        pl.when(i >= ncr)(lambda: run(False))

    return pl.pallas_call(
        body, grid=(ncb, nT), in_specs=[cur(0), cur(ncb), hp(0), hn(0), cur(0), vec(16), vec(1)],
        out_specs=[cur(0), cur(ncb), vec(8)],
        out_shape=[jax.ShapeDtypeStruct((T, CF), BF16), jax.ShapeDtypeStruct((T, 2 * CF), BF16),
                   jax.ShapeDtypeStruct((8, CF), F32)],
        scratch_shapes=[pltpu.VMEM((R + 2 * GRID_W + 16, cb), F32)], compiler_params=_cp(2), name=name)(
            up, up, up, up, dact, w16, b)


def ffn_act_bwd2(dgc, up, dup, w16, CF, ncr, name):
    T = up.shape[0]
    nT, cb = T // R, 2 * LANES
    ncb = CF // cb
    cur, hp, hn, vec = _ffn_specs(T, cb, 'ji')

    def body(d, dp_, dn_, g, gp, gn, w, dup_in, dgate, accw, dext, gext):
        i = pl.program_id(1)

        @pl.when(i == 0)
        def _():
            accw[...] = jnp.zeros_like(accw)

        has_prev, has_next = _ffn_flags(i, ncr, nT)
        _fill_ext(dext, d, dp_, dn_, has_prev, has_next)
        _fill_ext(gext, g, gp, gn, has_prev, has_next)
        col = lax.broadcasted_iota(jnp.int32, (SUB, LANES), 0)

        def run(is_ctx):
            for r0 in range(0, R, SUB):
                for c0 in range(0, cb, LANES):
                    dgate[r0:r0 + SUB, c0:c0 + LANES] = _conv9(dext, w, r0, c0, col, is_ctx, True).astype(BF16)
                    dcur = dext[pl.ds(FOFF + r0, SUB), pl.ds(c0, LANES)]
                    for dr, dc in _taps(is_ctx):
                        widx = (dr + 1) * 3 + dc + 1
                        src = _tap_src(gext, r0, c0, GRID_W * dr + dc, dc, col, is_ctx)
                        accw[widx:widx + 1, pl.ds(c0, LANES)] += jnp.sum(dcur * src, axis=0, keepdims=True)

        pl.when(i < ncr)(lambda: run(True))
        pl.when(i >= ncr)(lambda: run(False))

    ext = pltpu.VMEM((R + 2 * GRID_W + 16, cb), F32)
    return pl.pallas_call(
        body, grid=(ncb, nT),
        in_specs=[cur(0), hp(0), hn(0), cur(0), hp(0), hn(0), vec(16), pl.BlockSpec(memory_space=pl.ANY)],
        out_specs=[cur(0), vec(16)],
        out_shape=[jax.ShapeDtypeStruct((T, 2 * CF), BF16), jax.ShapeDtypeStruct((16, CF), F32)],
        input_output_aliases={7: 0}, scratch_shapes=[ext, ext], compiler_params=_cp(2), name=name)(
            dgc, dgc, dgc, up, up, up, w16, dup)


def mod_fwd(cs, w_mod, name):
    L, D, Ns = w_mod.shape
    tn = _lane_tile(Ns, 768)

    def body(c_ref, w_ref, o_ref):
        cv = c_ref[...]
        o_ref[...] = _dot((cv * _sig(cv)).astype(BF16), w_ref[...].astype(BF16))

    return pl.pallas_call(
        body, grid=(L, Ns // tn),
        in_specs=[pl.BlockSpec((16, D), lambda l, j: (0, 0)), pl.BlockSpec((None, D, tn), lambda l, j: (l, 0, j))],
        out_specs=pl.BlockSpec((None, 16, tn), lambda l, j: (l, 0, j)),
        out_shape=jax.ShapeDtypeStruct((L, 16, Ns), F32), compiler_params=_cp(2), name=name)(cs, w_mod)


def mod_bwd(cs, w_mod, dmod, name):
    L, D, Ns = w_mod.shape
    tn = _lane_tile(Ns, 768)

    def body(c_ref, w_ref, dm_ref, gw_ref, ds_ref):
        @pl.when(jnp.logical_and(pl.program_id(0) == 0, pl.program_id(1) == 0))
        def _():
            ds_ref[...] = jnp.zeros_like(ds_ref)

        cv = c_ref[...]
        dm = dm_ref[...].astype(BF16)
        gw_ref[...] = _dot_tn((cv * _sig(cv)).astype(BF16), dm)
        ds_ref[...] += _dot_nt(dm, w_ref[...].astype(BF16))

    return pl.pallas_call(
        body, grid=(L, Ns // tn),
        in_specs=[pl.BlockSpec((16, D), lambda l, j: (0, 0)), pl.BlockSpec((None, D, tn), lambda l, j: (l, 0, j)),
                  pl.BlockSpec((None, 16, tn), lambda l, j: (l, 0, j))],
        out_specs=[pl.BlockSpec((None, D, tn), lambda l, j: (l, 0, j)), pl.BlockSpec((16, D), lambda l, j: (0, 0))],
        out_shape=[jax.ShapeDtypeStruct((L, D, Ns), F32), jax.ShapeDtypeStruct((16, D), F32)],
        compiler_params=_cp(2), name=name)(cs, w_mod, dmod)


def cast_bf16(w, name):
    L, Kb, Nb = w.shape
    w2 = w.reshape(L * Kb, Nb)
    tr = _row_tile(L * Kb, Nb, 1 << 19)

    def body(w_ref, o_ref):
        o_ref[...] = w_ref[...].astype(BF16)

    spec = pl.BlockSpec((tr, Nb), lambda i: (i, 0))
    out = pl.pallas_call(body, grid=(L * Kb // tr,), in_specs=[spec], out_specs=spec,
                         out_shape=jax.ShapeDtypeStruct((L * Kb, Nb), BF16), compiler_params=_cp(1), name=name)(w2)
    return out.reshape(L, Kb, Nb)


def add_half(dw, recv, c_idx, name):
    S, Kb, Nb = dw.shape
    Kh = Kb // 2
    tr = _row_tile(Kh, Nb, 1 << 18)
    nb = Kh // tr

    def body(c_ref, a_ref, b_ref, o_ref):
        o_ref[...] = a_ref[...] + b_ref[...]

    return pl.pallas_call(
        body,
        grid_spec=pltpu.PrefetchScalarGridSpec(
            num_scalar_prefetch=1, grid=(S, nb),
            in_specs=[pl.BlockSpec((None, tr, Nb), lambda s, i, c: (s, c[0] * nb + i, 0)),
                      pl.BlockSpec((None, tr, Nb), lambda s, i, c: (s, i, 0))],
            out_specs=pl.BlockSpec((None, tr, Nb), lambda s, i, c: (s, i, 0))),
        out_shape=jax.ShapeDtypeStruct((S, Kh, Nb), F32), compiler_params=_cp(2), name=name)(c_idx, dw, recv)


def add_shards(parts, name):
    S, Kh, Nb = parts.shape
    tr = _row_tile(Kh, Nb, 1 << 17)

    def body(p_ref, o_ref):
        acc = p_ref[0]
        for s in range(1, S):
            acc = acc + p_ref[s]
        o_ref[...] = acc

    return pl.pallas_call(
        body, grid=(Kh // tr,), in_specs=[pl.BlockSpec((S, tr, Nb), lambda i: (0, i, 0))],
        out_specs=pl.BlockSpec((tr, Nb), lambda i: (i, 0)),
        out_shape=jax.ShapeDtypeStruct((Kh, Nb), F32), compiler_params=_cp(1), name=name)(parts)


def adamw(w, g, m, v, name):
    shape = w.shape
    cols = shape[-1]
    rows = w.size // cols
    w2, g2, m2, v2 = (t.reshape(rows, cols) for t in (w, g, m, v))
    tr = _row_tile(rows, cols, 3 << 17)

    def body(w_ref, g_ref, m_ref, v_ref, d_ref, nm_ref, nv_ref):
        gv = g_ref[...]
        nm = ADAM_B1 * m_ref[...] + (1.0 - ADAM_B1) * gv
        nv = ADAM_B2 * v_ref[...] + (1.0 - ADAM_B2) * jnp.square(gv)
        m_hat = nm / (1.0 - ADAM_B1 ** ADAM_STEP)
        v_hat = nv / (1.0 - ADAM_B2 ** ADAM_STEP)
        d_ref[...] = -ADAM_LR * (m_hat / (jnp.sqrt(v_hat) + ADAM_EPS) + ADAM_WD * w_ref[...])
        nm_ref[...] = nm
        nv_ref[...] = nv

    spec = pl.BlockSpec((tr, cols), lambda i: (i, 0))
    sds = jax.ShapeDtypeStruct((rows, cols), F32)
    d, nm, nv = pl.pallas_call(body, grid=(rows // tr,), in_specs=[spec] * 4, out_specs=[spec] * 3,
                               out_shape=[sds, sds, sds], compiler_params=_cp(1), name=name)(w2, g2, m2, v2)
    return d.reshape(shape), nm.reshape(shape), nv.reshape(shape)


def _place():
    return lax.axis_index("x"), lax.axis_index("y"), lax.axis_index("c")


def gather_sum(v, name):
    r, cols = v.shape

    def body(v_ref, g_ref, s_ref, send_sems, recv_sems, local_sem):
        x, y, c = _place()
        me = 4 * x + 2 * y + c
        mine = pltpu.make_async_copy(v_ref, g_ref.at[me], local_sem)
        mine.start()
        sends, peers = [], []
        for k in range(1, 8):
            px = 1 - x if k & 4 else x
            py = 1 - y if k & 2 else y
            pc = 1 - c if k & 1 else c
            cp = pltpu.make_async_remote_copy(src_ref=v_ref, dst_ref=g_ref.at[me], send_sem=send_sems.at[k - 1],
                                              recv_sem=recv_sems.at[k - 1], device_id=(px, py, pc), device_id_type=MESH)
            cp.start()
            sends.append(cp)
            peers.append((px, py, pc))
        for k, (px, py, pc) in enumerate(peers):
            pltpu.make_async_remote_copy(src_ref=v_ref, dst_ref=g_ref.at[4 * px + 2 * py + pc], send_sem=send_sems.at[k],
                                         recv_sem=recv_sems.at[k], device_id=(px, py, pc), device_id_type=MESH).wait_recv()
        for cp in sends:
            cp.wait_send()
        mine.wait()
        acc = g_ref[0]
        for d in range(1, 8):
            acc = acc + g_ref[d]
        s_ref[...] = acc

    vm = pl.BlockSpec(memory_space=pltpu.VMEM)
    return pl.pallas_call(
        body, in_specs=[vm], out_specs=[vm, vm],
        out_shape=[jax.ShapeDtypeStruct((8, r, cols), F32), jax.ShapeDtypeStruct((r, cols), F32)],
        scratch_shapes=[pltpu.SemaphoreType.DMA((7,)), pltpu.SemaphoreType.DMA((7,)), pltpu.SemaphoreType.DMA],
        compiler_params=_cp0(), name=name)(v)


def _other_chips(x, y):
    return [(1 - x, y), (x, 1 - y), (1 - x, 1 - y)]


def gather_weights(wb, name):
    L, Kb, Nb = wb.shape
    Kh = Kb // 2

    def body(w_ref, *rest):
        outs, (send_sems, recv_sems, local_sems) = rest[:L], rest[L:]
        x, y, c = _place()
        jm = 2 * x + y
        sib = (x, y, 1 - c)
        chips = _other_chips(x, y)
        mine_rows, sib_rows = pl.ds(c * Kh, Kh), pl.ds((1 - c) * Kh, Kh)

        def cp(l, t, src, dst, to):
            return pltpu.make_async_remote_copy(src_ref=src, dst_ref=dst, send_sem=send_sems.at[7 * l + t],
                                                recv_sem=recv_sems.at[7 * l + t], device_id=to, device_id_type=MESH)

        started, local = [], []
        for l in range(L):
            src = w_ref.at[l, mine_rows, :]
            dst = outs[l].at[jm, mine_rows, :]
            lc = pltpu.make_async_copy(src, dst, local_sems.at[l])
            lc.start()
            local.append(lc)
            for t, (px, py) in enumerate(chips):
                started.append(cp(l, t, src, dst, (px, py, c)))
            started.append(cp(l, 3, src, dst, sib))
            for s in started[-4:]:
                s.start()
        for l in range(L):
            for t, (px, py) in enumerate(chips):
                blk = outs[l].at[2 * px + py, mine_rows, :]
                cp(l, t, blk, blk, (px, py, c)).wait_recv()
                fwd = cp(l, 4 + t, blk, blk, sib)
                fwd.start()
                started.append(fwd)
        for l in range(L):
            blk = outs[l].at[jm, sib_rows, :]
            cp(l, 3, blk, blk, sib).wait_recv()
            for t, (px, py) in enumerate(chips):
                blk = outs[l].at[2 * px + py, sib_rows, :]
                cp(l, 4 + t, blk, blk, sib).wait_recv()
        for s in started:
            s.wait_send()
        for lc in local:
            lc.wait()

    hbm = pl.BlockSpec(memory_space=pl.ANY)
    return pl.pallas_call(
        body, in_specs=[hbm], out_specs=[hbm] * L,
        out_shape=[jax.ShapeDtypeStruct((4, Kb, Nb), BF16)] * L,
        scratch_shapes=[pltpu.SemaphoreType.DMA((7 * L,)), pltpu.SemaphoreType.DMA((7 * L,)), pltpu.SemaphoreType.DMA((L,))],
        compiler_params=_cp0(), name=name)(wb)


def swap_halves(dw, name):
    S, Kb, Nb = dw.shape
    Kh = Kb // 2

    def body(d_ref, o_ref, send_sems, recv_sems):
        x, y, c = _place()
        sib = (x, y, 1 - c)
        cps = [pltpu.make_async_remote_copy(src_ref=d_ref.at[s, pl.ds((1 - c) * Kh, Kh), :], dst_ref=o_ref.at[s],
                                            send_sem=send_sems.at[s], recv_sem=recv_sems.at[s], device_id=sib,
                                            device_id_type=MESH) for s in range(S)]
        for cpy in cps:
            cpy.start()
        for cpy in cps:
            cpy.wait_recv()
        for cpy in cps:
            cpy.wait_send()

    hbm = pl.BlockSpec(memory_space=pl.ANY)
    return pl.pallas_call(
        body, in_specs=[hbm], out_specs=hbm, out_shape=jax.ShapeDtypeStruct((S, Kh, Nb), F32),
        scratch_shapes=[pltpu.SemaphoreType.DMA((S,)), pltpu.SemaphoreType.DMA((S,))],
        compiler_params=_cp0(), name=name)(dw)


def chip_exchange(a, name):
    S, Kh, Nb = a.shape

    def body(a_ref, o_ref, send_sems, recv_sems, local_sem):
        x, y, c = _place()
        jm = 2 * x + y
        chips = _other_chips(x, y)
        mine = pltpu.make_async_copy(a_ref.at[jm], o_ref.at[jm], local_sem)
        mine.start()
        cps = [pltpu.make_async_remote_copy(src_ref=a_ref.at[2 * px + py], dst_ref=o_ref.at[jm], send_sem=send_sems.at[t],
                                            recv_sem=recv_sems.at[t], device_id=(px, py, c), device_id_type=MESH)
               for t, (px, py) in enumerate(chips)]
        for cpy in cps:
            cpy.start()
        for t, (px, py) in enumerate(chips):
            blk = o_ref.at[2 * px + py]
            pltpu.make_async_remote_copy(src_ref=blk, dst_ref=blk, send_sem=send_sems.at[t], recv_sem=recv_sems.at[t],
                                         device_id=(px, py, c), device_id_type=MESH).wait_recv()
        for cpy in cps:
            cpy.wait_send()
        mine.wait()

    hbm = pl.BlockSpec(memory_space=pl.ANY)
    return pl.pallas_call(
        body, in_specs=[hbm], out_specs=hbm, out_shape=jax.ShapeDtypeStruct((S, Kh, Nb), F32),
        scratch_shapes=[pltpu.SemaphoreType.DMA((3,)), pltpu.SemaphoreType.DMA((3,)), pltpu.SemaphoreType.DMA],
        compiler_params=_cp0(), name=name)(a)


def join_halves(gh, name):
    Kh, Nb = gh.shape

    def body(g_ref, o_ref, send_sem, recv_sem, local_sem):
        x, y, c = _place()
        sib = (x, y, 1 - c)
        rows = o_ref.at[pl.ds(c * Kh, Kh), :]
        mine = pltpu.make_async_copy(g_ref, rows, local_sem)
        mine.start()
        cpy = pltpu.make_async_remote_copy(src_ref=g_ref, dst_ref=rows, send_sem=send_sem, recv_sem=recv_sem,
                                           device_id=sib, device_id_type=MESH)
        cpy.start()
        other = o_ref.at[pl.ds((1 - c) * Kh, Kh), :]
        pltpu.make_async_remote_copy(src_ref=g_ref, dst_ref=other, send_sem=send_sem, recv_sem=recv_sem,
                                     device_id=sib, device_id_type=MESH).wait_recv()
        cpy.wait_send()
        mine.wait()

    hbm = pl.BlockSpec(memory_space=pl.ANY)
    return pl.pallas_call(
        body, in_specs=[hbm], out_specs=hbm, out_shape=jax.ShapeDtypeStruct((2 * Kh, Nb), F32),
        scratch_shapes=[pltpu.SemaphoreType.DMA, pltpu.SemaphoreType.DMA, pltpu.SemaphoreType.DMA],
        compiler_params=_cp0(), name=name)(gh)


def reduce_weight_grad(dw, c_idx, tag):
    recv = swap_halves(dw, f"rs_swap_{tag}")
    chip_sum = add_half(dw, recv, c_idx, f"rs_add_half_{tag}")
    parts = chip_exchange(chip_sum, f"rs_exchange_{tag}")
    return join_halves(add_shards(parts, f"rs_add_shards_{tag}"), f"rs_join_{tag}")


def _pack(parts):
    flat = jnp.concatenate([t.reshape(-1).astype(F32) for t in parts])
    n = flat.shape[0]
    pad = (-n) % 1024
    return jnp.pad(flat, (0, pad)).reshape(-1, 128)


def _unpack(buf, shapes):
    flat = buf.reshape(buf.shape[:-2] + (-1,))
    out, o = [], 0
    for s in shapes:
        n = 1
        for d in s:
            n *= d
        out.append(flat[..., o:o + n].reshape(buf.shape[:-2] + tuple(s)))
        o += n
    return out


def _rope_tables(seq, ctx_len):
    t = jnp.arange(seq)
    inv = 1.0 / (ROPE_THETA ** (jnp.arange(0, DH // 4, dtype=F32) / (DH // 4)))
    ar = (t // GRID_W).astype(F32)[:, None] * inv[None, :]
    ac = (t % GRID_W).astype(F32)[:, None] * inv[None, :]
    cos = jnp.concatenate([jnp.cos(ar), jnp.cos(ar), jnp.cos(ac), jnp.cos(ac)], axis=-1)
    sin = jnp.concatenate([-jnp.sin(ar), jnp.sin(ar), -jnp.sin(ac), jnp.sin(ac)], axis=-1)
    return (jnp.concatenate([jnp.ones((ctx_len, DH), F32), cos], axis=0),
            jnp.concatenate([jnp.zeros((ctx_len, DH), F32), sin], axis=0))


def kernel(x, c, ctx, c_ctx, w_mod, b_mod, norm1_g, norm2_g, w_in, ret_decay_f, ret_decay_b, conv_dw_w, conv_dw_b, conv_ln_g, conv_ln_b, w_out, ffn_w_up, ffn_dw_w, ffn_dw_b, ffn_w_down, final_norm_g, loss_target, m_c_ctx, m_w_mod, m_b_mod, m_norm1_g, m_norm2_g, m_w_in, m_ret_decay_f, m_ret_decay_b, m_conv_dw_w, m_conv_dw_b, m_conv_ln_g, m_conv_ln_b, m_w_out, m_ffn_w_up, m_ffn_dw_w, m_ffn_dw_b, m_ffn_w_down, m_final_norm_g, v_c_ctx, v_w_mod, v_b_mod, v_norm1_g, v_norm2_g, v_w_in, v_ret_decay_f, v_ret_decay_b, v_conv_dw_w, v_conv_dw_b, v_conv_ln_g, v_conv_ln_b, v_w_out, v_ffn_w_up, v_ffn_dw_w, v_ffn_dw_b, v_ffn_w_down, v_final_norm_g):
    _, SEQ, D = x.shape
    CTX = ctx.shape[1]
    L = w_in.shape[0]
    CWs = conv_dw_w.shape[2]
    CW = 4 * CWs
    RW = 4 * w_out.shape[1] - CW
    H = RW // DH
    CFs = ffn_dw_w.shape[-1]
    CF = 4 * CFs
    NMs = w_mod.shape[2]
    T = CTX + SEQ
    ncr, ncc = CTX // R, CTX // RC
    assert CTX == R and RW == CW and RW % DH == 0 and SEQ % R == 0 and R % GRID_W == 0
    assert w_in.shape[2] * 4 == 4 * RW + 2 * CW and NMs * 4 == N_MOD * D

    mx, my, mc = _place()
    me = 4 * mx + 2 * my + mc
    jm = 2 * mx + my
    c_idx = jnp.reshape(mc, (1,)).astype(jnp.int32)

    shapes0 = [(D,), (L, CONV_K, CWs), (L, 9, CFs)]
    g0, _ = gather_sum(_pack([c[0], conv_dw_w, ffn_dw_w.reshape(L, 9, CFs)]), "gather_cond")
    c_all, cw_all, fw_all = _unpack(g0, shapes0)
    conv_w = jnp.concatenate([cw_all[2 * j] for j in range(4)], axis=-1)
    ffn_w = jnp.concatenate([fw_all[2 * j] for j in range(4)], axis=-1)
    conv_w32 = jnp.pad(conv_w, ((0, 0), (0, 32 - CONV_K), (0, 0)))
    ffn_w16 = jnp.pad(ffn_w, ((0, 0), (0, 7), (0, 0)))
    cs = jnp.concatenate([c_all, c_ctx[None, :], jnp.zeros((7, D), F32)], axis=0)
    mod_shard = mod_fwd(cs, w_mod, "mod_fwd")
    g1, _ = gather_sum(mod_shard.reshape(-1, 128), "gather_mod")
    mod_all = g1.reshape(8, L, 16, NMs)
    mod_full = jnp.concatenate([mod_all[2 * j] for j in range(4)], axis=-1) + b_mod[:, None, :]
    mod_mine = lax.dynamic_index_in_dim(mod_full, me, axis=1, keepdims=False)
    modv = jnp.stack([mod_full[:, 8], mod_mine], axis=1).reshape(L, 2, N_MOD, D)

    big = {"w_in": (w_in, 'n'), "w_out": (w_out, 'k'), "w_up": (ffn_w_up, 'n'), "w_down": (ffn_w_down, 'k')}
    wg = {k: gather_weights(cast_bf16(w, f"cast_{k}"), f"gather_{k}") for k, (w, _) in big.items()}

    cosT, sinT = _rope_tables(SEQ, CTX)
    lgt = [jnp.broadcast_to(jnp.concatenate([jax.nn.log_sigmoid(ret_decay_f[l]), jax.nn.log_sigmoid(ret_decay_b[l])])[:, None],
                            (2 * H, DH)) for l in range(L)]
    row = lambda t: t.reshape(1, -1)

    xs = jnp.concatenate([ctx[0], x[0]], axis=0)
    saved = []
    for l in range(L):
        h1 = norm_mod(xs, row(norm1_g[l]), modv[l], 0, 1, ncr, "norm_mod")
        p = mm_nn(h1, wg["w_in"][l], 'n', BF16, "mm_in")
        uc = conv31_fwd(p, conv_w32[l], row(conv_dw_b[l]), RW, CW, ncr, "conv31_fwd")
        o_f, o_b, stf, stb = ret_fwd(p, cosT, sinT, lgt[l], H, ncc, "ret_fwd")
        mix = mix_fwd(o_f, o_b, p, uc, row(conv_ln_g[l]), row(conv_ln_b[l]), H, "mix_fwd")
        y1 = mm_nn(mix, wg["w_out"][l], 'k', F32, "mm_out")
        x2 = gate_res(xs, y1, modv[l], 2, ncr, "gate_res")
        h2 = norm_mod(x2, row(norm2_g[l]), modv[l], 3, 4, ncr, "norm_mod")
        up = mm_nn(h2, wg["w_up"][l], 'n', BF16, "mm_up")
        act = ffn_act(up, ffn_w16[l], row(ffn_dw_b[l]), CF, ncr, "ffn_act")
        y2 = mm_nn(act, wg["w_down"][l], 'k', F32, "mm_down")
        x3 = gate_res(x2, y2, modv[l], 5, ncr, "gate_res")
        saved.append((xs, h1, p, uc, o_f, o_b, stf, stb, mix, y1, x2, h2, up, act, y2))
        xs = x3

    dx, acc_loss = loss_head(xs, loss_target[0], row(final_norm_g), ncr, "loss_head")
    loss = lax.psum(0.5 / D * jnp.sum(acc_loss[1]), ("x", "y", "c"))

    small, gbig = [None] * L, {k: [None] * L for k in big}
    for l in reversed(range(L)):
        x1, h1, p, uc, o_f, o_b, stf, stb, mix, y1, x2, h2, up, act, y2 = saved[l]
        dy2, ag2 = gate_res_bwd(dx, y2, modv[l], 5, ncr, "gate_res_bwd")
        gbig["w_down"][l] = reduce_weight_grad(mm_tn(act, dy2, 4, 'k', "mm_down_dw"), c_idx, "w_down")
        dact = mm_nt(dy2, wg["w_down"][l], 'k', BF16, "mm_down_dx")
        dgc, dup, accb = ffn_act_bwd1(up, dact, ffn_w16[l], row(ffn_dw_b[l]), CF, ncr, "ffn_act_bwd1")
        dup, accfw = ffn_act_bwd2(dgc, up, dup, ffn_w16[l], CF, ncr, "ffn_act_bwd2")
        gbig["w_up"][l] = reduce_weight_grad(mm_tn(h2, dup, 4, 'n', "mm_up_dw"), c_idx, "w_up")
        dh2 = mm_nt(dup, wg["w_up"][l], 'n', F32, "mm_up_dx")
        dx2, an2 = norm_mod_bwd(x2, row(norm2_g[l]), modv[l], dh2, dx, 4, ncr, "norm_mod_bwd")
        dy1, ag1 = gate_res_bwd(dx2, y1, modv[l], 2, ncr, "gate_res_bwd")
        gbig["w_out"][l] = reduce_weight_grad(mm_tn(mix, dy1, 4, 'k', "mm_out_dw"), c_idx, "w_out")
        dmix = mm_nt(dy1, wg["w_out"][l], 'k', BF16, "mm_out_dx")
        do, dg, duc, accln = mix_bwd(dmix, o_f, o_b, p, uc, row(conv_ln_g[l]), row(conv_ln_b[l]), H, "mix_bwd")
        da, dbg, acccw = conv31_bwd(duc, p, conv_w32[l], RW, CW, ncr, "conv31_bwd")
        dqf, dkf, dvf, dqb, dkb, dvb, glg = ret_bwd(p, do, cosT, sinT, lgt[l], stf, stb, H, ncc, "ret_bwd")
        dp = assemble_dp(dqf, dqb, dkf, dkb, dvf, dvb, dg, da, dbg, "assemble_dp")
        gbig["w_in"][l] = reduce_weight_grad(mm_tn(h1, dp, 4, 'n', "mm_in_dw"), c_idx, "w_in")
        dh1 = mm_nt(dp, wg["w_in"][l], 'n', F32, "mm_in_dx")
        dx, an1 = norm_mod_bwd(x1, row(norm1_g[l]), modv[l], dh1, dx2, 1, ncr, "norm_mod_bwd")
        dmod = jnp.stack([jnp.stack([an1[0], an1[1], ag1[0], an2[0], an2[1], ag2[0]]),
                          jnp.stack([an1[2], an1[3], ag1[1], an2[2], an2[3], ag2[1]])])
        dlg = jnp.sum(glg, axis=(1, 2))
        dth = dlg * jnp.concatenate([jax.nn.sigmoid(-ret_decay_f[l]), jax.nn.sigmoid(-ret_decay_b[l])])
        small[l] = [dmod, an1[4], an2[4], acccw[31], accln[0], accln[1], acccw[:CONV_K], accfw[:9], accb[0], dth]
    grad_x = dx[CTX:][None]

    shapes1 = [(2, N_MOD, D), (D,), (D,), (CW,), (CW,), (CW,), (CONV_K, CW), (9, CF), (CF,), (2 * H,)]
    flat_parts = [t for l in range(L) for t in small[l]] + [acc_loss[0]]
    g2, s2 = gather_sum(_pack(flat_parts), "gather_small_grads")
    sums = _unpack(s2, shapes1 * L + [(D,)])
    per_dev = _unpack(g2, shapes1 * L + [(D,)])
    nS = len(shapes1)
    col = lambda i: jnp.stack([sums[l * nS + i] for l in range(L)])
    dmod_sum = col(0)
    dmod_dev = jnp.stack([per_dev[l * nS] for l in range(L)], axis=0)
    g_b_mod = (dmod_sum[:, 0] + dmod_sum[:, 1]).reshape(L, N_MOD * D)
    g_norm1, g_norm2 = col(1), col(2)
    g_conv_b, g_ln_g, g_ln_b = col(3), col(4), col(5)
    g_conv_w = lax.dynamic_slice_in_dim(col(6), jm * CWs, CWs, axis=2)
    g_ffn_w = lax.dynamic_slice_in_dim(col(7), jm * CFs, CFs, axis=2).reshape(L, 3, 3, CFs)
    g_ffn_b = col(8)
    g_ret = col(9)
    g_final = sums[-1]

    dmod_rows = jnp.concatenate([dmod_dev[:, :, 1].reshape(L, 8, N_MOD * D), dmod_sum[:, 0].reshape(L, 1, N_MOD * D),
                                 jnp.zeros((L, 7, N_MOD * D), F32)], axis=1)
    dmod_shard = lax.dynamic_slice_in_dim(dmod_rows, jm * NMs, NMs, axis=2)
    g_w_mod, ds_part = mod_bwd(cs, w_mod, dmod_shard, "mod_bwd")
    _, ds_sum = gather_sum(ds_part.reshape(-1, 128), "gather_dsilu")
    ds_ctx = 0.5 * ds_sum.reshape(16, D)[8]
    sg = jax.nn.sigmoid(c_ctx)
    g_c_ctx = ds_ctx * (sg * (1.0 + c_ctx * (1.0 - sg)))

    grads = {
        "c_ctx": g_c_ctx, "w_mod": g_w_mod, "b_mod": g_b_mod, "norm1_g": g_norm1, "norm2_g": g_norm2,
        "w_in": jnp.stack(gbig["w_in"]), "ret_decay_f": g_ret[:, :H], "ret_decay_b": g_ret[:, H:],
        "conv_dw_w": g_conv_w, "conv_dw_b": g_conv_b, "conv_ln_g": g_ln_g, "conv_ln_b": g_ln_b,
        "w_out": jnp.stack(gbig["w_out"]), "ffn_w_up": jnp.stack(gbig["w_up"]), "ffn_dw_w": g_ffn_w,
        "ffn_dw_b": g_ffn_b, "ffn_w_down": jnp.stack(gbig["w_down"]), "final_norm_g": g_final,
    }
    params = {
        "c_ctx": (c_ctx, m_c_ctx, v_c_ctx), "w_mod": (w_mod, m_w_mod, v_w_mod), "b_mod": (b_mod, m_b_mod, v_b_mod),
        "norm1_g": (norm1_g, m_norm1_g, v_norm1_g), "norm2_g": (norm2_g, m_norm2_g, v_norm2_g),
        "w_in": (w_in, m_w_in, v_w_in), "ret_decay_f": (ret_decay_f, m_ret_decay_f, v_ret_decay_f),
        "ret_decay_b": (ret_decay_b, m_ret_decay_b, v_ret_decay_b),
        "conv_dw_w": (conv_dw_w, m_conv_dw_w, v_conv_dw_w), "conv_dw_b": (conv_dw_b, m_conv_dw_b, v_conv_dw_b),
        "conv_ln_g": (conv_ln_g, m_conv_ln_g, v_conv_ln_g), "conv_ln_b": (conv_ln_b, m_conv_ln_b, v_conv_ln_b),
        "w_out": (w_out, m_w_out, v_w_out), "ffn_w_up": (ffn_w_up, m_ffn_w_up, v_ffn_w_up),
        "ffn_dw_w": (ffn_dw_w, m_ffn_dw_w, v_ffn_dw_w), "ffn_dw_b": (ffn_dw_b, m_ffn_dw_b, v_ffn_dw_b),
        "ffn_w_down": (ffn_w_down, m_ffn_w_down, v_ffn_w_down),
        "final_norm_g": (final_norm_g, m_final_norm_g, v_final_norm_g),
    }
    names = list(params)
    upd = {n: adamw(params[n][0], grads[n], params[n][1], params[n][2], f"adamw_{n}") for n in names}
    return (loss, grad_x, *[grads[n] for n in names], *[upd[n][0] for n in names],
            *[upd[n][1] for n in names], *[upd[n][2] for n in names])
```

```python
import jax
import jax.numpy as jnp
from jax import lax
from jax.experimental import pallas as pl
from jax.experimental.pallas import tpu as pltpu

F32 = jnp.float32
BF16 = jnp.bfloat16
EPS = 1e-6
DH = 128
RC = 128
GRID_W = 64
ROPE_THETA = 10000.0
N_MOD = 6
R = 256
HALO = 16
CONV_K = 31
SUB = 64
LANES = 256
VMEM_LIMIT = 48 * 1024 * 1024
MESH = pl.DeviceIdType.MESH
ADAM_LR, ADAM_B1, ADAM_B2, ADAM_EPS, ADAM_WD, ADAM_STEP = 0.001, 0.9, 0.999, 1e-08, 0.01, 10


def _cp(n):
    return pltpu.CompilerParams(dimension_semantics=("arbitrary",) * n, vmem_limit_bytes=VMEM_LIMIT)


def _cp0():
    return pltpu.CompilerParams(vmem_limit_bytes=VMEM_LIMIT)


def _sig(v):
    return 1.0 / (1.0 + jnp.exp(-v))


def _dot(a, b):
    return jnp.dot(a, b, preferred_element_type=F32)


def _dot_nt(a, b):
    return lax.dot_general(a, b, (((1,), (1,)), ((), ())), preferred_element_type=F32)


def _dot_tn(a, b):
    return lax.dot_general(a, b, (((0,), (0,)), ((), ())), preferred_element_type=F32)


def _lane_tile(n, cap):
    t = (min(n, cap) // 128) * 128
    while t >= 128:
        if n % t == 0:
            return t
        t -= 128
    raise ValueError(f"no lane tile for {n}")


def _row_tile(rows, cols, max_elems):
    if rows * cols <= max_elems:
        return rows
    t = (min(rows, max(8, max_elems // cols)) // 8) * 8
    while t >= 8:
        if rows % t == 0:
            return t
        t -= 8
    raise ValueError(f"no row tile for {rows}x{cols}")


def _place():
    return lax.axis_index("x"), lax.axis_index("y"), lax.axis_index("c")


def _other_chips(x, y):
    return [(1 - x, y), (x, 1 - y), (1 - x, 1 - y)]


def _first_last(grid):
    ids = [pl.program_id(k) for k in range(len(grid))]
    first, last = ids[0] == 0, ids[0] == grid[0] - 1
    for k in range(1, len(grid)):
        first = jnp.logical_and(first, ids[k] == 0)
        last = jnp.logical_and(last, ids[k] == grid[k] - 1)
    return first, last


def _side_copies(src_slab, dst_ref, send_sems, recv_sems, local_sem):
    x, y, c = _place()
    jm = 2 * x + y
    chips = _other_chips(x, y)
    mine = pltpu.make_async_copy(src_slab(jm), dst_ref.at[jm], local_sem)
    sends = [pltpu.make_async_remote_copy(src_ref=src_slab(2 * px + py), dst_ref=dst_ref.at[jm], send_sem=send_sems.at[t],
                                          recv_sem=recv_sems.at[t], device_id=(px, py, c), device_id_type=MESH)
             for t, (px, py) in enumerate(chips)]
    recvs = [pltpu.make_async_remote_copy(src_ref=dst_ref.at[2 * px + py], dst_ref=dst_ref.at[2 * px + py],
                                          send_sem=send_sems.at[t], recv_sem=recv_sems.at[t], device_id=(px, py, c),
                                          device_id_type=MESH) for t, (px, py) in enumerate(chips)]
    return mine, sends, recvs


def _side_start(mine, sends, recvs):
    mine.start()
    for s in sends:
        s.start()


def _side_finish(mine, sends, recvs):
    for r in recvs:
        r.wait_recv()
    for s in sends:
        s.wait_send()
    mine.wait()


_SIDE_SEMS = [pltpu.SemaphoreType.DMA((3,)), pltpu.SemaphoreType.DMA((3,)), pltpu.SemaphoreType.DMA]


def mm_nn(a, b, shard, out_dtype, name, bcast=None):
    M, K = a.shape
    S, Kb, Nb = b.shape
    N = S * Nb if shard == 'n' else Nb
    assert K == (Kb if shard == 'n' else S * Kb)
    tm, tn, tk = _lane_tile(M, 768), _lane_tile(Nb, 1536), _lane_tile(Kb, 2048)
    nk = K // tk
    grid = (M // tm, N // tn, nk)
    if shard == 'n':
        tps = Nb // tn
        b_map = lambda i, j, k: (j // tps, k, j % tps)
    else:
        kps = Kb // tk
        b_map = lambda i, j, k: (k // kps, k % kps, j)

    def body(a_ref, b_ref, *rest):
        if bcast is None:
            o_ref, acc_ref = rest
        else:
            w_ref, o_ref, g_ref, acc_ref, send_sems, recv_sems, local_sem = rest
            first, last = _first_last(grid)
            side = _side_copies(lambda j: w_ref, g_ref, send_sems, recv_sems, local_sem)
            pl.when(first)(lambda: _side_start(*side))
        k = pl.program_id(2)

        @pl.when(k == 0)
        def _():
            acc_ref[...] = jnp.zeros_like(acc_ref)

        acc_ref[...] += _dot(a_ref[...], b_ref[...])

        @pl.when(k == nk - 1)
        def _():
            o_ref[...] = acc_ref[...].astype(o_ref.dtype)

        if bcast is not None:
            pl.when(last)(lambda: _side_finish(*side))

    hbm = pl.BlockSpec(memory_space=pl.ANY)
    in_specs = [pl.BlockSpec((tm, tk), lambda i, j, k: (i, k)), pl.BlockSpec((None, tk, tn), b_map)]
    out_specs = [pl.BlockSpec((tm, tn), lambda i, j, k: (i, j))]
    out_shape = [jax.ShapeDtypeStruct((M, N), out_dtype)]
    scratch = [pltpu.VMEM((tm, tn), F32)]
    args = [a, b]
    if bcast is not None:
        in_specs.append(hbm)
        out_specs.append(hbm)
        out_shape.append(jax.ShapeDtypeStruct((4,) + bcast.shape, bcast.dtype))
        scratch += _SIDE_SEMS
        args.append(bcast)
    res = pl.pallas_call(body, grid=grid, in_specs=in_specs, out_specs=out_specs, out_shape=out_shape,
                         scratch_shapes=scratch, compiler_params=_cp(3), name=name)(*args)
    return res[0] if bcast is None else res


def mm_nt(a, b, shard, out_dtype, name, exch=None):
    M, N = a.shape
    S, Kb, Nb = b.shape
    K = Kb if shard == 'n' else S * Kb
    assert N == (S * Nb if shard == 'n' else Nb)
    tm, tko, tnr = _lane_tile(M, 768), _lane_tile(Kb, 1536), _lane_tile(Nb, 1536)
    nr = N // tnr
    grid = (M // tm, K // tko, nr)
    if shard == 'n':
        tps = Nb // tnr
        b_map = lambda i, j, r: (r // tps, j, r % tps)
    else:
        kps = Kb // tko
        b_map = lambda i, j, r: (j // kps, j % kps, r)

    def body(a_ref, b_ref, *rest):
        if exch is None:
            o_ref, acc_ref = rest
        else:
            e_ref, o_ref, p_ref, acc_ref, send_sems, recv_sems, local_sem = rest
            first, last = _first_last(grid)
            side = _side_copies(lambda j: e_ref.at[j], p_ref, send_sems, recv_sems, local_sem)
            pl.when(first)(lambda: _side_start(*side))
        r = pl.program_id(2)

        @pl.when(r == 0)
        def _():
            acc_ref[...] = jnp.zeros_like(acc_ref)

        acc_ref[...] += _dot_nt(a_ref[...], b_ref[...])

        @pl.when(r == nr - 1)
        def _():
            o_ref[...] = acc_ref[...].astype(o_ref.dtype)

        if exch is not None:
            pl.when(last)(lambda: _side_finish(*side))

    hbm = pl.BlockSpec(memory_space=pl.ANY)
    in_specs = [pl.BlockSpec((tm, tnr), lambda i, j, r: (i, r)), pl.BlockSpec((None, tko, tnr), b_map)]
    out_specs = [pl.BlockSpec((tm, tko), lambda i, j, r: (i, j))]
    out_shape = [jax.ShapeDtypeStruct((M, K), out_dtype)]
    scratch = [pltpu.VMEM((tm, tko), F32)]
    args = [a, b]
    if exch is not None:
        in_specs.append(hbm)
        out_specs.append(hbm)
        out_shape.append(jax.ShapeDtypeStruct(exch.shape, exch.dtype))
        scratch += _SIDE_SEMS
        args.append(exch)
    res = pl.pallas_call(body, grid=grid, in_specs=in_specs, out_specs=out_specs, out_shape=out_shape,
                         scratch_shapes=scratch, compiler_params=_cp(3), name=name)(*args)
    return res[0] if exch is None else res


def mm_tn(a, c, S, shard, name):
    M, K = a.shape
    N = c.shape[1]
    Kb, Nb = (K, N // S) if shard == 'n' else (K // S, N)
    tm, tk, tn = _lane_tile(M, 768), _lane_tile(Kb, 1536), _lane_tile(Nb, 1536)
    nm = M // tm
    if shard == 'n':
        tps = Nb // tn
        o_map = lambda i, j, m: (j // tps, i, j % tps)
    else:
        kps = Kb // tk
        o_map = lambda i, j, m: (i // kps, i % kps, j)

    def body(a_ref, c_ref, o_ref, acc_ref):
        m = pl.program_id(2)

        @pl.when(m == 0)
        def _():
            acc_ref[...] = jnp.zeros_like(acc_ref)

        acc_ref[...] += _dot_tn(a_ref[...], c_ref[...])

        @pl.when(m == nm - 1)
        def _():
            o_ref[...] = acc_ref[...].astype(BF16)

    return pl.pallas_call(
        body, grid=(K // tk, N // tn, nm),
        in_specs=[pl.BlockSpec((tm, tk), lambda i, j, m: (m, i)), pl.BlockSpec((tm, tn), lambda i, j, m: (m, j))],
        out_specs=pl.BlockSpec((None, tk, tn), o_map),
        out_shape=jax.ShapeDtypeStruct((S, Kb, Nb), BF16),
        scratch_shapes=[pltpu.VMEM((tk, tn), F32)], compiler_params=_cp(3), name=name)(a, c)


def _mod_spec(D, ncr):
    return pl.BlockSpec((None, N_MOD, D), lambda i: (jnp.where(i < ncr, 0, 1), 0, 0))


def norm_mod(x, g, modv, i_sh, i_sc, ncr, name):
    T, D = x.shape

    def body(x_ref, g_ref, m_ref, h_ref):
        xv = x_ref[...]
        r = lax.rsqrt(jnp.mean(xv * xv, axis=-1, keepdims=True) + EPS)
        n = xv * r * g_ref[...]
        h_ref[...] = (n * (1.0 + m_ref[i_sc:i_sc + 1, :]) + m_ref[i_sh:i_sh + 1, :]).astype(BF16)

    return pl.pallas_call(
        body, grid=(T // R,),
        in_specs=[pl.BlockSpec((R, D), lambda i: (i, 0)), pl.BlockSpec((1, D), lambda i: (0, 0)), _mod_spec(D, ncr)],
        out_specs=pl.BlockSpec((R, D), lambda i: (i, 0)),
        out_shape=jax.ShapeDtypeStruct((T, D), BF16), compiler_params=_cp(1), name=name)(x, g, modv)


def norm_mod_bwd(x, g, modv, dh, dres, i_sc, ncr, name):
    T, D = x.shape

    def body(x_ref, g_ref, m_ref, dh_ref, dr_ref, dx_ref, acc_ref):
        i = pl.program_id(0)

        @pl.when(i == 0)
        def _():
            acc_ref[...] = jnp.zeros_like(acc_ref)

        xv = x_ref[...]
        r = lax.rsqrt(jnp.mean(xv * xv, axis=-1, keepdims=True) + EPS)
        xh = xv * r
        gv = g_ref[...]
        dhv = dh_ref[...]
        dn = dhv * (1.0 + m_ref[i_sc:i_sc + 1, :])
        s_sh = jnp.sum(dhv, axis=0, keepdims=True)
        s_sc = jnp.sum(dhv * (xh * gv), axis=0, keepdims=True)
        acc_ref[4:5, :] += jnp.sum(dn * xh, axis=0, keepdims=True)
        dxh = dn * gv
        dx_ref[...] = dr_ref[...] + r * (dxh - xh * jnp.mean(dxh * xh, axis=-1, keepdims=True))

        @pl.when(i < ncr)
        def _():
            acc_ref[0:1, :] += s_sh
            acc_ref[1:2, :] += s_sc

        @pl.when(i >= ncr)
        def _():
            acc_ref[2:3, :] += s_sh
            acc_ref[3:4, :] += s_sc

    row = pl.BlockSpec((R, D), lambda i: (i, 0))
    return pl.pallas_call(
        body, grid=(T // R,),
        in_specs=[row, pl.BlockSpec((1, D), lambda i: (0, 0)), _mod_spec(D, ncr), row, row],
        out_specs=[row, pl.BlockSpec((8, D), lambda i: (0, 0))],
        out_shape=[jax.ShapeDtypeStruct((T, D), F32), jax.ShapeDtypeStruct((8, D), F32)],
        compiler_params=_cp(1), name=name)(x, g, modv, dh, dres)


def gate_res(x, y, modv, i_g, ncr, name):
    T, D = x.shape

    def body(x_ref, y_ref, m_ref, o_ref):
        o_ref[...] = x_ref[...] + m_ref[i_g:i_g + 1, :] * y_ref[...]

    row = pl.BlockSpec((R, D), lambda i: (i, 0))
    return pl.pallas_call(
        body, grid=(T // R,), in_specs=[row, row, _mod_spec(D, ncr)], out_specs=row,
        out_shape=jax.ShapeDtypeStruct((T, D), F32), compiler_params=_cp(1), name=name)(x, y, modv)


def gate_res_bwd(dx, y, modv, i_g, ncr, name):
    T, D = dx.shape

    def body(dx_ref, y_ref, m_ref, dy_ref, acc_ref):
        i = pl.program_id(0)

        @pl.when(i == 0)
        def _():
            acc_ref[...] = jnp.zeros_like(acc_ref)

        dxv = dx_ref[...]
        dy_ref[...] = (m_ref[i_g:i_g + 1, :] * dxv).astype(BF16)
        s = jnp.sum(dxv * y_ref[...], axis=0, keepdims=True)

        @pl.when(i < ncr)
        def _():
            acc_ref[0:1, :] += s

        @pl.when(i >= ncr)
        def _():
            acc_ref[1:2, :] += s

    row = pl.BlockSpec((R, D), lambda i: (i, 0))
    return pl.pallas_call(
        body, grid=(T // R,), in_specs=[row, row, _mod_spec(D, ncr)],
        out_specs=[row, pl.BlockSpec((8, D), lambda i: (0, 0))],
        out_shape=[jax.ShapeDtypeStruct((T, D), BF16), jax.ShapeDtypeStruct((8, D), F32)],
        compiler_params=_cp(1), name=name)(dx, y, modv)


def loss_head(x, target, g, ncr, name):
    T, D = x.shape

    def body(x_ref, t_ref, g_ref, dx_ref, acc_ref):
        i = pl.program_id(0)

        @pl.when(i == 0)
        def _():
            acc_ref[...] = jnp.zeros_like(acc_ref)

        @pl.when(i < ncr)
        def _():
            dx_ref[...] = jnp.zeros_like(dx_ref)

        @pl.when(i >= ncr)
        def _():
            xv = x_ref[...]
            r = lax.rsqrt(jnp.mean(xv * xv, axis=-1, keepdims=True) + EPS)
            xh = xv * r
            gv = g_ref[...]
            e = xh * gv - t_ref[...]
            acc_ref[1:2, :] += jnp.sum(e * e, axis=0, keepdims=True)
            dy = e * (1.0 / D)
            acc_ref[0:1, :] += jnp.sum(dy * xh, axis=0, keepdims=True)
            dxh = dy * gv
            dx_ref[...] = r * (dxh - xh * jnp.mean(dxh * xh, axis=-1, keepdims=True))

    row = pl.BlockSpec((R, D), lambda i: (i, 0))
    return pl.pallas_call(
        body, grid=(T // R,),
        in_specs=[row, pl.BlockSpec((R, D), lambda i: (jnp.maximum(i - ncr, 0), 0)), pl.BlockSpec((1, D), lambda i: (0, 0))],
        out_specs=[row, pl.BlockSpec((8, D), lambda i: (0, 0))],
        out_shape=[jax.ShapeDtypeStruct((T, D), F32), jax.ShapeDtypeStruct((8, D), F32)],
        compiler_params=_cp(1), name=name)(x, target, g)


def _conv31_specs(T):
    nh = R // HALO
    pv = lambda i: jnp.maximum(i * nh - 1, 0)
    nx = lambda i: jnp.minimum((i + 1) * nh, T // HALO - 1)
    return pv, nx


def _row_shift_matrices():
    n = R + 2 * HALO
    t = lax.broadcasted_iota(jnp.int32, (8, n, n), 1)
    u = lax.broadcasted_iota(jnp.int32, (8, n, n), 2)
    s = lax.broadcasted_iota(jnp.int32, (8, n, n), 0)
    return jnp.where(u == t + s, 1.0, 0.0).astype(BF16)


def _shifted_copies(E, sm):
    v = E[0]
    hi = v.astype(BF16)
    lo = (v - hi.astype(F32)).astype(BF16)
    for s in range(1, 8):
        E[s] = _dot(sm[s], hi) + _dot(sm[s], lo)


def _tap31(E, r0, o):
    return E[o % 8, pl.ds(r0 + 8 * (o // 8), SUB), :]


def conv31_fwd(p, w32, b, sm, RW, CW, ncr, name):
    T = p.shape[0]
    nT, cbk = T // R, LANES
    n = R + 2 * HALO
    a0, g0 = 4 * RW // cbk, (4 * RW + CW) // cbk
    pv, nx = _conv31_specs(T)

    def body(a, g, ap, gp, an, gn, w, bb, sm_ref, uc, E):
        i = pl.program_id(0)
        has_prev = jnp.logical_and(i != 0, i != ncr)
        has_next = jnp.logical_and(i != ncr - 1, i != nT - 1)
        glu = lambda u, v: u.astype(F32) * _sig(v.astype(F32))
        E[0, 0:HALO, :] = jnp.where(has_prev, glu(ap[...], gp[...]), 0.0)
        E[0, HALO:HALO + R, :] = glu(a[...], g[...])
        E[0, HALO + R:, :] = jnp.where(has_next, glu(an[...], gn[...]), 0.0)
        _shifted_copies(E, sm_ref)
        for r0 in range(0, R, SUB):
            acc = jnp.broadcast_to(bb[...], (SUB, cbk))
            for k in range(CONV_K):
                acc = acc + w[k:k + 1, :] * _tap31(E, r0, 1 + k)
            uc[r0:r0 + SUB, :] = acc

    cur = lambda c0: pl.BlockSpec((R, cbk), lambda i, j: (i, c0 + j))
    hp = lambda c0: pl.BlockSpec((HALO, cbk), lambda i, j: (pv(i), c0 + j))
    hn = lambda c0: pl.BlockSpec((HALO, cbk), lambda i, j: (nx(i), c0 + j))
    return pl.pallas_call(
        body, grid=(nT, CW // cbk),
        in_specs=[cur(a0), cur(g0), hp(a0), hp(g0), hn(a0), hn(g0),
                  pl.BlockSpec((32, cbk), lambda i, j: (0, j)), pl.BlockSpec((1, cbk), lambda i, j: (0, j)),
                  pl.BlockSpec((8, n, n), lambda i, j: (0, 0, 0))],
        out_specs=pl.BlockSpec((R, cbk), lambda i, j: (i, j)),
        out_shape=jax.ShapeDtypeStruct((T, CW), F32),
        scratch_shapes=[pltpu.VMEM((8, n, cbk), F32)],
        compiler_params=_cp(2), name=name)(p, p, p, p, p, p, w32, b, sm)


def conv31_bwd(duc, p, w32, sm, RW, CW, ncr, name):
    T = p.shape[0]
    nT, cbk = T // R, LANES
    n = R + 2 * HALO
    a0, g0 = 4 * RW // cbk, (4 * RW + CW) // cbk
    pv, nx = _conv31_specs(T)

    def body(d, dp_, dn_, a, g, ap, gp, an, gn, w, sm_ref, da, dg, accw, U, Dd):
        i = pl.program_id(1)

        @pl.when(i == 0)
        def _():
            accw[...] = jnp.zeros_like(accw)

        has_prev = jnp.logical_and(i != 0, i != ncr)
        has_next = jnp.logical_and(i != ncr - 1, i != nT - 1)
        glu = lambda u, v: u.astype(F32) * _sig(v.astype(F32))
        U[0, 0:HALO, :] = jnp.where(has_prev, glu(ap[...], gp[...]), 0.0)
        U[0, HALO:HALO + R, :] = glu(a[...], g[...])
        U[0, HALO + R:, :] = jnp.where(has_next, glu(an[...], gn[...]), 0.0)
        Dd[0, 0:HALO, :] = jnp.where(has_prev, dp_[...], 0.0)
        Dd[0, HALO:HALO + R, :] = d[...]
        Dd[0, HALO + R:, :] = jnp.where(has_next, dn_[...], 0.0)
        _shifted_copies(U, sm_ref)
        _shifted_copies(Dd, sm_ref)
        for r0 in range(0, R, SUB):
            du = jnp.zeros((SUB, cbk), F32)
            for k in range(CONV_K):
                du = du + w[k:k + 1, :] * _tap31(Dd, r0, HALO + 15 - k)
            av = a[r0:r0 + SUB, :].astype(F32)
            sg = _sig(g[r0:r0 + SUB, :].astype(F32))
            da[r0:r0 + SUB, :] = (du * sg).astype(BF16)
            dg[r0:r0 + SUB, :] = (du * av * sg * (1.0 - sg)).astype(BF16)
            dcur = d[r0:r0 + SUB, :]
            for k in range(CONV_K):
                accw[k:k + 1, :] += jnp.sum(dcur * _tap31(U, r0, 1 + k), axis=0, keepdims=True)
            accw[31:32, :] += jnp.sum(dcur, axis=0, keepdims=True)

    cur = lambda c0: pl.BlockSpec((R, cbk), lambda j, i: (i, c0 + j))
    hp = lambda c0: pl.BlockSpec((HALO, cbk), lambda j, i: (pv(i), c0 + j))
    hn = lambda c0: pl.BlockSpec((HALO, cbk), lambda j, i: (nx(i), c0 + j))
    out = pl.BlockSpec((R, cbk), lambda j, i: (i, j))
    ext = pltpu.VMEM((8, n, cbk), F32)
    return pl.pallas_call(
        body, grid=(CW // cbk, nT),
        in_specs=[cur(0), hp(0), hn(0), cur(a0), cur(g0), hp(a0), hp(g0), hn(a0), hn(g0),
                  pl.BlockSpec((32, cbk), lambda j, i: (0, j)), pl.BlockSpec((8, n, n), lambda j, i: (0, 0, 0))],
        out_specs=[out, out, pl.BlockSpec((32, cbk), lambda j, i: (0, j))],
        out_shape=[jax.ShapeDtypeStruct((T, CW), BF16), jax.ShapeDtypeStruct((T, CW), BF16),
                   jax.ShapeDtypeStruct((32, CW), F32)],
        scratch_shapes=[ext, ext],
        compiler_params=_cp(2), name=name)(duc, duc, duc, p, p, p, p, p, p, w32, sm)


def _rope(v, cosv, sinv, first):
    swapped = jnp.where(first, pltpu.roll(v, 96, 1), pltpu.roll(v, 32, 1))
    return v * cosv + swapped * sinv


def _unrope(v, cosv, sinv, first):
    z = v * sinv
    return v * cosv + jnp.where(first, pltpu.roll(z, 96, 1), pltpu.roll(z, 32, 1))


def _decay_tables(lg_ref, H, DM, QD, KD, CD):
    n = lax.broadcasted_iota(jnp.int32, (RC, DH), 0).astype(F32)
    m = lax.broadcasted_iota(jnp.int32, (RC, DH), 1).astype(F32)
    for d in range(2):
        for h in range(H):
            i = d * H + h
            lg = lg_ref[i:i + 1, :]
            diff = (n - m) if d == 0 else (m - n)
            DM[i] = jnp.where(diff >= 0, jnp.exp(lg * jnp.maximum(diff, 0.0)), 0.0)
            QD[i] = jnp.exp(lg * ((n + 1.0) if d == 0 else (RC - n)))
            KD[i] = jnp.exp(lg * ((RC - 1.0 - n) if d == 0 else n))
            CD[i] = jnp.exp(lg * float(RC)) + jnp.zeros((RC, DH), F32)


def _chunk_orders(NC, ncc):
    cf = lambda s: s
    cb = lambda s: jnp.where(s < ncc, ncc - 1 - s, NC - 1 - (s - ncc))
    return cf, cb


def ret_fwd(p, cosT, sinT, lgt, H, ncc, name):
    T = p.shape[0]
    RW, NC = H * DH, T // RC
    cf, cb = _chunk_orders(NC, ncc)
    scale = DH ** -0.5

    def body(qf, kf, vf, qb, kb, vb, cosf, sinf, cosb, sinb, lg_ref, of_ref, ob_ref, sf_ref, sb_ref, S, DM, QD, KD, CD):
        s = pl.program_id(0)

        @pl.when(s == 0)
        def _():
            S[...] = jnp.zeros_like(S)
            _decay_tables(lg_ref, H, DM, QD, KD, CD)

        first = (lax.broadcasted_iota(jnp.int32, (RC, DH), 1) % 64) < 32
        for d, (q_ref, k_ref, v_ref, c_ref, s_ref, o_ref, st_ref) in enumerate(
                ((qf, kf, vf, cosf, sinf, of_ref, sf_ref), (qb, kb, vb, cosb, sinb, ob_ref, sb_ref))):
            cosv, sinv = c_ref[...], s_ref[...]
            for h in range(H):
                hs, i = slice(h * DH, (h + 1) * DH), d * H + h
                q16 = _rope(q_ref[:, hs].astype(F32), cosv, sinv, first).astype(BF16)
                k = _rope(k_ref[:, hs].astype(F32), cosv, sinv, first) * scale
                k16 = k.astype(BF16)
                v = v_ref[:, hs]
                s_in = S[i]
                s16 = s_in.astype(BF16)
                st_ref[h] = s16
                sc = _dot_nt(q16, k16) * DM[i]
                o_ref[:, hs] = _dot(sc.astype(BF16), v) + _dot(q16, s16) * QD[i]
                S[i] = s_in * CD[i] + _dot_tn((k * KD[i]).astype(BF16), v)

    pspec = lambda col, cm: pl.BlockSpec((RC, RW), lambda s: (cm(s), col))
    tspec = lambda cm: pl.BlockSpec((RC, DH), lambda s: (cm(s), 0))
    ospec = lambda cm: pl.BlockSpec((RC, RW), lambda s: (cm(s), 0))
    stspec = pl.BlockSpec((None, H, DH, DH), lambda s: (s, 0, 0, 0))
    tab = pltpu.VMEM((2 * H, RC, DH), F32)
    return pl.pallas_call(
        body, grid=(NC,),
        in_specs=[pspec(0, cf), pspec(1, cf), pspec(2, cf), pspec(0, cb), pspec(1, cb), pspec(2, cb),
                  tspec(cf), tspec(cf), tspec(cb), tspec(cb), pl.BlockSpec((2 * H, DH), lambda s: (0, 0))],
        out_specs=[ospec(cf), ospec(cb), stspec, stspec],
        out_shape=[jax.ShapeDtypeStruct((T, RW), F32), jax.ShapeDtypeStruct((T, RW), F32),
                   jax.ShapeDtypeStruct((NC, H, DH, DH), BF16), jax.ShapeDtypeStruct((NC, H, DH, DH), BF16)],
        scratch_shapes=[tab, tab, tab, tab, tab], compiler_params=_cp(1), name=name)(
            p, p, p, p, p, p, cosT, sinT, cosT, sinT, lgt)


def ret_bwd(p, do, cosT, sinT, lgt, stf, stb, H, ncc, name):
    T = p.shape[0]
    RW, NC = H * DH, T // RC
    cf0, cb0 = _chunk_orders(NC, ncc)
    cf = lambda sp: cf0(NC - 1 - sp)
    cb = lambda sp: cb0(NC - 1 - sp)
    scale = DH ** -0.5

    def body(qf, kf, vf, qb, kb, vb, dof, dob, cosf, sinf, cosb, sinb, lg_ref, stf_ref, stb_ref,
             dqf, dkf, dvf, dqb, dkb, dvb, glg, dS, DM, QD, KD, CD):
        sp = pl.program_id(0)

        @pl.when(sp == 0)
        def _():
            dS[...] = jnp.zeros_like(dS)
            glg[...] = jnp.zeros_like(glg)
            _decay_tables(lg_ref, H, DM, QD, KD, CD)

        first = (lax.broadcasted_iota(jnp.int32, (RC, DH), 1) % 64) < 32
        n = lax.broadcasted_iota(jnp.int32, (RC, DH), 0).astype(F32)
        m = lax.broadcasted_iota(jnp.int32, (RC, DH), 1).astype(F32)
        for d, (q_ref, k_ref, v_ref, do_ref, c_ref, s_ref, st_ref, dq_ref, dk_ref, dv_ref) in enumerate(
                ((qf, kf, vf, dof, cosf, sinf, stf_ref, dqf, dkf, dvf),
                 (qb, kb, vb, dob, cosb, sinb, stb_ref, dqb, dkb, dvb))):
            cosv, sinv = c_ref[...], s_ref[...]
            diff = (n - m) if d == 0 else (m - n)
            posq = (n + 1.0) if d == 0 else (RC - n)
            posk = (RC - 1.0 - n) if d == 0 else n
            for h in range(H):
                hs, i = slice(h * DH, (h + 1) * DH), d * H + h
                q = _rope(q_ref[:, hs].astype(F32), cosv, sinv, first)
                k = _rope(k_ref[:, hs].astype(F32), cosv, sinv, first) * scale
                q16, k16 = q.astype(BF16), k.astype(BF16)
                v = v_ref[:, hs]
                s_in = st_ref[h]
                ds_out = dS[i]
                ds16 = ds_out.astype(BF16)
                do16 = do_ref[:, hs]
                doq = (do16.astype(F32) * QD[i]).astype(BF16)
                a = _dot_nt(q16, k16) * DM[i]
                da_raw = _dot_nt(do16, v)
                da16 = (da_raw * DM[i]).astype(BF16)
                dq_state = _dot_nt(doq, s_in)
                dk_state = _dot_nt(v, ds16) * KD[i]
                dqr = _dot(da16, k16) + dq_state
                dkr = _dot_tn(da16, q16) + dk_state
                dv_ref[:, hs] = _dot_tn(a.astype(BF16), do16) + _dot((k * KD[i]).astype(BF16), ds16)
                dS[i] = ds_out * CD[i] + _dot_tn(q16, doq)
                glg[i] += (da_raw * a * diff + posq * q * dq_state + posk * k * dk_state
                           + float(RC) * CD[i] * ds_out * s_in.astype(F32))
                dq_ref[:, hs] = _unrope(dqr, cosv, sinv, first)
                dk_ref[:, hs] = _unrope(dkr, cosv, sinv, first) * scale

    pspec = lambda col, cm: pl.BlockSpec((RC, RW), lambda s: (cm(s), col))
    tspec = lambda cm: pl.BlockSpec((RC, DH), lambda s: (cm(s), 0))
    ospec = lambda cm: pl.BlockSpec((RC, RW), lambda s: (cm(s), 0))
    stspec = pl.BlockSpec((None, H, DH, DH), lambda s: (NC - 1 - s, 0, 0, 0))
    tab = pltpu.VMEM((2 * H, RC, DH), F32)
    big = jax.ShapeDtypeStruct((T, RW), F32)
    return pl.pallas_call(
        body, grid=(NC,),
        in_specs=[pspec(0, cf), pspec(1, cf), pspec(2, cf), pspec(0, cb), pspec(1, cb), pspec(2, cb),
                  ospec(cf), ospec(cb), tspec(cf), tspec(cf), tspec(cb), tspec(cb),
                  pl.BlockSpec((2 * H, DH), lambda s: (0, 0)), stspec, stspec],
        out_specs=[ospec(cf), ospec(cf), ospec(cf), ospec(cb), ospec(cb), ospec(cb),
                   pl.BlockSpec((2 * H, RC, DH), lambda s: (0, 0, 0))],
        out_shape=[big, big, big, big, big, big, jax.ShapeDtypeStruct((2 * H, RC, DH), F32)],
        scratch_shapes=[tab, tab, tab, tab, tab], compiler_params=_cp(1), name=name)(
            p, p, p, p, p, p, do, do, cosT, sinT, cosT, sinT, lgt, stf, stb)


def mix_fwd(o_f, o_b, p, uc, lng, lnb, H, name):
    T, RW = o_f.shape
    CW = uc.shape[1]

    def body(of_ref, ob_ref, g_ref, uc_ref, lg_ref, lb_ref, out_ref):
        for h in range(H):
            hs = slice(h * DH, (h + 1) * DH)
            o = of_ref[:, hs] + ob_ref[:, hs]
            on = o * lax.rsqrt(jnp.mean(o * o, axis=-1, keepdims=True) + EPS)
            gv = g_ref[:, hs].astype(F32)
            out_ref[:, hs] = (gv * _sig(gv) * on).astype(BF16)
        u = uc_ref[...]
        mu = jnp.mean(u, axis=-1, keepdims=True)
        var = jnp.mean(jnp.square(u - mu), axis=-1, keepdims=True)
        z = (u - mu) * lax.rsqrt(var + EPS) * lg_ref[...] + lb_ref[...]
        out_ref[:, RW:] = (z * _sig(z)).astype(BF16)

    rw = pl.BlockSpec((R, RW), lambda i: (i, 0))
    vec = pl.BlockSpec((1, CW), lambda i: (0, 0))
    return pl.pallas_call(
        body, grid=(T // R,),
        in_specs=[rw, rw, pl.BlockSpec((R, RW), lambda i: (i, 3)), pl.BlockSpec((R, CW), lambda i: (i, 0)), vec, vec],
        out_specs=pl.BlockSpec((R, RW + CW), lambda i: (i, 0)),
        out_shape=jax.ShapeDtypeStruct((T, RW + CW), BF16), compiler_params=_cp(1), name=name)(o_f, o_b, p, uc, lng, lnb)


def mix_bwd(dmix, o_f, o_b, p, uc, lng, lnb, H, name):
    T, RW = o_f.shape
    CW = uc.shape[1]

    def body(dm_ref, of_ref, ob_ref, g_ref, uc_ref, lg_ref, lb_ref, do_ref, dg_ref, duc_ref, acc_ref):
        i = pl.program_id(0)

        @pl.when(i == 0)
        def _():
            acc_ref[...] = jnp.zeros_like(acc_ref)

        for h in range(H):
            hs = slice(h * DH, (h + 1) * DH)
            o = of_ref[:, hs] + ob_ref[:, hs]
            r = lax.rsqrt(jnp.mean(o * o, axis=-1, keepdims=True) + EPS)
            on = o * r
            gv = g_ref[:, hs].astype(F32)
            sg = _sig(gv)
            dmr = dm_ref[:, hs].astype(F32)
            dg_ref[:, hs] = (dmr * on * (sg * (1.0 + gv * (1.0 - sg)))).astype(BF16)
            don = dmr * (gv * sg)
            do_ref[:, hs] = (r * (don - on * jnp.mean(don * on, axis=-1, keepdims=True))).astype(BF16)
        u = uc_ref[...]
        mu = jnp.mean(u, axis=-1, keepdims=True)
        rs = lax.rsqrt(jnp.mean(jnp.square(u - mu), axis=-1, keepdims=True) + EPS)
        zh = (u - mu) * rs
        lg = lg_ref[...]
        z = zh * lg + lb_ref[...]
        sz = _sig(z)
        dz = dm_ref[:, RW:].astype(F32) * (sz * (1.0 + z * (1.0 - sz)))
        acc_ref[0:1, :] += jnp.sum(dz * zh, axis=0, keepdims=True)
        acc_ref[1:2, :] += jnp.sum(dz, axis=0, keepdims=True)
        dzh = dz * lg
        duc_ref[...] = rs * (dzh - jnp.mean(dzh, axis=-1, keepdims=True)
                             - zh * jnp.mean(dzh * zh, axis=-1, keepdims=True))

    rw = pl.BlockSpec((R, RW), lambda i: (i, 0))
    cw = pl.BlockSpec((R, CW), lambda i: (i, 0))
    vec = pl.BlockSpec((1, CW), lambda i: (0, 0))
    return pl.pallas_call(
        body, grid=(T // R,),
        in_specs=[pl.BlockSpec((R, RW + CW), lambda i: (i, 0)), rw, rw, pl.BlockSpec((R, RW), lambda i: (i, 3)), cw, vec, vec],
        out_specs=[rw, rw, cw, pl.BlockSpec((8, CW), lambda i: (0, 0))],
        out_shape=[jax.ShapeDtypeStruct((T, RW), BF16), jax.ShapeDtypeStruct((T, RW), BF16),
                   jax.ShapeDtypeStruct((T, CW), F32), jax.ShapeDtypeStruct((8, CW), F32)],
        compiler_params=_cp(1), name=name)(dmix, o_f, o_b, p, uc, lng, lnb)


def assemble_dp(dqf, dqb, dkf, dkb, dvf, dvb, dg, da, dbg, name):
    T, RW = dqf.shape
    CW = da.shape[1]

    def body(qf, qb, kf, kb, vf, vb, g, a, b, out):
        out[:, 0:RW] = (qf[...] + qb[...]).astype(BF16)
        out[:, RW:2 * RW] = (kf[...] + kb[...]).astype(BF16)
        out[:, 2 * RW:3 * RW] = (vf[...] + vb[...]).astype(BF16)
        out[:, 3 * RW:4 * RW] = g[...]
        out[:, 4 * RW:4 * RW + CW] = a[...]
        out[:, 4 * RW + CW:] = b[...]

    rw = pl.BlockSpec((R, RW), lambda i: (i, 0))
    cw = pl.BlockSpec((R, CW), lambda i: (i, 0))
    W = 4 * RW + 2 * CW
    return pl.pallas_call(
        body, grid=(T // R,), in_specs=[rw] * 7 + [cw, cw], out_specs=pl.BlockSpec((R, W), lambda i: (i, 0)),
        out_shape=jax.ShapeDtypeStruct((T, W), BF16), compiler_params=_cp(1), name=name)(
            dqf, dqb, dkf, dkb, dvf, dvb, dg, da, dbg)


def _col_shift_matrices():
    n = R + 2 * GRID_W
    t = lax.broadcasted_iota(jnp.int32, (n, n), 0)
    u = lax.broadcasted_iota(jnp.int32, (n, n), 1)
    one = lambda cond: jnp.where(cond, 1.0, 0.0).astype(BF16)
    left, right = u == t - 1, u == t + 1
    return jnp.stack([one(jnp.logical_and(left, t % GRID_W != 0)), one(jnp.logical_and(right, t % GRID_W != GRID_W - 1)),
                      one(left), one(right)])


def _fill_ext(bufs, eb, sm, cur, prv, nxt, has_prev, has_next, is_ctx):
    pv, nv = prv[...], nxt[...]
    eb[0:GRID_W, :] = jnp.where(has_prev, pv, jnp.zeros_like(pv))
    eb[GRID_W:GRID_W + R, :] = cur[...]
    eb[GRID_W + R:, :] = jnp.where(has_next, nv, jnp.zeros_like(nv))
    v = eb[...]
    bufs[1][...] = v.astype(F32)
    bufs[0][...] = _dot(sm[2 if is_ctx else 0], v)
    bufs[2][...] = _dot(sm[3 if is_ctx else 1], v)


def _taps(is_ctx):
    return [(dr, dc) for dr in ((0,) if is_ctx else (-1, 0, 1)) for dc in (-1, 0, 1)]


def _tap_src(bufs, r0, c0, dr, dc):
    return bufs[dc + 1][pl.ds(GRID_W + r0 + GRID_W * dr, SUB), pl.ds(c0, LANES)]


def _conv9(bufs, w, r0, c0, is_ctx, flip):
    acc = jnp.zeros((SUB, LANES), F32)
    for dr, dc in _taps(is_ctx):
        widx = (dr + 1) * 3 + dc + 1
        src = _tap_src(bufs, r0, c0, -dr, -dc) if flip else _tap_src(bufs, r0, c0, dr, dc)
        acc = acc + w[widx:widx + 1, pl.ds(c0, LANES)] * src
    return acc


def _ffn_specs(T, cb, order):
    nq = R // GRID_W
    n = R + 2 * GRID_W
    pv = lambda i: jnp.maximum(i * nq - 1, 0)
    nx = lambda i: jnp.minimum((i + 1) * nq, T // GRID_W - 1)
    if order == 'ij':
        mk = lambda blk, rf, c0: pl.BlockSpec(blk, lambda i, j: (rf(i), c0 + j))
        smat = pl.BlockSpec((4, n, n), lambda i, j: (0, 0, 0))
    else:
        mk = lambda blk, rf, c0: pl.BlockSpec(blk, lambda j, i: (rf(i), c0 + j))
        smat = pl.BlockSpec((4, n, n), lambda j, i: (0, 0, 0))
    cur = lambda c0: mk((R, cb), lambda i: i, c0)
    hp = lambda c0: mk((GRID_W, cb), pv, c0)
    hn = lambda c0: mk((GRID_W, cb), nx, c0)
    vec = lambda rows: mk((rows, cb), lambda i: 0, 0)
    return cur, hp, hn, vec, smat


def _ffn_flags(i, ncr, nT):
    return i > ncr, jnp.logical_and(i >= ncr, i != nT - 1)


def _ffn_scratch(cb, sets):
    n = R + 2 * GRID_W
    return [pltpu.VMEM((n, cb), F32)] * (3 * sets) + [pltpu.VMEM((n, cb), BF16)]


def ffn_act(up, w16, b, sm, CF, ncr, name):
    T = up.shape[0]
    nT, cb = T // R, 2 * LANES
    ncb = CF // cb
    cur, hp, hn, vec, smat = _ffn_specs(T, cb, 'ij')

    def body(g, v, gp, gn, w, bb, sm_ref, out, e0, e1, e2, eb):
        i = pl.program_id(0)
        has_prev, has_next = _ffn_flags(i, ncr, nT)
        bufs = (e0, e1, e2)

        def run(is_ctx):
            _fill_ext(bufs, eb, sm_ref, g, gp, gn, has_prev, has_next, is_ctx)
            for r0 in range(0, R, SUB):
                for c0 in range(0, cb, LANES):
                    gc = _conv9(bufs, w, r0, c0, is_ctx, False) + bb[:, pl.ds(c0, LANES)]
                    val = v[r0:r0 + SUB, c0:c0 + LANES].astype(F32)
                    out[r0:r0 + SUB, c0:c0 + LANES] = (gc * _sig(gc) * val).astype(BF16)

        pl.when(i < ncr)(lambda: run(True))
        pl.when(i >= ncr)(lambda: run(False))

    return pl.pallas_call(
        body, grid=(nT, ncb), in_specs=[cur(0), cur(ncb), hp(0), hn(0), vec(16), vec(1), smat], out_specs=cur(0),
        out_shape=jax.ShapeDtypeStruct((T, CF), BF16),
        scratch_shapes=_ffn_scratch(cb, 1), compiler_params=_cp(2), name=name)(up, up, up, up, w16, b, sm)


def ffn_act_bwd1(up, dact, w16, b, sm, CF, ncr, name):
    T = up.shape[0]
    nT, cb = T // R, 2 * LANES
    ncb = CF // cb
    cur, hp, hn, vec, smat = _ffn_specs(T, cb, 'ji')

    def body(g, v, gp, gn, da, w, bb, sm_ref, dgc, dup, accb, e0, e1, e2, eb):
        i = pl.program_id(1)

        @pl.when(i == 0)
        def _():
            accb[...] = jnp.zeros_like(accb)

        has_prev, has_next = _ffn_flags(i, ncr, nT)
        bufs = (e0, e1, e2)

        def run(is_ctx):
            _fill_ext(bufs, eb, sm_ref, g, gp, gn, has_prev, has_next, is_ctx)
            for r0 in range(0, R, SUB):
                for c0 in range(0, cb, LANES):
                    gc = _conv9(bufs, w, r0, c0, is_ctx, False) + bb[:, pl.ds(c0, LANES)]
                    sg = _sig(gc)
                    val = v[r0:r0 + SUB, c0:c0 + LANES].astype(F32)
                    dav = da[r0:r0 + SUB, c0:c0 + LANES].astype(F32)
                    dup[r0:r0 + SUB, c0:c0 + LANES] = (dav * gc * sg).astype(BF16)
                    d = dav * val * (sg * (1.0 + gc * (1.0 - sg)))
                    dgc[r0:r0 + SUB, c0:c0 + LANES] = d.astype(BF16)
                    accb[0:1, pl.ds(c0, LANES)] += jnp.sum(d, axis=0, keepdims=True)

        pl.when(i < ncr)(lambda: run(True))
        pl.when(i >= ncr)(lambda: run(False))

    return pl.pallas_call(
        body, grid=(ncb, nT), in_specs=[cur(0), cur(ncb), hp(0), hn(0), cur(0), vec(16), vec(1), smat],
        out_specs=[cur(0), cur(ncb), vec(8)],
        out_shape=[jax.ShapeDtypeStruct((T, CF), BF16), jax.ShapeDtypeStruct((T, 2 * CF), BF16),
                   jax.ShapeDtypeStruct((8, CF), F32)],
        scratch_shapes=_ffn_scratch(cb, 1), compiler_params=_cp(2), name=name)(up, up, up, up, dact, w16, b, sm)


def ffn_act_bwd2(dgc, up, dup, w16, sm, CF, ncr, name):
    T = up.shape[0]
    nT, cb = T // R, 2 * LANES
    ncb = CF // cb
    cur, hp, hn, vec, smat = _ffn_specs(T, cb, 'ji')

    def body(d, dp_, dn_, g, gp, gn, w, sm_ref, dup_in, dgate, accw, d0, d1, d2, g0, g1, g2, eb):
        i = pl.program_id(1)

        @pl.when(i == 0)
        def _():
            accw[...] = jnp.zeros_like(accw)

        has_prev, has_next = _ffn_flags(i, ncr, nT)
        dbufs, gbufs = (d0, d1, d2), (g0, g1, g2)

        def run(is_ctx):
            _fill_ext(dbufs, eb, sm_ref, d, dp_, dn_, has_prev, has_next, is_ctx)
            _fill_ext(gbufs, eb, sm_ref, g, gp, gn, has_prev, has_next, is_ctx)
            for r0 in range(0, R, SUB):
                for c0 in range(0, cb, LANES):
                    dgate[r0:r0 + SUB, c0:c0 + LANES] = _conv9(dbufs, w, r0, c0, is_ctx, True).astype(BF16)
                    dcur = _tap_src(dbufs, r0, c0, 0, 0)
                    for dr, dc in _taps(is_ctx):
                        widx = (dr + 1) * 3 + dc + 1
                        accw[widx:widx + 1, pl.ds(c0, LANES)] += jnp.sum(dcur * _tap_src(gbufs, r0, c0, dr, dc),
                                                                         axis=0, keepdims=True)

        pl.when(i < ncr)(lambda: run(True))
        pl.when(i >= ncr)(lambda: run(False))

    return pl.pallas_call(
        body, grid=(ncb, nT),
        in_specs=[cur(0), hp(0), hn(0), cur(0), hp(0), hn(0), vec(16), smat, pl.BlockSpec(memory_space=pl.ANY)],
        out_specs=[cur(0), vec(16)],
        out_shape=[jax.ShapeDtypeStruct((T, 2 * CF), BF16), jax.ShapeDtypeStruct((16, CF), F32)],
        input_output_aliases={8: 0}, scratch_shapes=_ffn_scratch(cb, 2), compiler_params=_cp(2), name=name)(
            dgc, dgc, dgc, up, up, up, w16, sm, dup)


def mod_fwd(cs, w_mod, name):
    L, D, Ns = w_mod.shape
    tn = _lane_tile(Ns, 768)

    def body(c_ref, w_ref, o_ref):
        cv = c_ref[...]
        o_ref[...] = _dot((cv * _sig(cv)).astype(BF16), w_ref[...].astype(BF16))

    return pl.pallas_call(
        body, grid=(L, Ns // tn),
        in_specs=[pl.BlockSpec((16, D), lambda l, j: (0, 0)), pl.BlockSpec((None, D, tn), lambda l, j: (l, 0, j))],
        out_specs=pl.BlockSpec((None, 16, tn), lambda l, j: (l, 0, j)),
        out_shape=jax.ShapeDtypeStruct((L, 16, Ns), F32), compiler_params=_cp(2), name=name)(cs, w_mod)


def mod_bwd(cs, w_mod, dmod, name):
    L, D, Ns = w_mod.shape
    tn = _lane_tile(Ns, 768)

    def body(c_ref, w_ref, dm_ref, gw_ref, ds_ref):
        @pl.when(jnp.logical_and(pl.program_id(0) == 0, pl.program_id(1) == 0))
        def _():
            ds_ref[...] = jnp.zeros_like(ds_ref)

        cv = c_ref[...]
        dm = dm_ref[...].astype(BF16)
        gw_ref[...] = _dot_tn((cv * _sig(cv)).astype(BF16), dm)
        ds_ref[...] += _dot_nt(dm, w_ref[...].astype(BF16))

    return pl.pallas_call(
        body, grid=(L, Ns // tn),
        in_specs=[pl.BlockSpec((16, D), lambda l, j: (0, 0)), pl.BlockSpec((None, D, tn), lambda l, j: (l, 0, j)),
                  pl.BlockSpec((None, 16, tn), lambda l, j: (l, 0, j))],
        out_specs=[pl.BlockSpec((None, D, tn), lambda l, j: (l, 0, j)), pl.BlockSpec((16, D), lambda l, j: (0, 0))],
        out_shape=[jax.ShapeDtypeStruct((L, D, Ns), F32), jax.ShapeDtypeStruct((16, D), F32)],
        compiler_params=_cp(2), name=name)(cs, w_mod, dmod)


def cast_bf16(w, name):
    L, Kb, Nb = w.shape
    w2 = w.reshape(L * Kb, Nb)
    tr = _row_tile(L * Kb, Nb, 1 << 19)

    def body(w_ref, o_ref):
        o_ref[...] = w_ref[...].astype(BF16)

    spec = pl.BlockSpec((tr, Nb), lambda i: (i, 0))
    out = pl.pallas_call(body, grid=(L * Kb // tr,), in_specs=[spec], out_specs=spec,
                         out_shape=jax.ShapeDtypeStruct((L * Kb, Nb), BF16), compiler_params=_cp(1), name=name)(w2)
    return out.reshape(L, Kb, Nb)


def add_half(dw, recv, c_idx, name):
    S, Kb, Nb = dw.shape
    Kh = Kb // 2
    tr = _row_tile(Kh, Nb, 1 << 19)
    nb = Kh // tr

    def body(c_ref, a_ref, b_ref, o_ref):
        o_ref[...] = (a_ref[...].astype(F32) + b_ref[...].astype(F32)).astype(BF16)

    return pl.pallas_call(
        body,
        grid_spec=pltpu.PrefetchScalarGridSpec(
            num_scalar_prefetch=1, grid=(S, nb),
            in_specs=[pl.BlockSpec((None, tr, Nb), lambda s, i, c: (s, c[0] * nb + i, 0)),
                      pl.BlockSpec((None, tr, Nb), lambda s, i, c: (s, i, 0))],
            out_specs=pl.BlockSpec((None, tr, Nb), lambda s, i, c: (s, i, 0))),
        out_shape=jax.ShapeDtypeStruct((S, Kh, Nb), BF16), compiler_params=_cp(2), name=name)(c_idx, dw, recv)


def add_shards(parts, name):
    S, Kh, Nb = parts.shape
    tr = _row_tile(Kh, Nb, 1 << 18)

    def body(p_ref, o_ref):
        acc = p_ref[0].astype(F32)
        for s in range(1, S):
            acc = acc + p_ref[s].astype(F32)
        o_ref[...] = acc

    return pl.pallas_call(
        body, grid=(Kh // tr,), in_specs=[pl.BlockSpec((S, tr, Nb), lambda i: (0, i, 0))],
        out_specs=pl.BlockSpec((tr, Nb), lambda i: (i, 0)),
        out_shape=jax.ShapeDtypeStruct((Kh, Nb), F32), compiler_params=_cp(1), name=name)(parts)


def adamw(w, g, m, v, name):
    shape = w.shape
    cols = shape[-1]
    rows = w.size // cols
    w2, g2, m2, v2 = (t.reshape(rows, cols) for t in (w, g, m, v))
    tr = _row_tile(rows, cols, 3 << 17)

    def body(w_ref, g_ref, m_ref, v_ref, d_ref, nm_ref, nv_ref):
        gv = g_ref[...]
        nm = ADAM_B1 * m_ref[...] + (1.0 - ADAM_B1) * gv
        nv = ADAM_B2 * v_ref[...] + (1.0 - ADAM_B2) * jnp.square(gv)
        m_hat = nm / (1.0 - ADAM_B1 ** ADAM_STEP)
        v_hat = nv / (1.0 - ADAM_B2 ** ADAM_STEP)
        d_ref[...] = -ADAM_LR * (m_hat / (jnp.sqrt(v_hat) + ADAM_EPS) + ADAM_WD * w_ref[...])
        nm_ref[...] = nm
        nv_ref[...] = nv

    spec = pl.BlockSpec((tr, cols), lambda i: (i, 0))
    sds = jax.ShapeDtypeStruct((rows, cols), F32)
    d, nm, nv = pl.pallas_call(body, grid=(rows // tr,), in_specs=[spec] * 4, out_specs=[spec] * 3,
                               out_shape=[sds, sds, sds], compiler_params=_cp(1), name=name)(w2, g2, m2, v2)
    return d.reshape(shape), nm.reshape(shape), nv.reshape(shape)


def gather_sum(v, name):
    r, cols = v.shape

    def body(v_ref, g_ref, s_ref, send_sems, recv_sems, local_sem):
        x, y, c = _place()
        me = 4 * x + 2 * y + c
        mine = pltpu.make_async_copy(v_ref, g_ref.at[me], local_sem)
        mine.start()
        sends, peers = [], []
        for k in range(1, 8):
            px = 1 - x if k & 4 else x
            py = 1 - y if k & 2 else y
            pc = 1 - c if k & 1 else c
            cp = pltpu.make_async_remote_copy(src_ref=v_ref, dst_ref=g_ref.at[me], send_sem=send_sems.at[k - 1],
                                              recv_sem=recv_sems.at[k - 1], device_id=(px, py, pc), device_id_type=MESH)
            cp.start()
            sends.append(cp)
            peers.append((px, py, pc))
        for k, (px, py, pc) in enumerate(peers):
            pltpu.make_async_remote_copy(src_ref=v_ref, dst_ref=g_ref.at[4 * px + 2 * py + pc], send_sem=send_sems.at[k],
                                         recv_sem=recv_sems.at[k], device_id=(px, py, pc), device_id_type=MESH).wait_recv()
        for cp in sends:
            cp.wait_send()
        mine.wait()
        acc = g_ref[0]
        for d in range(1, 8):
            acc = acc + g_ref[d]
        s_ref[...] = acc

    vm = pl.BlockSpec(memory_space=pltpu.VMEM)
    return pl.pallas_call(
        body, in_specs=[vm], out_specs=[vm, vm],
        out_shape=[jax.ShapeDtypeStruct((8, r, cols), F32), jax.ShapeDtypeStruct((r, cols), F32)],
        scratch_shapes=[pltpu.SemaphoreType.DMA((7,)), pltpu.SemaphoreType.DMA((7,)), pltpu.SemaphoreType.DMA],
        compiler_params=_cp0(), name=name)(v)


def gather_weights(wb, name):
    L, Kb, Nb = wb.shape
    Kh = Kb // 2

    def body(w_ref, *rest):
        outs, (send_sems, recv_sems, local_sems) = rest[:L], rest[L:]
        x, y, c = _place()
        jm = 2 * x + y
        sib = (x, y, 1 - c)
        chips = _other_chips(x, y)
        mine_rows, sib_rows = pl.ds(c * Kh, Kh), pl.ds((1 - c) * Kh, Kh)

        def cp(l, t, src, dst, to):
            return pltpu.make_async_remote_copy(src_ref=src, dst_ref=dst, send_sem=send_sems.at[7 * l + t],
                                                recv_sem=recv_sems.at[7 * l + t], device_id=to, device_id_type=MESH)

        started, local = [], []
        for l in range(L):
            src = w_ref.at[l, mine_rows, :]
            dst = outs[l].at[jm, mine_rows, :]
            lc = pltpu.make_async_copy(src, dst, local_sems.at[l])
            lc.start()
            local.append(lc)
            for t, (px, py) in enumerate(chips):
                started.append(cp(l, t, src, dst, (px, py, c)))
            started.append(cp(l, 3, src, dst, sib))
            for s in started[-4:]:
                s.start()
        for l in range(L):
            for t, (px, py) in enumerate(chips):
                blk = outs[l].at[2 * px + py, mine_rows, :]
                cp(l, t, blk, blk, (px, py, c)).wait_recv()
                fwd = cp(l, 4 + t, blk, blk, sib)
                fwd.start()
                started.append(fwd)
        for l in range(L):
            blk = outs[l].at[jm, sib_rows, :]
            cp(l, 3, blk, blk, sib).wait_recv()
            for t, (px, py) in enumerate(chips):
                blk = outs[l].at[2 * px + py, sib_rows, :]
                cp(l, 4 + t, blk, blk, sib).wait_recv()
        for s in started:
            s.wait_send()
        for lc in local:
            lc.wait()

    hbm = pl.BlockSpec(memory_space=pl.ANY)
    return pl.pallas_call(
        body, in_specs=[hbm], out_specs=[hbm] * L,
        out_shape=[jax.ShapeDtypeStruct((4, Kb, Nb), BF16)] * L,
        scratch_shapes=[pltpu.SemaphoreType.DMA((7 * L,)), pltpu.SemaphoreType.DMA((7 * L,)), pltpu.SemaphoreType.DMA((L,))],
        compiler_params=_cp0(), name=name)(wb)


def swap_halves(dw, name):
    S, Kb, Nb = dw.shape
    Kh = Kb // 2

    def body(d_ref, o_ref, send_sems, recv_sems):
        x, y, c = _place()
        sib = (x, y, 1 - c)
        cps = [pltpu.make_async_remote_copy(src_ref=d_ref.at[s, pl.ds((1 - c) * Kh, Kh), :], dst_ref=o_ref.at[s],
                                            send_sem=send_sems.at[s], recv_sem=recv_sems.at[s], device_id=sib,
                                            device_id_type=MESH) for s in range(S)]
        for cpy in cps:
            cpy.start()
        for cpy in cps:
            cpy.wait_recv()
        for cpy in cps:
            cpy.wait_send()

    hbm = pl.BlockSpec(memory_space=pl.ANY)
    return pl.pallas_call(
        body, in_specs=[hbm], out_specs=hbm, out_shape=jax.ShapeDtypeStruct((S, Kh, Nb), dw.dtype),
        scratch_shapes=[pltpu.SemaphoreType.DMA((S,)), pltpu.SemaphoreType.DMA((S,))],
        compiler_params=_cp0(), name=name)(dw)


def join_halves(gh, name):
    Kh, Nb = gh.shape

    def body(g_ref, o_ref, send_sem, recv_sem, local_sem):
        x, y, c = _place()
        sib = (x, y, 1 - c)
        rows = o_ref.at[pl.ds(c * Kh, Kh), :]
        mine = pltpu.make_async_copy(g_ref, rows, local_sem)
        mine.start()
        cpy = pltpu.make_async_remote_copy(src_ref=g_ref, dst_ref=rows, send_sem=send_sem, recv_sem=recv_sem,
                                           device_id=sib, device_id_type=MESH)
        cpy.start()
        other = o_ref.at[pl.ds((1 - c) * Kh, Kh), :]
        pltpu.make_async_remote_copy(src_ref=g_ref, dst_ref=other, send_sem=send_sem, recv_sem=recv_sem,
                                     device_id=sib, device_id_type=MESH).wait_recv()
        cpy.wait_send()
        mine.wait()

    hbm = pl.BlockSpec(memory_space=pl.ANY)
    return pl.pallas_call(
        body, in_specs=[hbm], out_specs=hbm, out_shape=jax.ShapeDtypeStruct((2 * Kh, Nb), F32),
        scratch_shapes=[pltpu.SemaphoreType.DMA, pltpu.SemaphoreType.DMA, pltpu.SemaphoreType.DMA],
        compiler_params=_cp0(), name=name)(gh)


def chip_sums(dw, c_idx, tag):
    return add_half(dw, swap_halves(dw, f"rs_swap_{tag}"), c_idx, f"rs_add_half_{tag}")


def finish_grad(parts, tag):
    return join_halves(add_shards(parts, f"rs_add_shards_{tag}"), f"rs_join_{tag}")


def _pack(parts):
    flat = jnp.concatenate([t.reshape(-1).astype(F32) for t in parts])
    n = flat.shape[0]
    pad = (-n) % 1024
    return jnp.pad(flat, (0, pad)).reshape(-1, 128)


def _unpack(buf, shapes):
    flat = buf.reshape(buf.shape[:-2] + (-1,))
    out, o = [], 0
    for s in shapes:
        n = 1
        for d in s:
            n *= d
        out.append(flat[..., o:o + n].reshape(buf.shape[:-2] + tuple(s)))
        o += n
    return out


def _rope_tables(seq, ctx_len):
    t = jnp.arange(seq)
    inv = 1.0 / (ROPE_THETA ** (jnp.arange(0, DH // 4, dtype=F32) / (DH // 4)))
    ar = (t // GRID_W).astype(F32)[:, None] * inv[None, :]
    ac = (t % GRID_W).astype(F32)[:, None] * inv[None, :]
    cos = jnp.concatenate([jnp.cos(ar), jnp.cos(ar), jnp.cos(ac), jnp.cos(ac)], axis=-1)
    sin = jnp.concatenate([-jnp.sin(ar), jnp.sin(ar), -jnp.sin(ac), jnp.sin(ac)], axis=-1)
    return (jnp.concatenate([jnp.ones((ctx_len, DH), F32), cos], axis=0),
            jnp.concatenate([jnp.zeros((ctx_len, DH), F32), sin], axis=0))


def kernel(x, c, ctx, c_ctx, w_mod, b_mod, norm1_g, norm2_g, w_in, ret_decay_f, ret_decay_b, conv_dw_w, conv_dw_b, conv_ln_g, conv_ln_b, w_out, ffn_w_up, ffn_dw_w, ffn_dw_b, ffn_w_down, final_norm_g, loss_target, m_c_ctx, m_w_mod, m_b_mod, m_norm1_g, m_norm2_g, m_w_in, m_ret_decay_f, m_ret_decay_b, m_conv_dw_w, m_conv_dw_b, m_conv_ln_g, m_conv_ln_b, m_w_out, m_ffn_w_up, m_ffn_dw_w, m_ffn_dw_b, m_ffn_w_down, m_final_norm_g, v_c_ctx, v_w_mod, v_b_mod, v_norm1_g, v_norm2_g, v_w_in, v_ret_decay_f, v_ret_decay_b, v_conv_dw_w, v_conv_dw_b, v_conv_ln_g, v_conv_ln_b, v_w_out, v_ffn_w_up, v_ffn_dw_w, v_ffn_dw_b, v_ffn_w_down, v_final_norm_g):
    _, SEQ, D = x.shape
    CTX = ctx.shape[1]
    L = w_in.shape[0]
    CWs = conv_dw_w.shape[2]
    CW = 4 * CWs
    RW = 4 * w_out.shape[1] - CW
    H = RW // DH
    CFs = ffn_dw_w.shape[-1]
    CF = 4 * CFs
    NMs = w_mod.shape[2]
    T = CTX + SEQ
    ncr, ncc = CTX // R, CTX // RC
    assert CTX == R and RW == CW and RW % DH == 0 and SEQ % R == 0 and R % GRID_W == 0
    assert w_in.shape[2] * 4 == 4 * RW + 2 * CW and NMs * 4 == N_MOD * D

    mx, my, mc = _place()
    me = 4 * mx + 2 * my + mc
    jm = 2 * mx + my
    c_idx = jnp.reshape(mc, (1,)).astype(jnp.int32)

    shapes0 = [(D,), (L, CONV_K, CWs), (L, 9, CFs)]
    g0, _ = gather_sum(_pack([c[0], conv_dw_w, ffn_dw_w.reshape(L, 9, CFs)]), "gather_cond")
    c_all, cw_all, fw_all = _unpack(g0, shapes0)
    conv_w = jnp.concatenate([cw_all[2 * j] for j in range(4)], axis=-1)
    ffn_w = jnp.concatenate([fw_all[2 * j] for j in range(4)], axis=-1)
    conv_w32 = jnp.pad(conv_w, ((0, 0), (0, 32 - CONV_K), (0, 0)))
    ffn_w16 = jnp.pad(ffn_w, ((0, 0), (0, 7), (0, 0)))
    cs = jnp.concatenate([c_all, c_ctx[None, :], jnp.zeros((7, D), F32)], axis=0)
    mod_shard = mod_fwd(cs, w_mod, "mod_fwd")
    g1, _ = gather_sum(mod_shard.reshape(-1, 128), "gather_mod")
    mod_all = g1.reshape(8, L, 16, NMs)
    mod_full = jnp.concatenate([mod_all[2 * j] for j in range(4)], axis=-1) + b_mod[:, None, :]
    mod_mine = lax.dynamic_index_in_dim(mod_full, me, axis=1, keepdims=False)
    modv = jnp.stack([mod_full[:, 8], mod_mine], axis=1).reshape(L, 2, N_MOD, D)

    big = {"w_in": (w_in, 'n'), "w_out": (w_out, 'k'), "w_up": (ffn_w_up, 'n'), "w_down": (ffn_w_down, 'k')}
    wb = {k: cast_bf16(w, f"cast_{k}") for k, (w, _) in big.items()}
    wg = {k: [gather_weights(wb[k][0:1], f"gather_{k}")[0]] for k in big}
    nxt = lambda k, l: wb[k][l + 1] if l + 1 < L else None

    cosT, sinT = _rope_tables(SEQ, CTX)
    lgt = [jnp.broadcast_to(jnp.concatenate([jax.nn.log_sigmoid(ret_decay_f[l]), jax.nn.log_sigmoid(ret_decay_b[l])])[:, None],
                            (2 * H, DH)) for l in range(L)]
    sm31, sm9 = _row_shift_matrices(), _col_shift_matrices()
    row = lambda t: t.reshape(1, -1)

    def with_next(res, k):
        if isinstance(res, (list, tuple)):
            wg[k].append(res[1])
            return res[0]
        return res

    xs = jnp.concatenate([ctx[0], x[0]], axis=0)
    saved = []
    for l in range(L):
        h1 = norm_mod(xs, row(norm1_g[l]), modv[l], 0, 1, ncr, "norm_mod")
        p = with_next(mm_nn(h1, wg["w_in"][l], 'n', BF16, "mm_in", nxt("w_in", l)), "w_in")
        uc = conv31_fwd(p, conv_w32[l], row(conv_dw_b[l]), sm31, RW, CW, ncr, "conv31_fwd")
        o_f, o_b, stf, stb = ret_fwd(p, cosT, sinT, lgt[l], H, ncc, "ret_fwd")
        mix = mix_fwd(o_f, o_b, p, uc, row(conv_ln_g[l]), row(conv_ln_b[l]), H, "mix_fwd")
        y1 = with_next(mm_nn(mix, wg["w_out"][l], 'k', F32, "mm_out", nxt("w_out", l)), "w_out")
        x2 = gate_res(xs, y1, modv[l], 2, ncr, "gate_res")
        h2 = norm_mod(x2, row(norm2_g[l]), modv[l], 3, 4, ncr, "norm_mod")
        up = with_next(mm_nn(h2, wg["w_up"][l], 'n', BF16, "mm_up", nxt("w_up", l)), "w_up")
        act = ffn_act(up, ffn_w16[l], row(ffn_dw_b[l]), sm9, CF, ncr, "ffn_act")
        y2 = with_next(mm_nn(act, wg["w_down"][l], 'k', F32, "mm_down", nxt("w_down", l)), "w_down")
        x3 = gate_res(x2, y2, modv[l], 5, ncr, "gate_res")
        saved.append((xs, h1, p, uc, o_f, o_b, stf, stb, mix, y1, x2, h2, up, act, y2))
        xs = x3

    dx, acc_loss = loss_head(xs, loss_target[0], row(final_norm_g), ncr, "loss_head")
    loss = lax.psum(0.5 / D * jnp.sum(acc_loss[1]), ("x", "y", "c"))

    small, gbig = [None] * L, {k: [None] * L for k in big}
    for l in reversed(range(L)):
        x1, h1, p, uc, o_f, o_b, stf, stb, mix, y1, x2, h2, up, act, y2 = saved[l]
        dy2, ag2 = gate_res_bwd(dx, y2, modv[l], 5, ncr, "gate_res_bwd")
        cs_down = chip_sums(mm_tn(act, dy2, 4, 'k', "mm_down_dw"), c_idx, "w_down")
        dact, parts = mm_nt(dy2, wg["w_down"][l], 'k', BF16, "mm_down_dx", cs_down)
        gbig["w_down"][l] = finish_grad(parts, "w_down")
        dgc, dup, accb = ffn_act_bwd1(up, dact, ffn_w16[l], row(ffn_dw_b[l]), sm9, CF, ncr, "ffn_act_bwd1")
        dup, accfw = ffn_act_bwd2(dgc, up, dup, ffn_w16[l], sm9, CF, ncr, "ffn_act_bwd2")
        cs_up = chip_sums(mm_tn(h2, dup, 4, 'n', "mm_up_dw"), c_idx, "w_up")
        dh2, parts = mm_nt(dup, wg["w_up"][l], 'n', F32, "mm_up_dx", cs_up)
        gbig["w_up"][l] = finish_grad(parts, "w_up")
        dx2, an2 = norm_mod_bwd(x2, row(norm2_g[l]), modv[l], dh2, dx, 4, ncr, "norm_mod_bwd")
        dy1, ag1 = gate_res_bwd(dx2, y1, modv[l], 2, ncr, "gate_res_bwd")
        cs_out = chip_sums(mm_tn(mix, dy1, 4, 'k', "mm_out_dw"), c_idx, "w_out")
        dmix, parts = mm_nt(dy1, wg["w_out"][l], 'k', BF16, "mm_out_dx", cs_out)
        gbig["w_out"][l] = finish_grad(parts, "w_out")
        do, dg, duc, accln = mix_bwd(dmix, o_f, o_b, p, uc, row(conv_ln_g[l]), row(conv_ln_b[l]), H, "mix_bwd")
        da, dbg, acccw = conv31_bwd(duc, p, conv_w32[l], sm31, RW, CW, ncr, "conv31_bwd")
        dqf, dkf, dvf, dqb, dkb, dvb, glg = ret_bwd(p, do, cosT, sinT, lgt[l], stf, stb, H, ncc, "ret_bwd")
        dp = assemble_dp(dqf, dqb, dkf, dkb, dvf, dvb, dg, da, dbg, "assemble_dp")
        cs_in = chip_sums(mm_tn(h1, dp, 4, 'n', "mm_in_dw"), c_idx, "w_in")
        dh1, parts = mm_nt(dp, wg["w_in"][l], 'n', F32, "mm_in_dx", cs_in)
        gbig["w_in"][l] = finish_grad(parts, "w_in")
        dx, an1 = norm_mod_bwd(x1, row(norm1_g[l]), modv[l], dh1, dx2, 1, ncr, "norm_mod_bwd")
        dmod = jnp.stack([jnp.stack([an1[0], an1[1], ag1[0], an2[0], an2[1], ag2[0]]),
                          jnp.stack([an1[2], an1[3], ag1[1], an2[2], an2[3], ag2[1]])])
        dlg = jnp.sum(glg, axis=(1, 2))
        dth = dlg * jnp.concatenate([jax.nn.sigmoid(-ret_decay_f[l]), jax.nn.sigmoid(-ret_decay_b[l])])
        small[l] = [dmod, an1[4], an2[4], acccw[31], accln[0], accln[1], acccw[:CONV_K], accfw[:9], accb[0], dth]
    grad_x = dx[CTX:][None]

    shapes1 = [(2, N_MOD, D), (D,), (D,), (CW,), (CW,), (CW,), (CONV_K, CW), (9, CF), (CF,), (2 * H,)]
    flat_parts = [t for l in range(L) for t in small[l]] + [acc_loss[0]]
    g2, s2 = gather_sum(_pack(flat_parts), "gather_small_grads")
    sums = _unpack(s2, shapes1 * L + [(D,)])
    per_dev = _unpack(g2, shapes1 * L + [(D,)])
    nS = len(shapes1)
    col = lambda i: jnp.stack([sums[l * nS + i] for l in range(L)])
    dmod_sum = col(0)
    dmod_dev = jnp.stack([per_dev[l * nS] for l in range(L)], axis=0)
    g_b_mod = (dmod_sum[:, 0] + dmod_sum[:, 1]).reshape(L, N_MOD * D)
    g_norm1, g_norm2 = col(1), col(2)
    g_conv_b, g_ln_g, g_ln_b = col(3), col(4), col(5)
    g_conv_w = lax.dynamic_slice_in_dim(col(6), jm * CWs, CWs, axis=2)
    g_ffn_w = lax.dynamic_slice_in_dim(col(7), jm * CFs, CFs, axis=2).reshape(L, 3, 3, CFs)
    g_ffn_b = col(8)
    g_ret = col(9)
    g_final = sums[-1]

    dmod_rows = jnp.concatenate([dmod_dev[:, :, 1].reshape(L, 8, N_MOD * D), dmod_sum[:, 0].reshape(L, 1, N_MOD * D),
                                 jnp.zeros((L, 7, N_MOD * D), F32)], axis=1)
    dmod_shard = lax.dynamic_slice_in_dim(dmod_rows, jm * NMs, NMs, axis=2)
    g_w_mod, ds_part = mod_bwd(cs, w_mod, dmod_shard, "mod_bwd")
    _, ds_sum = gather_sum(ds_part.reshape(-1, 128), "gather_dsilu")
    ds_ctx = 0.5 * ds_sum.reshape(16, D)[8]
    sg = jax.nn.sigmoid(c_ctx)
    g_c_ctx = ds_ctx * (sg * (1.0 + c_ctx * (1.0 - sg)))

    grads = {
        "c_ctx": g_c_ctx, "w_mod": g_w_mod, "b_mod": g_b_mod, "norm1_g": g_norm1, "norm2_g": g_norm2,
        "w_in": jnp.stack(gbig["w_in"]), "ret_decay_f": g_ret[:, :H], "ret_decay_b": g_ret[:, H:],
        "conv_dw_w": g_conv_w, "conv_dw_b": g_conv_b, "conv_ln_g": g_ln_g, "conv_ln_b": g_ln_b,
        "w_out": jnp.stack(gbig["w_out"]), "ffn_w_up": jnp.stack(gbig["w_up"]), "ffn_dw_w": g_ffn_w,
        "ffn_dw_b": g_ffn_b, "ffn_w_down": jnp.stack(gbig["w_down"]), "final_norm_g": g_final,
    }
    params = {
        "c_ctx": (c_ctx, m_c_ctx, v_c_ctx), "w_mod": (w_mod, m_w_mod, v_w_mod), "b_mod": (b_mod, m_b_mod, v_b_mod),
        "norm1_g": (norm1_g, m_norm1_g, v_norm1_g), "norm2_g": (norm2_g, m_norm2_g, v_norm2_g),
        "w_in": (w_in, m_w_in, v_w_in), "ret_decay_f": (ret_decay_f, m_ret_decay_f, v_ret_decay_f),
        "ret_decay_b": (ret_decay_b, m_ret_decay_b, v_ret_decay_b),
        "conv_dw_w": (conv_dw_w, m_conv_dw_w, v_conv_dw_w), "conv_dw_b": (conv_dw_b, m_conv_dw_b, v_conv_dw_b),
        "conv_ln_g": (conv_ln_g, m_conv_ln_g, v_conv_ln_g), "conv_ln_b": (conv_ln_b, m_conv_ln_b, v_conv_ln_b),
        "w_out": (w_out, m_w_out, v_w_out), "ffn_w_up": (ffn_w_up, m_ffn_w_up, v_ffn_w_up),
        "ffn_dw_w": (ffn_dw_w, m_ffn_dw_w, v_ffn_dw_w), "ffn_dw_b": (ffn_dw_b, m_ffn_dw_b, v_ffn_dw_b),
        "ffn_w_down": (ffn_w_down, m_ffn_w_down, v_ffn_w_down),
        "final_norm_g": (final_norm_g, m_final_norm_g, v_final_norm_g),
    }
    names = list(params)
    upd = {n: adamw(params[n][0], grads[n], params[n][1], params[n][2], f"adamw_{n}") for n in names}
    return (loss, grad_x, *[grads[n] for n in names], *[upd[n][0] for n in names],
            *[upd[n][1] for n in names], *[upd[n][2] for n in names])
```

```python
import jax
import jax.numpy as jnp
from jax import lax
from jax.experimental import pallas as pl
from jax.experimental.pallas import tpu as pltpu

F32 = jnp.float32
BF16 = jnp.bfloat16
EPS = 1e-6
DH = 128
RC = 128
GRID_W = 64
ROPE_THETA = 10000.0
N_MOD = 6
R = 256
HALO = 16
CONV_K = 31
SUB = 64
LANES = 256
VMEM_LIMIT = 48 * 1024 * 1024
MESH = pl.DeviceIdType.MESH
ADAM_LR, ADAM_B1, ADAM_B2, ADAM_EPS, ADAM_WD, ADAM_STEP = 0.001, 0.9, 0.999, 1e-08, 0.01, 10


def _cp(n):
    return pltpu.CompilerParams(dimension_semantics=("arbitrary",) * n, vmem_limit_bytes=VMEM_LIMIT)


def _cp0():
    return pltpu.CompilerParams(vmem_limit_bytes=VMEM_LIMIT)


def _sig(v):
    return 1.0 / (1.0 + jnp.exp(-v))


def _dot(a, b):
    return jnp.dot(a, b, preferred_element_type=F32)


def _dot_nt(a, b):
    return lax.dot_general(a, b, (((1,), (1,)), ((), ())), preferred_element_type=F32)


def _dot_tn(a, b):
    return lax.dot_general(a, b, (((0,), (0,)), ((), ())), preferred_element_type=F32)


def _lane_tile(n, cap):
    t = (min(n, cap) // 128) * 128
    while t >= 128:
        if n % t == 0:
            return t
        t -= 128
    raise ValueError(f"no lane tile for {n}")


def _row_tile(rows, cols, max_elems):
    if rows * cols <= max_elems:
        return rows
    t = (min(rows, max(8, max_elems // cols)) // 8) * 8
    while t >= 8:
        if rows % t == 0:
            return t
        t -= 8
    raise ValueError(f"no row tile for {rows}x{cols}")


def _place():
    return lax.axis_index("x"), lax.axis_index("y"), lax.axis_index("c")


def _other_chips(x, y):
    return [(1 - x, y), (x, 1 - y), (1 - x, 1 - y)]


def _first_last(grid):
    ids = [pl.program_id(k) for k in range(len(grid))]
    first, last = ids[0] == 0, ids[0] == grid[0] - 1
    for k in range(1, len(grid)):
        first = jnp.logical_and(first, ids[k] == 0)
        last = jnp.logical_and(last, ids[k] == grid[k] - 1)
    return first, last


def _side_copies(src_slab, dst_ref, send_sems, recv_sems, local_sem):
    x, y, c = _place()
    jm = 2 * x + y
    chips = _other_chips(x, y)
    mine = pltpu.make_async_copy(src_slab(jm), dst_ref.at[jm], local_sem)
    sends = [pltpu.make_async_remote_copy(src_ref=src_slab(2 * px + py), dst_ref=dst_ref.at[jm], send_sem=send_sems.at[t],
                                          recv_sem=recv_sems.at[t], device_id=(px, py, c), device_id_type=MESH)
             for t, (px, py) in enumerate(chips)]
    recvs = [pltpu.make_async_remote_copy(src_ref=dst_ref.at[2 * px + py], dst_ref=dst_ref.at[2 * px + py],
                                          send_sem=send_sems.at[t], recv_sem=recv_sems.at[t], device_id=(px, py, c),
                                          device_id_type=MESH) for t, (px, py) in enumerate(chips)]
    return mine, sends, recvs


def _side_start(mine, sends, recvs):
    mine.start()
    for s in sends:
        s.start()


def _side_finish(mine, sends, recvs):
    for r in recvs:
        r.wait_recv()
    for s in sends:
        s.wait_send()
    mine.wait()


_SIDE_SEMS = [pltpu.SemaphoreType.DMA((3,)), pltpu.SemaphoreType.DMA((3,)), pltpu.SemaphoreType.DMA]


def mm_nn(a, b, shard, out_dtype, name, bcast=None, res=None):
    M, K = a.shape
    S, Kb, Nb = b.shape
    N = S * Nb if shard == 'n' else Nb
    assert K == (Kb if shard == 'n' else S * Kb)
    tm, tn, tk = _lane_tile(M, 768), _lane_tile(Nb, 1536), _lane_tile(Kb, 2048)
    nk = K // tk
    grid = (M // tm, N // tn, nk)
    if shard == 'n':
        tps = Nb // tn
        b_map = lambda i, j, k: (j // tps, k, j % tps)
    else:
        kps = Kb // tk
        b_map = lambda i, j, k: (k // kps, k % kps, j)
    n_in = 2 + (2 if res is not None else 0) + (1 if bcast is not None else 0)
    n_out = 1 + (1 if res is not None else 0) + (1 if bcast is not None else 0)

    def body(*refs):
        ins, outs, scr = list(refs[:n_in]), list(refs[n_in:n_in + n_out]), list(refs[n_in + n_out:])
        a_ref, b_ref = ins[0], ins[1]
        o_ref, acc_ref = outs[0], scr[0]
        if bcast is not None:
            first, last = _first_last(grid)
            side = _side_copies(lambda j: ins[-1], outs[-1], *scr[1:])
            pl.when(first)(lambda: _side_start(*side))
        k = pl.program_id(2)

        @pl.when(k == 0)
        def _():
            acc_ref[...] = jnp.zeros_like(acc_ref)

        acc_ref[...] += _dot(a_ref[...], b_ref[...])

        @pl.when(k == nk - 1)
        def _():
            acc = acc_ref[...]
            o_ref[...] = acc.astype(o_ref.dtype)
            if res is not None:
                x_ref, g_ref = ins[2], ins[3]
                rows = pl.program_id(0) * tm + lax.broadcasted_iota(jnp.int32, (tm, 1), 0)
                gate = jnp.where(rows < res[2], g_ref[0:1, :], g_ref[1:2, :])
                outs[1][...] = x_ref[...] + gate * acc

        if bcast is not None:
            pl.when(last)(lambda: _side_finish(*side))

    hbm = pl.BlockSpec(memory_space=pl.ANY)
    tile = pl.BlockSpec((tm, tn), lambda i, j, k: (i, j))
    in_specs = [pl.BlockSpec((tm, tk), lambda i, j, k: (i, k)), pl.BlockSpec((None, tk, tn), b_map)]
    out_specs = [tile]
    out_shape = [jax.ShapeDtypeStruct((M, N), out_dtype)]
    scratch = [pltpu.VMEM((tm, tn), F32)]
    args = [a, b]
    if res is not None:
        in_specs += [tile, pl.BlockSpec((2, tn), lambda i, j, k: (0, j))]
        out_specs.append(tile)
        out_shape.append(jax.ShapeDtypeStruct((M, N), F32))
        args += [res[0], res[1]]
    if bcast is not None:
        in_specs.append(hbm)
        out_specs.append(hbm)
        out_shape.append(jax.ShapeDtypeStruct((4,) + bcast.shape, bcast.dtype))
        scratch += _SIDE_SEMS
        args.append(bcast)
    got = pl.pallas_call(body, grid=grid, in_specs=in_specs, out_specs=out_specs, out_shape=out_shape,
                         scratch_shapes=scratch, compiler_params=_cp(3), name=name)(*args)
    return got[0], (got[1] if res is not None else None), (got[-1] if bcast is not None else None)


def mm_nt(a, b, shard, out_dtype, name, exch=None):
    M, N = a.shape
    S, Kb, Nb = b.shape
    K = Kb if shard == 'n' else S * Kb
    assert N == (S * Nb if shard == 'n' else Nb)
    tm, tko, tnr = _lane_tile(M, 768), _lane_tile(Kb, 1536), _lane_tile(Nb, 1536)
    nr = N // tnr
    grid = (M // tm, K // tko, nr)
    if shard == 'n':
        tps = Nb // tnr
        b_map = lambda i, j, r: (r // tps, j, r % tps)
    else:
        kps = Kb // tko
        b_map = lambda i, j, r: (j // kps, j % kps, r)

    def body(a_ref, b_ref, *rest):
        if exch is None:
            o_ref, acc_ref = rest
        else:
            e_ref, o_ref, p_ref, acc_ref, send_sems, recv_sems, local_sem = rest
            first, last = _first_last(grid)
            side = _side_copies(lambda j: e_ref.at[j], p_ref, send_sems, recv_sems, local_sem)
            pl.when(first)(lambda: _side_start(*side))
        r = pl.program_id(2)

        @pl.when(r == 0)
        def _():
            acc_ref[...] = jnp.zeros_like(acc_ref)

        acc_ref[...] += _dot_nt(a_ref[...], b_ref[...])

        @pl.when(r == nr - 1)
        def _():
            o_ref[...] = acc_ref[...].astype(o_ref.dtype)

        if exch is not None:
            pl.when(last)(lambda: _side_finish(*side))

    hbm = pl.BlockSpec(memory_space=pl.ANY)
    in_specs = [pl.BlockSpec((tm, tnr), lambda i, j, r: (i, r)), pl.BlockSpec((None, tko, tnr), b_map)]
    out_specs = [pl.BlockSpec((tm, tko), lambda i, j, r: (i, j))]
    out_shape = [jax.ShapeDtypeStruct((M, K), out_dtype)]
    scratch = [pltpu.VMEM((tm, tko), F32)]
    args = [a, b]
    if exch is not None:
        in_specs.append(hbm)
        out_specs.append(hbm)
        out_shape.append(jax.ShapeDtypeStruct(exch.shape, exch.dtype))
        scratch += _SIDE_SEMS
        args.append(exch)
    res = pl.pallas_call(body, grid=grid, in_specs=in_specs, out_specs=out_specs, out_shape=out_shape,
                         scratch_shapes=scratch, compiler_params=_cp(3), name=name)(*args)
    return res[0] if exch is None else res


def mm_tn(a, c, S, shard, name):
    M, K = a.shape
    N = c.shape[1]
    Kb, Nb = (K, N // S) if shard == 'n' else (K // S, N)
    tm, tk, tn = _lane_tile(M, 768), _lane_tile(Kb, 1536), _lane_tile(Nb, 1536)
    nm = M // tm
    if shard == 'n':
        tps = Nb // tn
        o_map = lambda i, j, m: (j // tps, i, j % tps)
    else:
        kps = Kb // tk
        o_map = lambda i, j, m: (i // kps, i % kps, j)

    def body(a_ref, c_ref, o_ref, acc_ref):
        m = pl.program_id(2)

        @pl.when(m == 0)
        def _():
            acc_ref[...] = jnp.zeros_like(acc_ref)

        acc_ref[...] += _dot_tn(a_ref[...], c_ref[...])

        @pl.when(m == nm - 1)
        def _():
            o_ref[...] = acc_ref[...].astype(BF16)

    return pl.pallas_call(
        body, grid=(K // tk, N // tn, nm),
        in_specs=[pl.BlockSpec((tm, tk), lambda i, j, m: (m, i)), pl.BlockSpec((tm, tn), lambda i, j, m: (m, j))],
        out_specs=pl.BlockSpec((None, tk, tn), o_map),
        out_shape=jax.ShapeDtypeStruct((S, Kb, Nb), BF16),
        scratch_shapes=[pltpu.VMEM((tk, tn), F32)], compiler_params=_cp(3), name=name)(a, c)


def _mod_spec(D, ncr):
    return pl.BlockSpec((None, N_MOD, D), lambda i: (jnp.where(i < ncr, 0, 1), 0, 0))


def norm_mod(x, g, modv, i_sh, i_sc, ncr, name):
    T, D = x.shape

    def body(x_ref, g_ref, m_ref, h_ref):
        xv = x_ref[...]
        r = lax.rsqrt(jnp.mean(xv * xv, axis=-1, keepdims=True) + EPS)
        n = xv * r * g_ref[...]
        h_ref[...] = (n * (1.0 + m_ref[i_sc:i_sc + 1, :]) + m_ref[i_sh:i_sh + 1, :]).astype(BF16)

    return pl.pallas_call(
        body, grid=(T // R,),
        in_specs=[pl.BlockSpec((R, D), lambda i: (i, 0)), pl.BlockSpec((1, D), lambda i: (0, 0)), _mod_spec(D, ncr)],
        out_specs=pl.BlockSpec((R, D), lambda i: (i, 0)),
        out_shape=jax.ShapeDtypeStruct((T, D), BF16), compiler_params=_cp(1), name=name)(x, g, modv)


def norm_mod_bwd(x, g, modv, dh, dres, i_sc, ncr, name):
    T, D = x.shape

    def body(x_ref, g_ref, m_ref, dh_ref, dr_ref, dx_ref, acc_ref):
        i = pl.program_id(0)

        @pl.when(i == 0)
        def _():
            acc_ref[...] = jnp.zeros_like(acc_ref)

        xv = x_ref[...]
        r = lax.rsqrt(jnp.mean(xv * xv, axis=-1, keepdims=True) + EPS)
        xh = xv * r
        gv = g_ref[...]
        dhv = dh_ref[...]
        dn = dhv * (1.0 + m_ref[i_sc:i_sc + 1, :])
        s_sh = jnp.sum(dhv, axis=0, keepdims=True)
        s_sc = jnp.sum(dhv * (xh * gv), axis=0, keepdims=True)
        acc_ref[4:5, :] += jnp.sum(dn * xh, axis=0, keepdims=True)
        dxh = dn * gv
        dx_ref[...] = dr_ref[...] + r * (dxh - xh * jnp.mean(dxh * xh, axis=-1, keepdims=True))

        @pl.when(i < ncr)
        def _():
            acc_ref[0:1, :] += s_sh
            acc_ref[1:2, :] += s_sc

        @pl.when(i >= ncr)
        def _():
            acc_ref[2:3, :] += s_sh
            acc_ref[3:4, :] += s_sc

    row = pl.BlockSpec((R, D), lambda i: (i, 0))
    return pl.pallas_call(
        body, grid=(T // R,),
        in_specs=[row, pl.BlockSpec((1, D), lambda i: (0, 0)), _mod_spec(D, ncr), row, row],
        out_specs=[row, pl.BlockSpec((8, D), lambda i: (0, 0))],
        out_shape=[jax.ShapeDtypeStruct((T, D), F32), jax.ShapeDtypeStruct((8, D), F32)],
        compiler_params=_cp(1), name=name)(x, g, modv, dh, dres)


def gate_res_bwd(dx, y, modv, i_g, ncr, name):
    T, D = dx.shape

    def body(dx_ref, y_ref, m_ref, dy_ref, acc_ref):
        i = pl.program_id(0)

        @pl.when(i == 0)
        def _():
            acc_ref[...] = jnp.zeros_like(acc_ref)

        dxv = dx_ref[...]
        dy_ref[...] = (m_ref[i_g:i_g + 1, :] * dxv).astype(BF16)
        s = jnp.sum(dxv * y_ref[...], axis=0, keepdims=True)

        @pl.when(i < ncr)
        def _():
            acc_ref[0:1, :] += s

        @pl.when(i >= ncr)
        def _():
            acc_ref[1:2, :] += s

    row = pl.BlockSpec((R, D), lambda i: (i, 0))
    return pl.pallas_call(
        body, grid=(T // R,), in_specs=[row, row, _mod_spec(D, ncr)],
        out_specs=[row, pl.BlockSpec((8, D), lambda i: (0, 0))],
        out_shape=[jax.ShapeDtypeStruct((T, D), BF16), jax.ShapeDtypeStruct((8, D), F32)],
        compiler_params=_cp(1), name=name)(dx, y, modv)


def loss_head(x, target, g, ncr, name):
    T, D = x.shape

    def body(x_ref, t_ref, g_ref, dx_ref, acc_ref):
        i = pl.program_id(0)

        @pl.when(i == 0)
        def _():
            acc_ref[...] = jnp.zeros_like(acc_ref)

        @pl.when(i < ncr)
        def _():
            dx_ref[...] = jnp.zeros_like(dx_ref)

        @pl.when(i >= ncr)
        def _():
            xv = x_ref[...]
            r = lax.rsqrt(jnp.mean(xv * xv, axis=-1, keepdims=True) + EPS)
            xh = xv * r
            gv = g_ref[...]
            e = xh * gv - t_ref[...]
            acc_ref[1:2, :] += jnp.sum(e * e, axis=0, keepdims=True)
            dy = e * (1.0 / D)
            acc_ref[0:1, :] += jnp.sum(dy * xh, axis=0, keepdims=True)
            dxh = dy * gv
            dx_ref[...] = r * (dxh - xh * jnp.mean(dxh * xh, axis=-1, keepdims=True))

    row = pl.BlockSpec((R, D), lambda i: (i, 0))
    return pl.pallas_call(
        body, grid=(T // R,),
        in_specs=[row, pl.BlockSpec((R, D), lambda i: (jnp.maximum(i - ncr, 0), 0)), pl.BlockSpec((1, D), lambda i: (0, 0))],
        out_specs=[row, pl.BlockSpec((8, D), lambda i: (0, 0))],
        out_shape=[jax.ShapeDtypeStruct((T, D), F32), jax.ShapeDtypeStruct((8, D), F32)],
        compiler_params=_cp(1), name=name)(x, target, g)


def _conv31_specs(T):
    nh = R // HALO
    pv = lambda i: jnp.maximum(i * nh - 1, 0)
    nx = lambda i: jnp.minimum((i + 1) * nh, T // HALO - 1)
    return pv, nx


def _shifted_copies(E):
    n = E.shape[1]
    for s in range(1, 8):
        E[s, 0:n - 8, :] = E[0, pl.ds(s, n - 8), :]


def _tap31(E, r0, o):
    return E[o % 8, pl.ds(r0 + 8 * (o // 8), SUB), :]


def conv31_fwd(p, w32, b, RW, CW, ncr, name):
    T = p.shape[0]
    nT, cbk = T // R, LANES
    n = R + 2 * HALO
    a0, g0 = 4 * RW // cbk, (4 * RW + CW) // cbk
    pv, nx = _conv31_specs(T)

    def body(a, g, ap, gp, an, gn, w, bb, uc, E):
        i = pl.program_id(0)
        has_prev = jnp.logical_and(i != 0, i != ncr)
        has_next = jnp.logical_and(i != ncr - 1, i != nT - 1)
        glu = lambda u, v: u.astype(F32) * _sig(v.astype(F32))
        E[0, 0:HALO, :] = jnp.where(has_prev, glu(ap[...], gp[...]), 0.0)
        E[0, HALO:HALO + R, :] = glu(a[...], g[...])
        E[0, HALO + R:, :] = jnp.where(has_next, glu(an[...], gn[...]), 0.0)
        _shifted_copies(E)
        for r0 in range(0, R, SUB):
            acc = jnp.broadcast_to(bb[...], (SUB, cbk))
            for k in range(CONV_K):
                acc = acc + w[k:k + 1, :] * _tap31(E, r0, 1 + k)
            uc[r0:r0 + SUB, :] = acc

    cur = lambda c0: pl.BlockSpec((R, cbk), lambda i, j: (i, c0 + j))
    hp = lambda c0: pl.BlockSpec((HALO, cbk), lambda i, j: (pv(i), c0 + j))
    hn = lambda c0: pl.BlockSpec((HALO, cbk), lambda i, j: (nx(i), c0 + j))
    return pl.pallas_call(
        body, grid=(nT, CW // cbk),
        in_specs=[cur(a0), cur(g0), hp(a0), hp(g0), hn(a0), hn(g0),
                  pl.BlockSpec((32, cbk), lambda i, j: (0, j)), pl.BlockSpec((1, cbk), lambda i, j: (0, j))],
        out_specs=pl.BlockSpec((R, cbk), lambda i, j: (i, j)),
        out_shape=jax.ShapeDtypeStruct((T, CW), F32),
        scratch_shapes=[pltpu.VMEM((8, n, cbk), F32)],
        compiler_params=_cp(2), name=name)(p, p, p, p, p, p, w32, b)


def conv31_bwd(duc, p, w32, RW, CW, ncr, name):
    T = p.shape[0]
    nT, cbk = T // R, LANES
    n = R + 2 * HALO
    a0, g0 = 4 * RW // cbk, (4 * RW + CW) // cbk
    pv, nx = _conv31_specs(T)

    def body(d, dp_, dn_, a, g, ap, gp, an, gn, w, da, dg, accw, U, Dd):
        i = pl.program_id(1)

        @pl.when(i == 0)
        def _():
            accw[...] = jnp.zeros_like(accw)

        has_prev = jnp.logical_and(i != 0, i != ncr)
        has_next = jnp.logical_and(i != ncr - 1, i != nT - 1)
        glu = lambda u, v: u.astype(F32) * _sig(v.astype(F32))
        U[0, 0:HALO, :] = jnp.where(has_prev, glu(ap[...], gp[...]), 0.0)
        U[0, HALO:HALO + R, :] = glu(a[...], g[...])
        U[0, HALO + R:, :] = jnp.where(has_next, glu(an[...], gn[...]), 0.0)
        Dd[0, 0:HALO, :] = jnp.where(has_prev, dp_[...], 0.0)
        Dd[0, HALO:HALO + R, :] = d[...]
        Dd[0, HALO + R:, :] = jnp.where(has_next, dn_[...], 0.0)
        _shifted_copies(U)
        _shifted_copies(Dd)
        for r0 in range(0, R, SUB):
            du = jnp.zeros((SUB, cbk), F32)
            for k in range(CONV_K):
                du = du + w[k:k + 1, :] * _tap31(Dd, r0, HALO + 15 - k)
            av = a[r0:r0 + SUB, :].astype(F32)
            sg = _sig(g[r0:r0 + SUB, :].astype(F32))
            da[r0:r0 + SUB, :] = (du * sg).astype(BF16)
            dg[r0:r0 + SUB, :] = (du * av * sg * (1.0 - sg)).astype(BF16)
            dcur = d[r0:r0 + SUB, :]
            for k in range(CONV_K):
                accw[k:k + 1, :] += jnp.sum(dcur * _tap31(U, r0, 1 + k), axis=0, keepdims=True)
            accw[31:32, :] += jnp.sum(dcur, axis=0, keepdims=True)

    cur = lambda c0: pl.BlockSpec((R, cbk), lambda j, i: (i, c0 + j))
    hp = lambda c0: pl.BlockSpec((HALO, cbk), lambda j, i: (pv(i), c0 + j))
    hn = lambda c0: pl.BlockSpec((HALO, cbk), lambda j, i: (nx(i), c0 + j))
    out = pl.BlockSpec((R, cbk), lambda j, i: (i, j))
    ext = pltpu.VMEM((8, n, cbk), F32)
    return pl.pallas_call(
        body, grid=(CW // cbk, nT),
        in_specs=[cur(0), hp(0), hn(0), cur(a0), cur(g0), hp(a0), hp(g0), hn(a0), hn(g0),
                  pl.BlockSpec((32, cbk), lambda j, i: (0, j))],
        out_specs=[out, out, pl.BlockSpec((32, cbk), lambda j, i: (0, j))],
        out_shape=[jax.ShapeDtypeStruct((T, CW), BF16), jax.ShapeDtypeStruct((T, CW), BF16),
                   jax.ShapeDtypeStruct((32, CW), F32)],
        scratch_shapes=[ext, ext],
        compiler_params=_cp(2), name=name)(duc, duc, duc, p, p, p, p, p, p, w32)


def _rope(v, cosv, sinv, first):
    swapped = jnp.where(first, pltpu.roll(v, 96, 1), pltpu.roll(v, 32, 1))
    return v * cosv + swapped * sinv


def _unrope(v, cosv, sinv, first):
    z = v * sinv
    return v * cosv + jnp.where(first, pltpu.roll(z, 96, 1), pltpu.roll(z, 32, 1))


def _decay_tables(lg_ref, H, DM, QD, KD, CD):
    n = lax.broadcasted_iota(jnp.int32, (RC, DH), 0).astype(F32)
    m = lax.broadcasted_iota(jnp.int32, (RC, DH), 1).astype(F32)
    for d in range(2):
        for h in range(H):
            i = d * H + h
            lg = lg_ref[i:i + 1, :]
            diff = (n - m) if d == 0 else (m - n)
            DM[i] = jnp.where(diff >= 0, jnp.exp(lg * jnp.maximum(diff, 0.0)), 0.0)
            QD[i] = jnp.exp(lg * ((n + 1.0) if d == 0 else (RC - n)))
            KD[i] = jnp.exp(lg * ((RC - 1.0 - n) if d == 0 else n))
            CD[i] = jnp.exp(lg * float(RC)) + jnp.zeros((RC, DH), F32)


def _chunk_orders(NC, ncc):
    cf = lambda s: s
    cb = lambda s: jnp.where(s < ncc, ncc - 1 - s, NC - 1 - (s - ncc))
    return cf, cb


def ret_fwd(p, cosT, sinT, lgt, H, ncc, name):
    T = p.shape[0]
    RW, NC = H * DH, T // RC
    cf, cb = _chunk_orders(NC, ncc)
    scale = DH ** -0.5

    def body(qf, kf, vf, qb, kb, vb, cosf, sinf, cosb, sinb, lg_ref, of_ref, ob_ref, sf_ref, sb_ref, S, DM, QD, KD, CD):
        s = pl.program_id(0)

        @pl.when(s == 0)
        def _():
            S[...] = jnp.zeros_like(S)
            _decay_tables(lg_ref, H, DM, QD, KD, CD)

        first = (lax.broadcasted_iota(jnp.int32, (RC, DH), 1) % 64) < 32
        for d, (q_ref, k_ref, v_ref, c_ref, s_ref, o_ref, st_ref) in enumerate(
                ((qf, kf, vf, cosf, sinf, of_ref, sf_ref), (qb, kb, vb, cosb, sinb, ob_ref, sb_ref))):
            cosv, sinv = c_ref[...], s_ref[...]
            for h in range(H):
                hs, i = slice(h * DH, (h + 1) * DH), d * H + h
                q16 = _rope(q_ref[:, hs].astype(F32), cosv, sinv, first).astype(BF16)
                k = _rope(k_ref[:, hs].astype(F32), cosv, sinv, first) * scale
                k16 = k.astype(BF16)
                v = v_ref[:, hs]
                s_in = S[i]
                s16 = s_in.astype(BF16)
                st_ref[h] = s16
                sc = _dot_nt(q16, k16) * DM[i]
                o_ref[:, hs] = _dot(sc.astype(BF16), v) + _dot(q16, s16) * QD[i]
                S[i] = s_in * CD[i] + _dot_tn((k * KD[i]).astype(BF16), v)

    pspec = lambda col, cm: pl.BlockSpec((RC, RW), lambda s: (cm(s), col))
    tspec = lambda cm: pl.BlockSpec((RC, DH), lambda s: (cm(s), 0))
    ospec = lambda cm: pl.BlockSpec((RC, RW), lambda s: (cm(s), 0))
    stspec = pl.BlockSpec((None, H, DH, DH), lambda s: (s, 0, 0, 0))
    tab = pltpu.VMEM((2 * H, RC, DH), F32)
    return pl.pallas_call(
        body, grid=(NC,),
        in_specs=[pspec(0, cf), pspec(1, cf), pspec(2, cf), pspec(0, cb), pspec(1, cb), pspec(2, cb),
                  tspec(cf), tspec(cf), tspec(cb), tspec(cb), pl.BlockSpec((2 * H, DH), lambda s: (0, 0))],
        out_specs=[ospec(cf), ospec(cb), stspec, stspec],
        out_shape=[jax.ShapeDtypeStruct((T, RW), F32), jax.ShapeDtypeStruct((T, RW), F32),
                   jax.ShapeDtypeStruct((NC, H, DH, DH), BF16), jax.ShapeDtypeStruct((NC, H, DH, DH), BF16)],
        scratch_shapes=[tab, tab, tab, tab, tab], compiler_params=_cp(1), name=name)(
            p, p, p, p, p, p, cosT, sinT, cosT, sinT, lgt)


def ret_bwd(p, do, cosT, sinT, lgt, stf, stb, H, ncc, name):
    T = p.shape[0]
    RW, NC = H * DH, T // RC
    cf0, cb0 = _chunk_orders(NC, ncc)
    cf = lambda sp: cf0(NC - 1 - sp)
    cb = lambda sp: cb0(NC - 1 - sp)
    scale = DH ** -0.5

    def body(qf, kf, vf, qb, kb, vb, dof, dob, cosf, sinf, cosb, sinb, lg_ref, stf_ref, stb_ref,
             dqf, dkf, dvf, dqb, dkb, dvb, glg, dS, DM, QD, KD, CD):
        sp = pl.program_id(0)

        @pl.when(sp == 0)
        def _():
            dS[...] = jnp.zeros_like(dS)
            glg[...] = jnp.zeros_like(glg)
            _decay_tables(lg_ref, H, DM, QD, KD, CD)

        first = (lax.broadcasted_iota(jnp.int32, (RC, DH), 1) % 64) < 32
        n = lax.broadcasted_iota(jnp.int32, (RC, DH), 0).astype(F32)
        m = lax.broadcasted_iota(jnp.int32, (RC, DH), 1).astype(F32)
        for d, (q_ref, k_ref, v_ref, do_ref, c_ref, s_ref, st_ref, dq_ref, dk_ref, dv_ref) in enumerate(
                ((qf, kf, vf, dof, cosf, sinf, stf_ref, dqf, dkf, dvf),
                 (qb, kb, vb, dob, cosb, sinb, stb_ref, dqb, dkb, dvb))):
            cosv, sinv = c_ref[...], s_ref[...]
            diff = (n - m) if d == 0 else (m - n)
            posq = (n + 1.0) if d == 0 else (RC - n)
            posk = (RC - 1.0 - n) if d == 0 else n
            for h in range(H):
                hs, i = slice(h * DH, (h + 1) * DH), d * H + h
                q = _rope(q_ref[:, hs].astype(F32), cosv, sinv, first)
                k = _rope(k_ref[:, hs].astype(F32), cosv, sinv, first) * scale
                q16, k16 = q.astype(BF16), k.astype(BF16)
                v = v_ref[:, hs]
                s_in = st_ref[h]
                ds_out = dS[i]
                ds16 = ds_out.astype(BF16)
                do16 = do_ref[:, hs]
                doq = (do16.astype(F32) * QD[i]).astype(BF16)
                a = _dot_nt(q16, k16) * DM[i]
                da_raw = _dot_nt(do16, v)
                da16 = (da_raw * DM[i]).astype(BF16)
                dq_state = _dot_nt(doq, s_in)
                dk_state = _dot_nt(v, ds16) * KD[i]
                dqr = _dot(da16, k16) + dq_state
                dkr = _dot_tn(da16, q16) + dk_state
                dv_ref[:, hs] = _dot_tn(a.astype(BF16), do16) + _dot((k * KD[i]).astype(BF16), ds16)
                dS[i] = ds_out * CD[i] + _dot_tn(q16, doq)
                glg[i] += (da_raw * a * diff + posq * q * dq_state + posk * k * dk_state
                           + float(RC) * CD[i] * ds_out * s_in.astype(F32))
                dq_ref[:, hs] = _unrope(dqr, cosv, sinv, first)
                dk_ref[:, hs] = _unrope(dkr, cosv, sinv, first) * scale

    pspec = lambda col, cm: pl.BlockSpec((RC, RW), lambda s: (cm(s), col))
    tspec = lambda cm: pl.BlockSpec((RC, DH), lambda s: (cm(s), 0))
    ospec = lambda cm: pl.BlockSpec((RC, RW), lambda s: (cm(s), 0))
    stspec = pl.BlockSpec((None, H, DH, DH), lambda s: (NC - 1 - s, 0, 0, 0))
    tab = pltpu.VMEM((2 * H, RC, DH), F32)
    big = jax.ShapeDtypeStruct((T, RW), F32)
    return pl.pallas_call(
        body, grid=(NC,),
        in_specs=[pspec(0, cf), pspec(1, cf), pspec(2, cf), pspec(0, cb), pspec(1, cb), pspec(2, cb),
                  ospec(cf), ospec(cb), tspec(cf), tspec(cf), tspec(cb), tspec(cb),
                  pl.BlockSpec((2 * H, DH), lambda s: (0, 0)), stspec, stspec],
        out_specs=[ospec(cf), ospec(cf), ospec(cf), ospec(cb), ospec(cb), ospec(cb),
                   pl.BlockSpec((2 * H, RC, DH), lambda s: (0, 0, 0))],
        out_shape=[big, big, big, big, big, big, jax.ShapeDtypeStruct((2 * H, RC, DH), F32)],
        scratch_shapes=[tab, tab, tab, tab, tab], compiler_params=_cp(1), name=name)(
            p, p, p, p, p, p, do, do, cosT, sinT, cosT, sinT, lgt, stf, stb)


def mix_fwd(o_f, o_b, p, uc, lng, lnb, H, name):
    T, RW = o_f.shape
    CW = uc.shape[1]

    def body(of_ref, ob_ref, g_ref, uc_ref, lg_ref, lb_ref, out_ref):
        for h in range(H):
            hs = slice(h * DH, (h + 1) * DH)
            o = of_ref[:, hs] + ob_ref[:, hs]
            on = o * lax.rsqrt(jnp.mean(o * o, axis=-1, keepdims=True) + EPS)
            gv = g_ref[:, hs].astype(F32)
            out_ref[:, hs] = (gv * _sig(gv) * on).astype(BF16)
        u = uc_ref[...]
        mu = jnp.mean(u, axis=-1, keepdims=True)
        var = jnp.mean(jnp.square(u - mu), axis=-1, keepdims=True)
        z = (u - mu) * lax.rsqrt(var + EPS) * lg_ref[...] + lb_ref[...]
        out_ref[:, RW:] = (z * _sig(z)).astype(BF16)

    rw = pl.BlockSpec((R, RW), lambda i: (i, 0))
    vec = pl.BlockSpec((1, CW), lambda i: (0, 0))
    return pl.pallas_call(
        body, grid=(T // R,),
        in_specs=[rw, rw, pl.BlockSpec((R, RW), lambda i: (i, 3)), pl.BlockSpec((R, CW), lambda i: (i, 0)), vec, vec],
        out_specs=pl.BlockSpec((R, RW + CW), lambda i: (i, 0)),
        out_shape=jax.ShapeDtypeStruct((T, RW + CW), BF16), compiler_params=_cp(1), name=name)(o_f, o_b, p, uc, lng, lnb)


def mix_bwd(dmix, o_f, o_b, p, uc, lng, lnb, H, name):
    T, RW = o_f.shape
    CW = uc.shape[1]

    def body(dm_ref, of_ref, ob_ref, g_ref, uc_ref, lg_ref, lb_ref, do_ref, dg_ref, duc_ref, acc_ref):
        i = pl.program_id(0)

        @pl.when(i == 0)
        def _():
            acc_ref[...] = jnp.zeros_like(acc_ref)

        for h in range(H):
            hs = slice(h * DH, (h + 1) * DH)
            o = of_ref[:, hs] + ob_ref[:, hs]
            r = lax.rsqrt(jnp.mean(o * o, axis=-1, keepdims=True) + EPS)
            on = o * r
            gv = g_ref[:, hs].astype(F32)
            sg = _sig(gv)
            dmr = dm_ref[:, hs].astype(F32)
            dg_ref[:, hs] = (dmr * on * (sg * (1.0 + gv * (1.0 - sg)))).astype(BF16)
            don = dmr * (gv * sg)
            do_ref[:, hs] = (r * (don - on * jnp.mean(don * on, axis=-1, keepdims=True))).astype(BF16)
        u = uc_ref[...]
        mu = jnp.mean(u, axis=-1, keepdims=True)
        rs = lax.rsqrt(jnp.mean(jnp.square(u - mu), axis=-1, keepdims=True) + EPS)
        zh = (u - mu) * rs
        lg = lg_ref[...]
        z = zh * lg + lb_ref[...]
        sz = _sig(z)
        dz = dm_ref[:, RW:].astype(F32) * (sz * (1.0 + z * (1.0 - sz)))
        acc_ref[0:1, :] += jnp.sum(dz * zh, axis=0, keepdims=True)
        acc_ref[1:2, :] += jnp.sum(dz, axis=0, keepdims=True)
        dzh = dz * lg
        duc_ref[...] = rs * (dzh - jnp.mean(dzh, axis=-1, keepdims=True)
                             - zh * jnp.mean(dzh * zh, axis=-1, keepdims=True))

    rw = pl.BlockSpec((R, RW), lambda i: (i, 0))
    cw = pl.BlockSpec((R, CW), lambda i: (i, 0))
    vec = pl.BlockSpec((1, CW), lambda i: (0, 0))
    return pl.pallas_call(
        body, grid=(T // R,),
        in_specs=[pl.BlockSpec((R, RW + CW), lambda i: (i, 0)), rw, rw, pl.BlockSpec((R, RW), lambda i: (i, 3)), cw, vec, vec],
        out_specs=[rw, rw, cw, pl.BlockSpec((8, CW), lambda i: (0, 0))],
        out_shape=[jax.ShapeDtypeStruct((T, RW), BF16), jax.ShapeDtypeStruct((T, RW), BF16),
                   jax.ShapeDtypeStruct((T, CW), F32), jax.ShapeDtypeStruct((8, CW), F32)],
        compiler_params=_cp(1), name=name)(dmix, o_f, o_b, p, uc, lng, lnb)


def assemble_dp(dqf, dqb, dkf, dkb, dvf, dvb, dg, da, dbg, name):
    T, RW = dqf.shape
    CW = da.shape[1]

    def body(qf, qb, kf, kb, vf, vb, g, a, b, out):
        out[:, 0:RW] = (qf[...] + qb[...]).astype(BF16)
        out[:, RW:2 * RW] = (kf[...] + kb[...]).astype(BF16)
        out[:, 2 * RW:3 * RW] = (vf[...] + vb[...]).astype(BF16)
        out[:, 3 * RW:4 * RW] = g[...]
        out[:, 4 * RW:4 * RW + CW] = a[...]
        out[:, 4 * RW + CW:] = b[...]

    rw = pl.BlockSpec((R, RW), lambda i: (i, 0))
    cw = pl.BlockSpec((R, CW), lambda i: (i, 0))
    W = 4 * RW + 2 * CW
    return pl.pallas_call(
        body, grid=(T // R,), in_specs=[rw] * 7 + [cw, cw], out_specs=pl.BlockSpec((R, W), lambda i: (i, 0)),
        out_shape=jax.ShapeDtypeStruct((T, W), BF16), compiler_params=_cp(1), name=name)(
            dqf, dqb, dkf, dkb, dvf, dvb, dg, da, dbg)


FPAD = 8


def _fill_ext(bufs, cur, prv, nxt, has_prev, has_next, is_ctx):
    left, plain, right = bufs
    n, cb = left.shape
    pv, nv = prv[...], nxt[...]
    plain[0:FPAD, :] = jnp.zeros((FPAD, cb), F32)
    plain[FPAD + n:, :] = jnp.zeros((FPAD, cb), F32)
    plain[FPAD:FPAD + GRID_W, :] = jnp.where(has_prev, pv, jnp.zeros_like(pv)).astype(F32)
    plain[FPAD + GRID_W:FPAD + GRID_W + R, :] = cur[...].astype(F32)
    plain[FPAD + GRID_W + R:FPAD + n, :] = jnp.where(has_next, nv, jnp.zeros_like(nv)).astype(F32)
    left[...] = plain[pl.ds(FPAD - 1, n), :]
    right[...] = plain[pl.ds(FPAD + 1, n), :]
    if not is_ctx:
        for r in range(0, n, GRID_W):
            left[r:r + 1, :] = jnp.zeros((1, cb), F32)
            right[r + GRID_W - 1:r + GRID_W, :] = jnp.zeros((1, cb), F32)


def _taps(is_ctx):
    return [(dr, dc) for dr in ((0,) if is_ctx else (-1, 0, 1)) for dc in (-1, 0, 1)]


def _tap_src(bufs, r0, c0, dr, dc):
    off = (FPAD if dc == 0 else 0) + GRID_W + r0 + GRID_W * dr
    return bufs[dc + 1][pl.ds(off, SUB), pl.ds(c0, LANES)]


def _conv9(bufs, w, r0, c0, is_ctx, flip):
    acc = jnp.zeros((SUB, LANES), F32)
    for dr, dc in _taps(is_ctx):
        widx = (dr + 1) * 3 + dc + 1
        src = _tap_src(bufs, r0, c0, -dr, -dc) if flip else _tap_src(bufs, r0, c0, dr, dc)
        acc = acc + w[widx:widx + 1, pl.ds(c0, LANES)] * src
    return acc


def _ffn_specs(T, cb, order):
    nq = R // GRID_W
    pv = lambda i: jnp.maximum(i * nq - 1, 0)
    nx = lambda i: jnp.minimum((i + 1) * nq, T // GRID_W - 1)
    if order == 'ij':
        mk = lambda blk, rf, c0: pl.BlockSpec(blk, lambda i, j: (rf(i), c0 + j))
    else:
        mk = lambda blk, rf, c0: pl.BlockSpec(blk, lambda j, i: (rf(i), c0 + j))
    cur = lambda c0: mk((R, cb), lambda i: i, c0)
    hp = lambda c0: mk((GRID_W, cb), pv, c0)
    hn = lambda c0: mk((GRID_W, cb), nx, c0)
    vec = lambda rows: mk((rows, cb), lambda i: 0, 0)
    return cur, hp, hn, vec


def _ffn_flags(i, ncr, nT):
    return i > ncr, jnp.logical_and(i >= ncr, i != nT - 1)


def _ffn_scratch(cb, sets):
    n = R + 2 * GRID_W
    return [pltpu.VMEM((n, cb), F32), pltpu.VMEM((n + 2 * FPAD, cb), F32), pltpu.VMEM((n, cb), F32)] * sets


def ffn_act(up, w16, b, CF, ncr, name):
    T = up.shape[0]
    nT, cb = T // R, 2 * LANES
    ncb = CF // cb
    cur, hp, hn, vec = _ffn_specs(T, cb, 'ij')

    def body(g, v, gp, gn, w, bb, out, e0, e1, e2):
        i = pl.program_id(0)
        has_prev, has_next = _ffn_flags(i, ncr, nT)
        bufs = (e0, e1, e2)

        def run(is_ctx):
            _fill_ext(bufs, g, gp, gn, has_prev, has_next, is_ctx)
            for r0 in range(0, R, SUB):
                for c0 in range(0, cb, LANES):
                    gc = _conv9(bufs, w, r0, c0, is_ctx, False) + bb[:, pl.ds(c0, LANES)]
                    val = v[r0:r0 + SUB, c0:c0 + LANES].astype(F32)
                    out[r0:r0 + SUB, c0:c0 + LANES] = (gc * _sig(gc) * val).astype(BF16)

        pl.when(i < ncr)(lambda: run(True))
        pl.when(i >= ncr)(lambda: run(False))

    return pl.pallas_call(
        body, grid=(nT, ncb), in_specs=[cur(0), cur(ncb), hp(0), hn(0), vec(16), vec(1)], out_specs=cur(0),
        out_shape=jax.ShapeDtypeStruct((T, CF), BF16),
        scratch_shapes=_ffn_scratch(cb, 1), compiler_params=_cp(2), name=name)(up, up, up, up, w16, b)


def ffn_act_bwd1(up, dact, w16, b, CF, ncr, name):
    T = up.shape[0]
    nT, cb = T // R, 2 * LANES
    ncb = CF // cb
    cur, hp, hn, vec = _ffn_specs(T, cb, 'ji')

    def body(g, v, gp, gn, da, w, bb, dgc, dup, accb, e0, e1, e2):
        i = pl.program_id(1)

        @pl.when(i == 0)
        def _():
            accb[...] = jnp.zeros_like(accb)

        has_prev, has_next = _ffn_flags(i, ncr, nT)
        bufs = (e0, e1, e2)

        def run(is_ctx):
            _fill_ext(bufs, g, gp, gn, has_prev, has_next, is_ctx)
            for r0 in range(0, R, SUB):
                for c0 in range(0, cb, LANES):
                    gc = _conv9(bufs, w, r0, c0, is_ctx, False) + bb[:, pl.ds(c0, LANES)]
                    sg = _sig(gc)
                    val = v[r0:r0 + SUB, c0:c0 + LANES].astype(F32)
                    dav = da[r0:r0 + SUB, c0:c0 + LANES].astype(F32)
                    dup[r0:r0 + SUB, c0:c0 + LANES] = (dav * gc * sg).astype(BF16)
                    d = dav * val * (sg * (1.0 + gc * (1.0 - sg)))
                    dgc[r0:r0 + SUB, c0:c0 + LANES] = d.astype(BF16)
                    accb[0:1, pl.ds(c0, LANES)] += jnp.sum(d, axis=0, keepdims=True)

        pl.when(i < ncr)(lambda: run(True))
        pl.when(i >= ncr)(lambda: run(False))

    return pl.pallas_call(
        body, grid=(ncb, nT), in_specs=[cur(0), cur(ncb), hp(0), hn(0), cur(0), vec(16), vec(1)],
        out_specs=[cur(0), cur(ncb), vec(8)],
        out_shape=[jax.ShapeDtypeStruct((T, CF), BF16), jax.ShapeDtypeStruct((T, 2 * CF), BF16),
                   jax.ShapeDtypeStruct((8, CF), F32)],
        scratch_shapes=_ffn_scratch(cb, 1), compiler_params=_cp(2), name=name)(up, up, up, up, dact, w16, b)


def ffn_act_bwd2(dgc, up, dup, w16, CF, ncr, name):
    T = up.shape[0]
    nT, cb = T // R, 2 * LANES
    ncb = CF // cb
    cur, hp, hn, vec = _ffn_specs(T, cb, 'ji')

    def body(d, dp_, dn_, g, gp, gn, w, dup_in, dgate, accw, d0, d1, d2, g0, g1, g2):
        i = pl.program_id(1)

        @pl.when(i == 0)
        def _():
            accw[...] = jnp.zeros_like(accw)

        has_prev, has_next = _ffn_flags(i, ncr, nT)
        dbufs, gbufs = (d0, d1, d2), (g0, g1, g2)

        def run(is_ctx):
            _fill_ext(dbufs, d, dp_, dn_, has_prev, has_next, is_ctx)
            _fill_ext(gbufs, g, gp, gn, has_prev, has_next, is_ctx)
            for r0 in range(0, R, SUB):
                for c0 in range(0, cb, LANES):
                    dgate[r0:r0 + SUB, c0:c0 + LANES] = _conv9(dbufs, w, r0, c0, is_ctx, True).astype(BF16)
                    dcur = _tap_src(dbufs, r0, c0, 0, 0)
                    for dr, dc in _taps(is_ctx):
                        widx = (dr + 1) * 3 + dc + 1
                        accw[widx:widx + 1, pl.ds(c0, LANES)] += jnp.sum(dcur * _tap_src(gbufs, r0, c0, dr, dc),
                                                                         axis=0, keepdims=True)

        pl.when(i < ncr)(lambda: run(True))
        pl.when(i >= ncr)(lambda: run(False))

    return pl.pallas_call(
        body, grid=(ncb, nT),
        in_specs=[cur(0), hp(0), hn(0), cur(0), hp(0), hn(0), vec(16), pl.BlockSpec(memory_space=pl.ANY)],
        out_specs=[cur(0), vec(16)],
        out_shape=[jax.ShapeDtypeStruct((T, 2 * CF), BF16), jax.ShapeDtypeStruct((16, CF), F32)],
        input_output_aliases={7: 0}, scratch_shapes=_ffn_scratch(cb, 2), compiler_params=_cp(2), name=name)(
            dgc, dgc, dgc, up, up, up, w16, dup)


def mod_fwd(cs, w_mod, name):
    L, D, Ns = w_mod.shape
    tn = _lane_tile(Ns, 768)

    def body(c_ref, w_ref, o_ref):
        cv = c_ref[...]
        o_ref[...] = _dot((cv * _sig(cv)).astype(BF16), w_ref[...].astype(BF16))

    return pl.pallas_call(
        body, grid=(L, Ns // tn),
        in_specs=[pl.BlockSpec((16, D), lambda l, j: (0, 0)), pl.BlockSpec((None, D, tn), lambda l, j: (l, 0, j))],
        out_specs=pl.BlockSpec((None, 16, tn), lambda l, j: (l, 0, j)),
        out_shape=jax.ShapeDtypeStruct((L, 16, Ns), F32), compiler_params=_cp(2), name=name)(cs, w_mod)


def mod_bwd(cs, w_mod, dmod, name):
    L, D, Ns = w_mod.shape
    tn = _lane_tile(Ns, 768)

    def body(c_ref, w_ref, dm_ref, gw_ref, ds_ref):
        @pl.when(jnp.logical_and(pl.program_id(0) == 0, pl.program_id(1) == 0))
        def _():
            ds_ref[...] = jnp.zeros_like(ds_ref)

        cv = c_ref[...]
        dm = dm_ref[...].astype(BF16)
        gw_ref[...] = _dot_tn((cv * _sig(cv)).astype(BF16), dm)
        ds_ref[...] += _dot_nt(dm, w_ref[...].astype(BF16))

    return pl.pallas_call(
        body, grid=(L, Ns // tn),
        in_specs=[pl.BlockSpec((16, D), lambda l, j: (0, 0)), pl.BlockSpec((None, D, tn), lambda l, j: (l, 0, j)),
                  pl.BlockSpec((None, 16, tn), lambda l, j: (l, 0, j))],
        out_specs=[pl.BlockSpec((None, D, tn), lambda l, j: (l, 0, j)), pl.BlockSpec((16, D), lambda l, j: (0, 0))],
        out_shape=[jax.ShapeDtypeStruct((L, D, Ns), F32), jax.ShapeDtypeStruct((16, D), F32)],
        compiler_params=_cp(2), name=name)(cs, w_mod, dmod)


def cast_bf16(w, name):
    L, Kb, Nb = w.shape
    w2 = w.reshape(L * Kb, Nb)
    tr = _row_tile(L * Kb, Nb, 1 << 19)

    def body(w_ref, o_ref):
        o_ref[...] = w_ref[...].astype(BF16)

    spec = pl.BlockSpec((tr, Nb), lambda i: (i, 0))
    out = pl.pallas_call(body, grid=(L * Kb // tr,), in_specs=[spec], out_specs=spec,
                         out_shape=jax.ShapeDtypeStruct((L * Kb, Nb), BF16), compiler_params=_cp(1), name=name)(w2)
    return out.reshape(L, Kb, Nb)


def add_half(dw, recv, c_idx, name):
    S, Kb, Nb = dw.shape
    Kh = Kb // 2
    tr = _row_tile(Kh, Nb, 1 << 19)
    nb = Kh // tr

    def body(c_ref, a_ref, b_ref, o_ref):
        o_ref[...] = (a_ref[...].astype(F32) + b_ref[...].astype(F32)).astype(BF16)

    return pl.pallas_call(
        body,
        grid_spec=pltpu.PrefetchScalarGridSpec(
            num_scalar_prefetch=1, grid=(S, nb),
            in_specs=[pl.BlockSpec((None, tr, Nb), lambda s, i, c: (s, c[0] * nb + i, 0)),
                      pl.BlockSpec((None, tr, Nb), lambda s, i, c: (s, i, 0))],
            out_specs=pl.BlockSpec((None, tr, Nb), lambda s, i, c: (s, i, 0))),
        out_shape=jax.ShapeDtypeStruct((S, Kh, Nb), BF16), compiler_params=_cp(2), name=name)(c_idx, dw, recv)


def add_shards(parts, name):
    S, Kh, Nb = parts.shape
    tr = _row_tile(Kh, Nb, 1 << 18)

    def body(p_ref, o_ref):
        acc = p_ref[0].astype(F32)
        for s in range(1, S):
            acc = acc + p_ref[s].astype(F32)
        o_ref[...] = acc

    return pl.pallas_call(
        body, grid=(Kh // tr,), in_specs=[pl.BlockSpec((S, tr, Nb), lambda i: (0, i, 0))],
        out_specs=pl.BlockSpec((tr, Nb), lambda i: (i, 0)),
        out_shape=jax.ShapeDtypeStruct((Kh, Nb), F32), compiler_params=_cp(1), name=name)(parts)


def adamw(w, g, m, v, name):
    shape = w.shape
    cols = shape[-1]
    rows = w.size // cols
    w2, g2, m2, v2 = (t.reshape(rows, cols) for t in (w, g, m, v))
    tr = _row_tile(rows, cols, 3 << 17)

    def body(w_ref, g_ref, m_ref, v_ref, d_ref, nm_ref, nv_ref):
        gv = g_ref[...]
        nm = ADAM_B1 * m_ref[...] + (1.0 - ADAM_B1) * gv
        nv = ADAM_B2 * v_ref[...] + (1.0 - ADAM_B2) * jnp.square(gv)
        m_hat = nm / (1.0 - ADAM_B1 ** ADAM_STEP)
        v_hat = nv / (1.0 - ADAM_B2 ** ADAM_STEP)
        d_ref[...] = -ADAM_LR * (m_hat / (jnp.sqrt(v_hat) + ADAM_EPS) + ADAM_WD * w_ref[...])
        nm_ref[...] = nm
        nv_ref[...] = nv

    spec = pl.BlockSpec((tr, cols), lambda i: (i, 0))
    sds = jax.ShapeDtypeStruct((rows, cols), F32)
    d, nm, nv = pl.pallas_call(body, grid=(rows // tr,), in_specs=[spec] * 4, out_specs=[spec] * 3,
                               out_shape=[sds, sds, sds], compiler_params=_cp(1), name=name)(w2, g2, m2, v2)
    return d.reshape(shape), nm.reshape(shape), nv.reshape(shape)


def gather_sum(v, name):
    r, cols = v.shape

    def body(v_ref, g_ref, s_ref, send_sems, recv_sems, local_sem):
        x, y, c = _place()
        me = 4 * x + 2 * y + c
        mine = pltpu.make_async_copy(v_ref, g_ref.at[me], local_sem)
        mine.start()
        sends, peers = [], []
        for k in range(1, 8):
            px = 1 - x if k & 4 else x
            py = 1 - y if k & 2 else y
            pc = 1 - c if k & 1 else c
            cp = pltpu.make_async_remote_copy(src_ref=v_ref, dst_ref=g_ref.at[me], send_sem=send_sems.at[k - 1],
                                              recv_sem=recv_sems.at[k - 1], device_id=(px, py, pc), device_id_type=MESH)
            cp.start()
            sends.append(cp)
            peers.append((px, py, pc))
        for k, (px, py, pc) in enumerate(peers):
            pltpu.make_async_remote_copy(src_ref=v_ref, dst_ref=g_ref.at[4 * px + 2 * py + pc], send_sem=send_sems.at[k],
                                         recv_sem=recv_sems.at[k], device_id=(px, py, pc), device_id_type=MESH).wait_recv()
        for cp in sends:
            cp.wait_send()
        mine.wait()
        acc = g_ref[0]
        for d in range(1, 8):
            acc = acc + g_ref[d]
        s_ref[...] = acc

    vm = pl.BlockSpec(memory_space=pltpu.VMEM)
    return pl.pallas_call(
        body, in_specs=[vm], out_specs=[vm, vm],
        out_shape=[jax.ShapeDtypeStruct((8, r, cols), F32), jax.ShapeDtypeStruct((r, cols), F32)],
        scratch_shapes=[pltpu.SemaphoreType.DMA((7,)), pltpu.SemaphoreType.DMA((7,)), pltpu.SemaphoreType.DMA],
        compiler_params=_cp0(), name=name)(v)


def gather_weights(wb, name):
    L, Kb, Nb = wb.shape
    Kh = Kb // 2

    def body(w_ref, *rest):
        outs, (send_sems, recv_sems, local_sems) = rest[:L], rest[L:]
        x, y, c = _place()
        jm = 2 * x + y
        sib = (x, y, 1 - c)
        chips = _other_chips(x, y)

        def cp(l, t, src, dst, to):
            return pltpu.make_async_remote_copy(src_ref=src, dst_ref=dst, send_sem=send_sems.at[7 * l + t],
                                                recv_sem=recv_sems.at[7 * l + t], device_id=to, device_id_type=MESH)

        started, local = [], []
        for l in range(L):
            src = w_ref.at[l, pl.ds(c * Kh, Kh), :]
            dst = outs[l].at[jm, c]
            lc = pltpu.make_async_copy(src, dst, local_sems.at[l])
            lc.start()
            local.append(lc)
            for t, (px, py) in enumerate(chips):
                started.append(cp(l, t, src, dst, (px, py, c)))
            started.append(cp(l, 3, src, dst, sib))
            for s in started[-4:]:
                s.start()
        for l in range(L):
            for t, (px, py) in enumerate(chips):
                blk = outs[l].at[2 * px + py, c]
                cp(l, t, blk, blk, (px, py, c)).wait_recv()
                fwd = cp(l, 4 + t, blk, blk, sib)
                fwd.start()
                started.append(fwd)
        for l in range(L):
            blk = outs[l].at[jm, 1 - c]
            cp(l, 3, blk, blk, sib).wait_recv()
            for t, (px, py) in enumerate(chips):
                blk = outs[l].at[2 * px + py, 1 - c]
                cp(l, 4 + t, blk, blk, sib).wait_recv()
        for s in started:
            s.wait_send()
        for lc in local:
            lc.wait()

    hbm = pl.BlockSpec(memory_space=pl.ANY)
    outs = pl.pallas_call(
        body, in_specs=[hbm], out_specs=[hbm] * L,
        out_shape=[jax.ShapeDtypeStruct((4, 2, Kh, Nb), BF16)] * L,
        scratch_shapes=[pltpu.SemaphoreType.DMA((7 * L,)), pltpu.SemaphoreType.DMA((7 * L,)), pltpu.SemaphoreType.DMA((L,))],
        compiler_params=_cp0(), name=name)(wb)
    return [o.reshape(4, Kb, Nb) for o in outs]


def swap_halves(dw, name):
    S, Kb, Nb = dw.shape
    Kh = Kb // 2

    def body(d_ref, o_ref, send_sems, recv_sems):
        x, y, c = _place()
        sib = (x, y, 1 - c)
        cps = [pltpu.make_async_remote_copy(src_ref=d_ref.at[s, pl.ds((1 - c) * Kh, Kh), :], dst_ref=o_ref.at[s],
                                            send_sem=send_sems.at[s], recv_sem=recv_sems.at[s], device_id=sib,
                                            device_id_type=MESH) for s in range(S)]
        for cpy in cps:
            cpy.start()
        for cpy in cps:
            cpy.wait_recv()
        for cpy in cps:
            cpy.wait_send()

    hbm = pl.BlockSpec(memory_space=pl.ANY)
    return pl.pallas_call(
        body, in_specs=[hbm], out_specs=hbm, out_shape=jax.ShapeDtypeStruct((S, Kh, Nb), dw.dtype),
        scratch_shapes=[pltpu.SemaphoreType.DMA((S,)), pltpu.SemaphoreType.DMA((S,))],
        compiler_params=_cp0(), name=name)(dw)


def join_halves(gh, name):
    Kh, Nb = gh.shape

    def body(g_ref, o_ref, send_sem, recv_sem, local_sem):
        x, y, c = _place()
        sib = (x, y, 1 - c)
        mine = pltpu.make_async_copy(g_ref, o_ref.at[c], local_sem)
        mine.start()
        cpy = pltpu.make_async_remote_copy(src_ref=g_ref, dst_ref=o_ref.at[c], send_sem=send_sem, recv_sem=recv_sem,
                                           device_id=sib, device_id_type=MESH)
        cpy.start()
        pltpu.make_async_remote_copy(src_ref=g_ref, dst_ref=o_ref.at[1 - c], send_sem=send_sem, recv_sem=recv_sem,
                                     device_id=sib, device_id_type=MESH).wait_recv()
        cpy.wait_send()
        mine.wait()

    hbm = pl.BlockSpec(memory_space=pl.ANY)
    return pl.pallas_call(
        body, in_specs=[hbm], out_specs=hbm, out_shape=jax.ShapeDtypeStruct((2, Kh, Nb), F32),
        scratch_shapes=[pltpu.SemaphoreType.DMA, pltpu.SemaphoreType.DMA, pltpu.SemaphoreType.DMA],
        compiler_params=_cp0(), name=name)(gh).reshape(2 * Kh, Nb)


def chip_sums(dw, c_idx, tag):
    return add_half(dw, swap_halves(dw, f"rs_swap_{tag}"), c_idx, f"rs_add_half_{tag}")


def finish_grad(parts, tag):
    return join_halves(add_shards(parts, f"rs_add_shards_{tag}"), f"rs_join_{tag}")


def _pack(parts):
    flat = jnp.concatenate([t.reshape(-1).astype(F32) for t in parts])
    n = flat.shape[0]
    pad = (-n) % 1024
    return jnp.pad(flat, (0, pad)).reshape(-1, 128)


def _unpack(buf, shapes):
    flat = buf.reshape(buf.shape[:-2] + (-1,))
    out, o = [], 0
    for s in shapes:
        n = 1
        for d in s:
            n *= d
        out.append(flat[..., o:o + n].reshape(buf.shape[:-2] + tuple(s)))
        o += n
    return out


def _rope_tables(seq, ctx_len):
    t = jnp.arange(seq)
    inv = 1.0 / (ROPE_THETA ** (jnp.arange(0, DH // 4, dtype=F32) / (DH // 4)))
    ar = (t // GRID_W).astype(F32)[:, None] * inv[None, :]
    ac = (t % GRID_W).astype(F32)[:, None] * inv[None, :]
    cos = jnp.concatenate([jnp.cos(ar), jnp.cos(ar), jnp.cos(ac), jnp.cos(ac)], axis=-1)
    sin = jnp.concatenate([-jnp.sin(ar), jnp.sin(ar), -jnp.sin(ac), jnp.sin(ac)], axis=-1)
    return (jnp.concatenate([jnp.ones((ctx_len, DH), F32), cos], axis=0),
            jnp.concatenate([jnp.zeros((ctx_len, DH), F32), sin], axis=0))


def kernel(x, c, ctx, c_ctx, w_mod, b_mod, norm1_g, norm2_g, w_in, ret_decay_f, ret_decay_b, conv_dw_w, conv_dw_b, conv_ln_g, conv_ln_b, w_out, ffn_w_up, ffn_dw_w, ffn_dw_b, ffn_w_down, final_norm_g, loss_target, m_c_ctx, m_w_mod, m_b_mod, m_norm1_g, m_norm2_g, m_w_in, m_ret_decay_f, m_ret_decay_b, m_conv_dw_w, m_conv_dw_b, m_conv_ln_g, m_conv_ln_b, m_w_out, m_ffn_w_up, m_ffn_dw_w, m_ffn_dw_b, m_ffn_w_down, m_final_norm_g, v_c_ctx, v_w_mod, v_b_mod, v_norm1_g, v_norm2_g, v_w_in, v_ret_decay_f, v_ret_decay_b, v_conv_dw_w, v_conv_dw_b, v_conv_ln_g, v_conv_ln_b, v_w_out, v_ffn_w_up, v_ffn_dw_w, v_ffn_dw_b, v_ffn_w_down, v_final_norm_g):
    _, SEQ, D = x.shape
    CTX = ctx.shape[1]
    L = w_in.shape[0]
    CWs = conv_dw_w.shape[2]
    CW = 4 * CWs
    RW = 4 * w_out.shape[1] - CW
    H = RW // DH
    CFs = ffn_dw_w.shape[-1]
    CF = 4 * CFs
    NMs = w_mod.shape[2]
    T = CTX + SEQ
    ncr, ncc = CTX // R, CTX // RC
    assert CTX == R and RW == CW and RW % DH == 0 and SEQ % R == 0 and R % GRID_W == 0
    assert w_in.shape[2] * 4 == 4 * RW + 2 * CW and NMs * 4 == N_MOD * D

    mx, my, mc = _place()
    me = 4 * mx + 2 * my + mc
    jm = 2 * mx + my
    c_idx = jnp.reshape(mc, (1,)).astype(jnp.int32)

    shapes0 = [(D,), (L, CONV_K, CWs), (L, 9, CFs)]
    g0, _ = gather_sum(_pack([c[0], conv_dw_w, ffn_dw_w.reshape(L, 9, CFs)]), "gather_cond")
    c_all, cw_all, fw_all = _unpack(g0, shapes0)
    conv_w = jnp.concatenate([cw_all[2 * j] for j in range(4)], axis=-1)
    ffn_w = jnp.concatenate([fw_all[2 * j] for j in range(4)], axis=-1)
    conv_w32 = jnp.pad(conv_w, ((0, 0), (0, 32 - CONV_K), (0, 0)))
    ffn_w16 = jnp.pad(ffn_w, ((0, 0), (0, 7), (0, 0)))
    cs = jnp.concatenate([c_all, c_ctx[None, :], jnp.zeros((7, D), F32)], axis=0)
    mod_shard = mod_fwd(cs, w_mod, "mod_fwd")
    g1, _ = gather_sum(mod_shard.reshape(-1, 128), "gather_mod")
    mod_all = g1.reshape(8, L, 16, NMs)
    mod_full = jnp.concatenate([mod_all[2 * j] for j in range(4)], axis=-1) + b_mod[:, None, :]
    mod_mine = lax.dynamic_index_in_dim(mod_full, me, axis=1, keepdims=False)
    modv = jnp.stack([mod_full[:, 8], mod_mine], axis=1).reshape(L, 2, N_MOD, D)

    big = {"w_in": (w_in, 'n'), "w_out": (w_out, 'k'), "w_up": (ffn_w_up, 'n'), "w_down": (ffn_w_down, 'k')}
    wb = {k: cast_bf16(w, f"cast_{k}") for k, (w, _) in big.items()}
    wg = {k: [gather_weights(wb[k][0:1], f"gather_{k}")[0]] for k in big}
    nxt = lambda k, l: wb[k][l + 1] if l + 1 < L else None

    cosT, sinT = _rope_tables(SEQ, CTX)
    lgt = [jnp.broadcast_to(jnp.concatenate([jax.nn.log_sigmoid(ret_decay_f[l]), jax.nn.log_sigmoid(ret_decay_b[l])])[:, None],
                            (2 * H, DH)) for l in range(L)]
    row = lambda t: t.reshape(1, -1)

    def project(a, k, l, out_dtype, name, res=None):
        y, xn, gathered = mm_nn(a, wg[k][l], big[k][1], out_dtype, name, nxt(k, l), res)
        if gathered is not None:
            wg[k].append(gathered)
        return y, xn

    xs = jnp.concatenate([ctx[0], x[0]], axis=0)
    saved = []
    for l in range(L):
        h1 = norm_mod(xs, row(norm1_g[l]), modv[l], 0, 1, ncr, "norm_mod")
        p, _ = project(h1, "w_in", l, BF16, "mm_in")
        uc = conv31_fwd(p, conv_w32[l], row(conv_dw_b[l]), RW, CW, ncr, "conv31_fwd")
        o_f, o_b, stf, stb = ret_fwd(p, cosT, sinT, lgt[l], H, ncc, "ret_fwd")
        mix = mix_fwd(o_f, o_b, p, uc, row(conv_ln_g[l]), row(conv_ln_b[l]), H, "mix_fwd")
        y1, x2 = project(mix, "w_out", l, BF16, "mm_out", (xs, modv[l][:, 2], CTX))
        h2 = norm_mod(x2, row(norm2_g[l]), modv[l], 3, 4, ncr, "norm_mod")
        up, _ = project(h2, "w_up", l, BF16, "mm_up")
        act = ffn_act(up, ffn_w16[l], row(ffn_dw_b[l]), CF, ncr, "ffn_act")
        y2, x3 = project(act, "w_down", l, BF16, "mm_down", (x2, modv[l][:, 5], CTX))
        saved.append((xs, h1, p, uc, o_f, o_b, stf, stb, mix, y1, x2, h2, up, act, y2))
        xs = x3

    dx, acc_loss = loss_head(xs, loss_target[0], row(final_norm_g), ncr, "loss_head")
    loss = lax.psum(0.5 / D * jnp.sum(acc_loss[1]), ("x", "y", "c"))

    small, gbig = [None] * L, {k: [None] * L for k in big}
    for l in reversed(range(L)):
        x1, h1, p, uc, o_f, o_b, stf, stb, mix, y1, x2, h2, up, act, y2 = saved[l]
        dy2, ag2 = gate_res_bwd(dx, y2, modv[l], 5, ncr, "gate_res_bwd")
        cs_down = chip_sums(mm_tn(act, dy2, 4, 'k', "mm_down_dw"), c_idx, "w_down")
        dact, parts = mm_nt(dy2, wg["w_down"][l], 'k', BF16, "mm_down_dx", cs_down)
        gbig["w_down"][l] = finish_grad(parts, "w_down")
        dgc, dup, accb = ffn_act_bwd1(up, dact, ffn_w16[l], row(ffn_dw_b[l]), CF, ncr, "ffn_act_bwd1")
        dup, accfw = ffn_act_bwd2(dgc, up, dup, ffn_w16[l], CF, ncr, "ffn_act_bwd2")
        cs_up = chip_sums(mm_tn(h2, dup, 4, 'n', "mm_up_dw"), c_idx, "w_up")
        dh2, parts = mm_nt(dup, wg["w_up"][l], 'n', F32, "mm_up_dx", cs_up)
        gbig["w_up"][l] = finish_grad(parts, "w_up")
        dx2, an2 = norm_mod_bwd(x2, row(norm2_g[l]), modv[l], dh2, dx, 4, ncr, "norm_mod_bwd")
        dy1, ag1 = gate_res_bwd(dx2, y1, modv[l], 2, ncr, "gate_res_bwd")
        cs_out = chip_sums(mm_tn(mix, dy1, 4, 'k', "mm_out_dw"), c_idx, "w_out")
        dmix, parts = mm_nt(dy1, wg["w_out"][l], 'k', BF16, "mm_out_dx", cs_out)
        gbig["w_out"][l] = finish_grad(parts, "w_out")
        do, dg, duc, accln = mix_bwd(dmix, o_f, o_b, p, uc, row(conv_ln_g[l]), row(conv_ln_b[l]), H, "mix_bwd")
        da, dbg, acccw = conv31_bwd(duc, p, conv_w32[l], RW, CW, ncr, "conv31_bwd")
        dqf, dkf, dvf, dqb, dkb, dvb, glg = ret_bwd(p, do, cosT, sinT, lgt[l], stf, stb, H, ncc, "ret_bwd")
        dp = assemble_dp(dqf, dqb, dkf, dkb, dvf, dvb, dg, da, dbg, "assemble_dp")
        cs_in = chip_sums(mm_tn(h1, dp, 4, 'n', "mm_in_dw"), c_idx, "w_in")
        dh1, parts = mm_nt(dp, wg["w_in"][l], 'n', F32, "mm_in_dx", cs_in)
        gbig["w_in"][l] = finish_grad(parts, "w_in")
        dx, an1 = norm_mod_bwd(x1, row(norm1_g[l]), modv[l], dh1, dx2, 1, ncr, "norm_mod_bwd")
        dmod = jnp.stack([jnp.stack([an1[0], an1[1], ag1[0], an2[0], an2[1], ag2[0]]),
                          jnp.stack([an1[2], an1[3], ag1[1], an2[2], an2[3], ag2[1]])])
        dlg = jnp.sum(glg, axis=(1, 2))
        dth = dlg * jnp.concatenate([jax.nn.sigmoid(-ret_decay_f[l]), jax.nn.sigmoid(-ret_decay_b[l])])
        small[l] = [dmod, an1[4], an2[4], acccw[31], accln[0], accln[1], acccw[:CONV_K], accfw[:9], accb[0], dth]
    grad_x = dx[CTX:][None]

    shapes1 = [(2, N_MOD, D), (D,), (D,), (CW,), (CW,), (CW,), (CONV_K, CW), (9, CF), (CF,), (2 * H,)]
    flat_parts = [t for l in range(L) for t in small[l]] + [acc_loss[0]]
    g2, s2 = gather_sum(_pack(flat_parts), "gather_small_grads")
    sums = _unpack(s2, shapes1 * L + [(D,)])
    per_dev = _unpack(g2, shapes1 * L + [(D,)])
    nS = len(shapes1)
    col = lambda i: jnp.stack([sums[l * nS + i] for l in range(L)])
    dmod_sum = col(0)
    dmod_dev = jnp.stack([per_dev[l * nS] for l in range(L)], axis=0)
    g_b_mod = (dmod_sum[:, 0] + dmod_sum[:, 1]).reshape(L, N_MOD * D)
    g_norm1, g_norm2 = col(1), col(2)
    g_conv_b, g_ln_g, g_ln_b = col(3), col(4), col(5)
    g_conv_w = lax.dynamic_slice_in_dim(col(6), jm * CWs, CWs, axis=2)
    g_ffn_w = lax.dynamic_slice_in_dim(col(7), jm * CFs, CFs, axis=2).reshape(L, 3, 3, CFs)
    g_ffn_b = col(8)
    g_ret = col(9)
    g_final = sums[-1]

    dmod_rows = jnp.concatenate([dmod_dev[:, :, 1].reshape(L, 8, N_MOD * D), dmod_sum[:, 0].reshape(L, 1, N_MOD * D),
                                 jnp.zeros((L, 7, N_MOD * D), F32)], axis=1)
    dmod_shard = lax.dynamic_slice_in_dim(dmod_rows, jm * NMs, NMs, axis=2)
    g_w_mod, ds_part = mod_bwd(cs, w_mod, dmod_shard, "mod_bwd")
    _, ds_sum = gather_sum(ds_part.reshape(-1, 128), "gather_dsilu")
    ds_ctx = 0.5 * ds_sum.reshape(16, D)[8]
    sg = jax.nn.sigmoid(c_ctx)
    g_c_ctx = ds_ctx * (sg * (1.0 + c_ctx * (1.0 - sg)))

    grads = {
        "c_ctx": g_c_ctx, "w_mod": g_w_mod, "b_mod": g_b_mod, "norm1_g": g_norm1, "norm2_g": g_norm2,
        "w_in": jnp.stack(gbig["w_in"]), "ret_decay_f": g_ret[:, :H], "ret_decay_b": g_ret[:, H:],
        "conv_dw_w": g_conv_w, "conv_dw_b": g_conv_b, "conv_ln_g": g_ln_g, "conv_ln_b": g_ln_b,
        "w_out": jnp.stack(gbig["w_out"]), "ffn_w_up": jnp.stack(gbig["w_up"]), "ffn_dw_w": g_ffn_w,
        "ffn_dw_b": g_ffn_b, "ffn_w_down": jnp.stack(gbig["w_down"]), "final_norm_g": g_final,
    }
    params = {
        "c_ctx": (c_ctx, m_c_ctx, v_c_ctx), "w_mod": (w_mod, m_w_mod, v_w_mod), "b_mod": (b_mod, m_b_mod, v_b_mod),
        "norm1_g": (norm1_g, m_norm1_g, v_norm1_g), "norm2_g": (norm2_g, m_norm2_g, v_norm2_g),
        "w_in": (w_in, m_w_in, v_w_in), "ret_decay_f": (ret_decay_f, m_ret_decay_f, v_ret_decay_f),
        "ret_decay_b": (ret_decay_b, m_ret_decay_b, v_ret_decay_b),
        "conv_dw_w": (conv_dw_w, m_conv_dw_w, v_conv_dw_w), "conv_dw_b": (conv_dw_b, m_conv_dw_b, v_conv_dw_b),
        "conv_ln_g": (conv_ln_g, m_conv_ln_g, v_conv_ln_g), "conv_ln_b": (conv_ln_b, m_conv_ln_b, v_conv_ln_b),
        "w_out": (w_out, m_w_out, v_w_out), "ffn_w_up": (ffn_w_up, m_ffn_w_up, v_ffn_w_up),
        "ffn_dw_w": (ffn_dw_w, m_ffn_dw_w, v_ffn_dw_w), "ffn_dw_b": (ffn_dw_b, m_ffn_dw_b, v_ffn_dw_b),
        "ffn_w_down": (ffn_w_down, m_ffn_w_down, v_ffn_w_down),
        "final_norm_g": (final_norm_g, m_final_norm_g, v_final_norm_g),
    }
    names = list(params)
    upd = {n: adamw(params[n][0], grads[n], params[n][1], params[n][2], f"adamw_{n}") for n in names}
    return (loss, grad_x, *[grads[n] for n in names], *[upd[n][0] for n in names],
            *[upd[n][1] for n in names], *[upd[n][2] for n in names])
```

```python
import jax
import jax.numpy as jnp
from jax import lax
from jax.experimental import pallas as pl
from jax.experimental.pallas import tpu as pltpu

F32 = jnp.float32
BF16 = jnp.bfloat16
EPS = 1e-6
DH = 128
RC = 128
GRID_W = 64
ROPE_THETA = 10000.0
N_MOD = 6
R = 256
HALO = 16
CONV_K = 31
SUB = 64
LANES = 256
VMEM_LIMIT = 48 * 1024 * 1024
MESH = pl.DeviceIdType.MESH
ADAM_LR, ADAM_B1, ADAM_B2, ADAM_EPS, ADAM_WD, ADAM_STEP = 0.001, 0.9, 0.999, 1e-08, 0.01, 10


def _cp(n):
    return pltpu.CompilerParams(dimension_semantics=("arbitrary",) * n, vmem_limit_bytes=VMEM_LIMIT)


def _cp0():
    return pltpu.CompilerParams(vmem_limit_bytes=VMEM_LIMIT)


def _sig(v):
    return 1.0 / (1.0 + jnp.exp(-v))


def _dot(a, b):
    return jnp.dot(a, b, preferred_element_type=F32)


def _dot_nt(a, b):
    return lax.dot_general(a, b, (((1,), (1,)), ((), ())), preferred_element_type=F32)


def _dot_tn(a, b):
    return lax.dot_general(a, b, (((0,), (0,)), ((), ())), preferred_element_type=F32)


def _lane_tile(n, cap):
    t = (min(n, cap) // 128) * 128
    while t >= 128:
        if n % t == 0:
            return t
        t -= 128
    raise ValueError(f"no lane tile for {n}")


def _row_tile(rows, cols, max_elems):
    if rows * cols <= max_elems:
        return rows
    t = (min(rows, max(8, max_elems // cols)) // 8) * 8
    while t >= 8:
        if rows % t == 0:
            return t
        t -= 8
    raise ValueError(f"no row tile for {rows}x{cols}")


def _place():
    return lax.axis_index("x"), lax.axis_index("y"), lax.axis_index("c")


def _other_chips(x, y):
    return [(1 - x, y), (x, 1 - y), (1 - x, 1 - y)]


def _first_last(grid):
    ids = [pl.program_id(k) for k in range(len(grid))]
    first, last = ids[0] == 0, ids[0] == grid[0] - 1
    for k in range(1, len(grid)):
        first = jnp.logical_and(first, ids[k] == 0)
        last = jnp.logical_and(last, ids[k] == grid[k] - 1)
    return first, last


def _side_copies(src_slab, dst_ref, send_sems, recv_sems, local_sem):
    x, y, c = _place()
    jm = 2 * x + y
    chips = _other_chips(x, y)
    mine = pltpu.make_async_copy(src_slab(jm), dst_ref.at[jm], local_sem)
    sends = [pltpu.make_async_remote_copy(src_ref=src_slab(2 * px + py), dst_ref=dst_ref.at[jm], send_sem=send_sems.at[t],
                                          recv_sem=recv_sems.at[t], device_id=(px, py, c), device_id_type=MESH)
             for t, (px, py) in enumerate(chips)]
    recvs = [pltpu.make_async_remote_copy(src_ref=dst_ref.at[2 * px + py], dst_ref=dst_ref.at[2 * px + py],
                                          send_sem=send_sems.at[t], recv_sem=recv_sems.at[t], device_id=(px, py, c),
                                          device_id_type=MESH) for t, (px, py) in enumerate(chips)]
    return mine, sends, recvs


def _side_start(mine, sends, recvs):
    mine.start()
    for s in sends:
        s.start()


def _side_finish(mine, sends, recvs):
    for r in recvs:
        r.wait_recv()
    for s in sends:
        s.wait_send()
    mine.wait()


_SIDE_SEMS = [pltpu.SemaphoreType.DMA((3,)), pltpu.SemaphoreType.DMA((3,)), pltpu.SemaphoreType.DMA]


def mm_nn(a, b, out_dtype, name, bcast=None, res=None):
    M, K = a.shape
    S, Kb, Nb = b.shape
    N = S * Nb
    assert K == Kb
    tm, tn, tk = _lane_tile(M, 768), _lane_tile(Nb, 1536), _lane_tile(K, 2048)
    nk = K // tk
    grid = (M // tm, N // tn, nk)
    tps = Nb // tn
    b_map = lambda i, j, k: (j // tps, k, j % tps)
    n_in = 2 + (2 if res is not None else 0) + (1 if bcast is not None else 0)
    n_out = 1 + (1 if res is not None else 0) + (1 if bcast is not None else 0)

    def body(*refs):
        ins, outs, scr = list(refs[:n_in]), list(refs[n_in:n_in + n_out]), list(refs[n_in + n_out:])
        a_ref, b_ref = ins[0], ins[1]
        o_ref = outs[0]
        if nk > 1:
            acc_ref = scr.pop(0)
        if bcast is not None:
            first, last = _first_last(grid)
            side = _side_copies(lambda j: ins[-1], outs[-1], *scr)
            pl.when(first)(lambda: _side_start(*side))
        k = pl.program_id(2)

        def finish(acc):
            o_ref[...] = acc.astype(o_ref.dtype)
            if res is not None:
                x_ref, g_ref = ins[2], ins[3]
                rows = pl.program_id(0) * tm + lax.broadcasted_iota(jnp.int32, (tm, 1), 0)
                gate = jnp.where(rows < res[2], g_ref[0:1, :], g_ref[1:2, :])
                outs[1][...] = x_ref[...] + gate * acc

        if nk == 1:
            finish(_dot(a_ref[...], b_ref[...]))
        else:
            @pl.when(k == 0)
            def _():
                acc_ref[...] = jnp.zeros_like(acc_ref)

            acc_ref[...] += _dot(a_ref[...], b_ref[...])
            pl.when(k == nk - 1)(lambda: finish(acc_ref[...]))

        if bcast is not None:
            pl.when(last)(lambda: _side_finish(*side))

    hbm = pl.BlockSpec(memory_space=pl.ANY)
    tile = pl.BlockSpec((tm, tn), lambda i, j, k: (i, j))
    in_specs = [pl.BlockSpec((tm, tk), lambda i, j, k: (i, k)), pl.BlockSpec((None, tk, tn), b_map)]
    out_specs = [tile]
    out_shape = [jax.ShapeDtypeStruct((M, N), out_dtype)]
    scratch = [pltpu.VMEM((tm, tn), F32)] if nk > 1 else []
    args = [a, b]
    if res is not None:
        in_specs += [tile, pl.BlockSpec((2, tn), lambda i, j, k: (0, j))]
        out_specs.append(tile)
        out_shape.append(jax.ShapeDtypeStruct((M, N), F32))
        args += [res[0], res[1]]
    if bcast is not None:
        in_specs.append(hbm)
        out_specs.append(hbm)
        out_shape.append(jax.ShapeDtypeStruct((4,) + bcast.shape, bcast.dtype))
        scratch += _SIDE_SEMS
        args.append(bcast)
    got = pl.pallas_call(body, grid=grid, in_specs=in_specs, out_specs=out_specs, out_shape=out_shape,
                         scratch_shapes=scratch, compiler_params=_cp(3), name=name)(*args)
    return got[0], (got[1] if res is not None else None), (got[-1] if bcast is not None else None)


def mm_nt(a, b, out_dtype, name, exch=None):
    M, N = a.shape
    S, K, Nb = b.shape
    assert N == S * Nb
    tm, tko, tnr = _lane_tile(M, 768), _lane_tile(K, 1536), _lane_tile(Nb, 1536)
    nr = N // tnr
    grid = (M // tm, K // tko, nr)
    tps = Nb // tnr
    b_map = lambda i, j, r: (r // tps, j, r % tps)

    def body(a_ref, b_ref, *rest):
        if exch is None:
            o_ref, acc_ref = rest
        else:
            e_ref, o_ref, p_ref, acc_ref, send_sems, recv_sems, local_sem = rest
            first, last = _first_last(grid)
            side = _side_copies(lambda j: e_ref.at[j], p_ref, send_sems, recv_sems, local_sem)
            pl.when(first)(lambda: _side_start(*side))
        r = pl.program_id(2)

        @pl.when(r == 0)
        def _():
            acc_ref[...] = jnp.zeros_like(acc_ref)

        acc_ref[...] += _dot_nt(a_ref[...], b_ref[...])

        @pl.when(r == nr - 1)
        def _():
            o_ref[...] = acc_ref[...].astype(o_ref.dtype)

        if exch is not None:
            pl.when(last)(lambda: _side_finish(*side))

    hbm = pl.BlockSpec(memory_space=pl.ANY)
    in_specs = [pl.BlockSpec((tm, tnr), lambda i, j, r: (i, r)), pl.BlockSpec((None, tko, tnr), b_map)]
    out_specs = [pl.BlockSpec((tm, tko), lambda i, j, r: (i, j))]
    out_shape = [jax.ShapeDtypeStruct((M, K), out_dtype)]
    scratch = [pltpu.VMEM((tm, tko), F32)]
    args = [a, b]
    if exch is not None:
        in_specs.append(hbm)
        out_specs.append(hbm)
        out_shape.append(jax.ShapeDtypeStruct(exch.shape, exch.dtype))
        scratch += _SIDE_SEMS
        args.append(exch)
    res = pl.pallas_call(body, grid=grid, in_specs=in_specs, out_specs=out_specs, out_shape=out_shape,
                         scratch_shapes=scratch, compiler_params=_cp(3), name=name)(*args)
    return res[0] if exch is None else res


def mm_tn(a, c, S, name):
    M, K = a.shape
    N = c.shape[1]
    Kb, Nb = K, N // S
    tm, tk, tn = _lane_tile(M, 768), _lane_tile(Kb, 1536), _lane_tile(Nb, 1536)
    nm = M // tm
    tps = Nb // tn
    o_map = lambda i, j, m: (j // tps, i, j % tps)

    def body(a_ref, c_ref, o_ref, acc_ref):
        m = pl.program_id(2)

        @pl.when(m == 0)
        def _():
            acc_ref[...] = jnp.zeros_like(acc_ref)

        acc_ref[...] += _dot_tn(a_ref[...], c_ref[...])

        @pl.when(m == nm - 1)
        def _():
            o_ref[...] = acc_ref[...].astype(BF16)

    return pl.pallas_call(
        body, grid=(K // tk, N // tn, nm),
        in_specs=[pl.BlockSpec((tm, tk), lambda i, j, m: (m, i)), pl.BlockSpec((tm, tn), lambda i, j, m: (m, j))],
        out_specs=pl.BlockSpec((None, tk, tn), o_map),
        out_shape=jax.ShapeDtypeStruct((S, Kb, Nb), BF16),
        scratch_shapes=[pltpu.VMEM((tk, tn), F32)], compiler_params=_cp(3), name=name)(a, c)


def _mod_spec(D, ncr):
    return pl.BlockSpec((None, N_MOD, D), lambda i: (jnp.where(i < ncr, 0, 1), 0, 0))


def norm_mod(x, g, modv, i_sh, i_sc, ncr, name):
    T, D = x.shape

    def body(x_ref, g_ref, m_ref, h_ref):
        xv = x_ref[...]
        r = lax.rsqrt(jnp.mean(xv * xv, axis=-1, keepdims=True) + EPS)
        n = xv * r * g_ref[...]
        h_ref[...] = (n * (1.0 + m_ref[i_sc:i_sc + 1, :]) + m_ref[i_sh:i_sh + 1, :]).astype(BF16)

    return pl.pallas_call(
        body, grid=(T // R,),
        in_specs=[pl.BlockSpec((R, D), lambda i: (i, 0)), pl.BlockSpec((1, D), lambda i: (0, 0)), _mod_spec(D, ncr)],
        out_specs=pl.BlockSpec((R, D), lambda i: (i, 0)),
        out_shape=jax.ShapeDtypeStruct((T, D), BF16), compiler_params=_cp(1), name=name)(x, g, modv)


def norm_mod_bwd(x, g, modv, dh, dres, i_sc, ncr, name):
    T, D = x.shape

    def body(x_ref, g_ref, m_ref, dh_ref, dr_ref, dx_ref, acc_ref):
        i = pl.program_id(0)

        @pl.when(i == 0)
        def _():
            acc_ref[...] = jnp.zeros_like(acc_ref)

        xv = x_ref[...]
        r = lax.rsqrt(jnp.mean(xv * xv, axis=-1, keepdims=True) + EPS)
        xh = xv * r
        gv = g_ref[...]
        dhv = dh_ref[...]
        dn = dhv * (1.0 + m_ref[i_sc:i_sc + 1, :])
        s_sh = jnp.sum(dhv, axis=0, keepdims=True)
        s_sc = jnp.sum(dhv * (xh * gv), axis=0, keepdims=True)
        acc_ref[4:5, :] += jnp.sum(dn * xh, axis=0, keepdims=True)
        dxh = dn * gv
        dx_ref[...] = dr_ref[...] + r * (dxh - xh * jnp.mean(dxh * xh, axis=-1, keepdims=True))

        @pl.when(i < ncr)
        def _():
            acc_ref[0:1, :] += s_sh
            acc_ref[1:2, :] += s_sc

        @pl.when(i >= ncr)
        def _():
            acc_ref[2:3, :] += s_sh
            acc_ref[3:4, :] += s_sc

    row = pl.BlockSpec((R, D), lambda i: (i, 0))
    return pl.pallas_call(
        body, grid=(T // R,),
        in_specs=[row, pl.BlockSpec((1, D), lambda i: (0, 0)), _mod_spec(D, ncr), row, row],
        out_specs=[row, pl.BlockSpec((8, D), lambda i: (0, 0))],
        out_shape=[jax.ShapeDtypeStruct((T, D), F32), jax.ShapeDtypeStruct((8, D), F32)],
        compiler_params=_cp(1), name=name)(x, g, modv, dh, dres)


def gate_res_bwd(dx, y, modv, i_g, ncr, name):
    T, D = dx.shape

    def body(dx_ref, y_ref, m_ref, dy_ref, acc_ref):
        i = pl.program_id(0)

        @pl.when(i == 0)
        def _():
            acc_ref[...] = jnp.zeros_like(acc_ref)

        dxv = dx_ref[...]
        dy_ref[...] = (m_ref[i_g:i_g + 1, :] * dxv).astype(BF16)
        s = jnp.sum(dxv * y_ref[...], axis=0, keepdims=True)

        @pl.when(i < ncr)
        def _():
            acc_ref[0:1, :] += s

        @pl.when(i >= ncr)
        def _():
            acc_ref[1:2, :] += s

    row = pl.BlockSpec((R, D), lambda i: (i, 0))
    return pl.pallas_call(
        body, grid=(T // R,), in_specs=[row, row, _mod_spec(D, ncr)],
        out_specs=[row, pl.BlockSpec((8, D), lambda i: (0, 0))],
        out_shape=[jax.ShapeDtypeStruct((T, D), BF16), jax.ShapeDtypeStruct((8, D), F32)],
        compiler_params=_cp(1), name=name)(dx, y, modv)


def loss_head(x, target, g, ncr, name):
    T, D = x.shape

    def body(x_ref, t_ref, g_ref, dx_ref, acc_ref):
        i = pl.program_id(0)

        @pl.when(i == 0)
        def _():
            acc_ref[...] = jnp.zeros_like(acc_ref)

        @pl.when(i < ncr)
        def _():
            dx_ref[...] = jnp.zeros_like(dx_ref)

        @pl.when(i >= ncr)
        def _():
            xv = x_ref[...]
            r = lax.rsqrt(jnp.mean(xv * xv, axis=-1, keepdims=True) + EPS)
            xh = xv * r
            gv = g_ref[...]
            e = xh * gv - t_ref[...]
            acc_ref[1:2, :] += jnp.sum(e * e, axis=0, keepdims=True)
            dy = e * (1.0 / D)
            acc_ref[0:1, :] += jnp.sum(dy * xh, axis=0, keepdims=True)
            dxh = dy * gv
            dx_ref[...] = r * (dxh - xh * jnp.mean(dxh * xh, axis=-1, keepdims=True))

    row = pl.BlockSpec((R, D), lambda i: (i, 0))
    return pl.pallas_call(
        body, grid=(T // R,),
        in_specs=[row, pl.BlockSpec((R, D), lambda i: (jnp.maximum(i - ncr, 0), 0)), pl.BlockSpec((1, D), lambda i: (0, 0))],
        out_specs=[row, pl.BlockSpec((8, D), lambda i: (0, 0))],
        out_shape=[jax.ShapeDtypeStruct((T, D), F32), jax.ShapeDtypeStruct((8, D), F32)],
        compiler_params=_cp(1), name=name)(x, target, g)


def _conv31_specs(T):
    nh = R // HALO
    pv = lambda i: jnp.maximum(i * nh - 1, 0)
    nx = lambda i: jnp.minimum((i + 1) * nh, T // HALO - 1)
    return pv, nx


def _shifted_copies(E):
    n = E.shape[1]
    for s in range(1, 8):
        E[s, 0:n - 8, :] = E[0, pl.ds(s, n - 8), :]


def _tap31(E, r0, o):
    return E[o % 8, pl.ds(r0 + 8 * (o // 8), SUB), :]


def conv31_fwd(p, w32, b, RW, CW, ncr, name):
    T = p.shape[0]
    nT, cbk = T // R, LANES
    n = R + 2 * HALO
    a0, g0 = 4 * RW // cbk, (4 * RW + CW) // cbk
    pv, nx = _conv31_specs(T)

    def body(a, g, ap, gp, an, gn, w, bb, uc, E):
        i = pl.program_id(0)
        has_prev = jnp.logical_and(i != 0, i != ncr)
        has_next = jnp.logical_and(i != ncr - 1, i != nT - 1)
        glu = lambda u, v: u.astype(F32) * _sig(v.astype(F32))
        E[0, 0:HALO, :] = jnp.where(has_prev, glu(ap[...], gp[...]), 0.0)
        E[0, HALO:HALO + R, :] = glu(a[...], g[...])
        E[0, HALO + R:, :] = jnp.where(has_next, glu(an[...], gn[...]), 0.0)
        _shifted_copies(E)
        for r0 in range(0, R, SUB):
            acc = jnp.broadcast_to(bb[...], (SUB, cbk))
            for k in range(CONV_K):
                acc = acc + w[k:k + 1, :] * _tap31(E, r0, 1 + k)
            uc[r0:r0 + SUB, :] = acc

    cur = lambda c0: pl.BlockSpec((R, cbk), lambda i, j: (i, c0 + j))
    hp = lambda c0: pl.BlockSpec((HALO, cbk), lambda i, j: (pv(i), c0 + j))
    hn = lambda c0: pl.BlockSpec((HALO, cbk), lambda i, j: (nx(i), c0 + j))
    return pl.pallas_call(
        body, grid=(nT, CW // cbk),
        in_specs=[cur(a0), cur(g0), hp(a0), hp(g0), hn(a0), hn(g0),
                  pl.BlockSpec((32, cbk), lambda i, j: (0, j)), pl.BlockSpec((1, cbk), lambda i, j: (0, j))],
        out_specs=pl.BlockSpec((R, cbk), lambda i, j: (i, j)),
        out_shape=jax.ShapeDtypeStruct((T, CW), F32),
        scratch_shapes=[pltpu.VMEM((8, n, cbk), F32)],
        compiler_params=_cp(2), name=name)(p, p, p, p, p, p, w32, b)


def conv31_bwd(duc, p, w32, RW, CW, ncr, name):
    T = p.shape[0]
    nT, cbk = T // R, LANES
    n = R + 2 * HALO
    a0, g0 = 4 * RW // cbk, (4 * RW + CW) // cbk
    pv, nx = _conv31_specs(T)

    def body(d, dp_, dn_, a, g, ap, gp, an, gn, w, da, dg, accw, U, Dd):
        i = pl.program_id(1)

        @pl.when(i == 0)
        def _():
            accw[...] = jnp.zeros_like(accw)

        has_prev = jnp.logical_and(i != 0, i != ncr)
        has_next = jnp.logical_and(i != ncr - 1, i != nT - 1)
        glu = lambda u, v: u.astype(F32) * _sig(v.astype(F32))
        U[0, 0:HALO, :] = jnp.where(has_prev, glu(ap[...], gp[...]), 0.0)
        U[0, HALO:HALO + R, :] = glu(a[...], g[...])
        U[0, HALO + R:, :] = jnp.where(has_next, glu(an[...], gn[...]), 0.0)
        Dd[0, 0:HALO, :] = jnp.where(has_prev, dp_[...], 0.0)
        Dd[0, HALO:HALO + R, :] = d[...]
        Dd[0, HALO + R:, :] = jnp.where(has_next, dn_[...], 0.0)
        _shifted_copies(U)
        _shifted_copies(Dd)
        for r0 in range(0, R, SUB):
            du = jnp.zeros((SUB, cbk), F32)
            for k in range(CONV_K):
                du = du + w[k:k + 1, :] * _tap31(Dd, r0, HALO + 15 - k)
            av = a[r0:r0 + SUB, :].astype(F32)
            sg = _sig(g[r0:r0 + SUB, :].astype(F32))
            da[r0:r0 + SUB, :] = (du * sg).astype(BF16)
            dg[r0:r0 + SUB, :] = (du * av * sg * (1.0 - sg)).astype(BF16)
            dcur = d[r0:r0 + SUB, :]
            for k in range(CONV_K):
                accw[k:k + 1, :] += jnp.sum(dcur * _tap31(U, r0, 1 + k), axis=0, keepdims=True)
            accw[31:32, :] += jnp.sum(dcur, axis=0, keepdims=True)

    cur = lambda c0: pl.BlockSpec((R, cbk), lambda j, i: (i, c0 + j))
    hp = lambda c0: pl.BlockSpec((HALO, cbk), lambda j, i: (pv(i), c0 + j))
    hn = lambda c0: pl.BlockSpec((HALO, cbk), lambda j, i: (nx(i), c0 + j))
    out = pl.BlockSpec((R, cbk), lambda j, i: (i, j))
    ext = pltpu.VMEM((8, n, cbk), F32)
    return pl.pallas_call(
        body, grid=(CW // cbk, nT),
        in_specs=[cur(0), hp(0), hn(0), cur(a0), cur(g0), hp(a0), hp(g0), hn(a0), hn(g0),
                  pl.BlockSpec((32, cbk), lambda j, i: (0, j))],
        out_specs=[out, out, pl.BlockSpec((32, cbk), lambda j, i: (0, j))],
        out_shape=[jax.ShapeDtypeStruct((T, CW), BF16), jax.ShapeDtypeStruct((T, CW), BF16),
                   jax.ShapeDtypeStruct((32, CW), F32)],
        scratch_shapes=[ext, ext],
        compiler_params=_cp(2), name=name)(duc, duc, duc, p, p, p, p, p, p, w32)


def _rope(v, cosv, sinv, first):
    swapped = jnp.where(first, pltpu.roll(v, 96, 1), pltpu.roll(v, 32, 1))
    return v * cosv + swapped * sinv


def _unrope(v, cosv, sinv, first):
    z = v * sinv
    return v * cosv + jnp.where(first, pltpu.roll(z, 96, 1), pltpu.roll(z, 32, 1))


def _decay_tables(lg_ref, H, DM, QD, KD, CD):
    n = lax.broadcasted_iota(jnp.int32, (RC, DH), 0).astype(F32)
    m = lax.broadcasted_iota(jnp.int32, (RC, DH), 1).astype(F32)
    for d in range(2):
        for h in range(H):
            i = d * H + h
            lg = lg_ref[i:i + 1, :]
            diff = (n - m) if d == 0 else (m - n)
            DM[i] = jnp.where(diff >= 0, jnp.exp(lg * jnp.maximum(diff, 0.0)), 0.0)
            QD[i] = jnp.exp(lg * ((n + 1.0) if d == 0 else (RC - n)))
            KD[i] = jnp.exp(lg * ((RC - 1.0 - n) if d == 0 else n))
            CD[i] = jnp.exp(lg * float(RC)) + jnp.zeros((RC, DH), F32)


def _chunk_orders(NC, ncc):
    cf = lambda s: s
    cb = lambda s: jnp.where(s < ncc, ncc - 1 - s, NC - 1 - (s - ncc))
    return cf, cb


def ret_fwd(p, cosT, sinT, lgt, H, ncc, name):
    T = p.shape[0]
    RW, NC = H * DH, T // RC
    cf, cb = _chunk_orders(NC, ncc)
    scale = DH ** -0.5

    def body(qf, kf, vf, qb, kb, vb, cosf, sinf, cosb, sinb, lg_ref, of_ref, ob_ref, sf_ref, sb_ref, S, DM, QD, KD, CD):
        s = pl.program_id(0)

        @pl.when(s == 0)
        def _():
            S[...] = jnp.zeros_like(S)
            _decay_tables(lg_ref, H, DM, QD, KD, CD)

        first = (lax.broadcasted_iota(jnp.int32, (RC, DH), 1) % 64) < 32
        for d, (q_ref, k_ref, v_ref, c_ref, s_ref, o_ref, st_ref) in enumerate(
                ((qf, kf, vf, cosf, sinf, of_ref, sf_ref), (qb, kb, vb, cosb, sinb, ob_ref, sb_ref))):
            cosv, sinv = c_ref[...], s_ref[...]
            for h in range(H):
                hs, i = slice(h * DH, (h + 1) * DH), d * H + h
                q16 = _rope(q_ref[:, hs].astype(F32), cosv, sinv, first).astype(BF16)
                k = _rope(k_ref[:, hs].astype(F32), cosv, sinv, first) * scale
                k16 = k.astype(BF16)
                v = v_ref[:, hs]
                s_in = S[i]
                s16 = s_in.astype(BF16)
                st_ref[h] = s16
                sc = _dot_nt(q16, k16) * DM[i]
                o_ref[:, hs] = _dot(sc.astype(BF16), v) + _dot(q16, s16) * QD[i]
                S[i] = s_in * CD[i] + _dot_tn((k * KD[i]).astype(BF16), v)

    pspec = lambda col, cm: pl.BlockSpec((RC, RW), lambda s: (cm(s), col))
    tspec = lambda cm: pl.BlockSpec((RC, DH), lambda s: (cm(s), 0))
    ospec = lambda cm: pl.BlockSpec((RC, RW), lambda s: (cm(s), 0))
    stspec = pl.BlockSpec((None, H, DH, DH), lambda s: (s, 0, 0, 0))
    tab = pltpu.VMEM((2 * H, RC, DH), F32)
    return pl.pallas_call(
        body, grid=(NC,),
        in_specs=[pspec(0, cf), pspec(1, cf), pspec(2, cf), pspec(0, cb), pspec(1, cb), pspec(2, cb),
                  tspec(cf), tspec(cf), tspec(cb), tspec(cb), pl.BlockSpec((2 * H, DH), lambda s: (0, 0))],
        out_specs=[ospec(cf), ospec(cb), stspec, stspec],
        out_shape=[jax.ShapeDtypeStruct((T, RW), F32), jax.ShapeDtypeStruct((T, RW), F32),
                   jax.ShapeDtypeStruct((NC, H, DH, DH), BF16), jax.ShapeDtypeStruct((NC, H, DH, DH), BF16)],
        scratch_shapes=[tab, tab, tab, tab, tab], compiler_params=_cp(1), name=name)(
            p, p, p, p, p, p, cosT, sinT, cosT, sinT, lgt)


def ret_bwd(p, do, cosT, sinT, lgt, stf, stb, H, ncc, name):
    T = p.shape[0]
    RW, NC = H * DH, T // RC
    cf0, cb0 = _chunk_orders(NC, ncc)
    cf = lambda sp: cf0(NC - 1 - sp)
    cb = lambda sp: cb0(NC - 1 - sp)
    scale = DH ** -0.5

    def body(qf, kf, vf, qb, kb, vb, dof, dob, cosf, sinf, cosb, sinb, lg_ref, stf_ref, stb_ref,
             dqf, dkf, dvf, dqb, dkb, dvb, glg, dS, DM, QD, KD, CD):
        sp = pl.program_id(0)

        @pl.when(sp == 0)
        def _():
            dS[...] = jnp.zeros_like(dS)
            glg[...] = jnp.zeros_like(glg)
            _decay_tables(lg_ref, H, DM, QD, KD, CD)

        first = (lax.broadcasted_iota(jnp.int32, (RC, DH), 1) % 64) < 32
        n = lax.broadcasted_iota(jnp.int32, (RC, DH), 0).astype(F32)
        m = lax.broadcasted_iota(jnp.int32, (RC, DH), 1).astype(F32)
        for d, (q_ref, k_ref, v_ref, do_ref, c_ref, s_ref, st_ref, dq_ref, dk_ref, dv_ref) in enumerate(
                ((qf, kf, vf, dof, cosf, sinf, stf_ref, dqf, dkf, dvf),
                 (qb, kb, vb, dob, cosb, sinb, stb_ref, dqb, dkb, dvb))):
            cosv, sinv = c_ref[...], s_ref[...]
            diff = (n - m) if d == 0 else (m - n)
            posq = (n + 1.0) if d == 0 else (RC - n)
            posk = (RC - 1.0 - n) if d == 0 else n
            for h in range(H):
                hs, i = slice(h * DH, (h + 1) * DH), d * H + h
                q = _rope(q_ref[:, hs].astype(F32), cosv, sinv, first)
                k = _rope(k_ref[:, hs].astype(F32), cosv, sinv, first) * scale
                q16, k16 = q.astype(BF16), k.astype(BF16)
                v = v_ref[:, hs]
                s_in = st_ref[h]
                ds_out = dS[i]
                ds16 = ds_out.astype(BF16)
                do16 = do_ref[:, hs]
                doq = (do16.astype(F32) * QD[i]).astype(BF16)
                a = _dot_nt(q16, k16) * DM[i]
                da_raw = _dot_nt(do16, v)
                da16 = (da_raw * DM[i]).astype(BF16)
                dq_state = _dot_nt(doq, s_in)
                dk_state = _dot_nt(v, ds16) * KD[i]
                dqr = _dot(da16, k16) + dq_state
                dkr = _dot_tn(da16, q16) + dk_state
                dv_ref[:, hs] = _dot_tn(a.astype(BF16), do16) + _dot((k * KD[i]).astype(BF16), ds16)
                dS[i] = ds_out * CD[i] + _dot_tn(q16, doq)
                glg[i] += (da_raw * a * diff + posq * q * dq_state + posk * k * dk_state
                           + float(RC) * CD[i] * ds_out * s_in.astype(F32))
                dq_ref[:, hs] = _unrope(dqr, cosv, sinv, first)
                dk_ref[:, hs] = _unrope(dkr, cosv, sinv, first) * scale

    pspec = lambda col, cm: pl.BlockSpec((RC, RW), lambda s: (cm(s), col))
    tspec = lambda cm: pl.BlockSpec((RC, DH), lambda s: (cm(s), 0))
    ospec = lambda cm: pl.BlockSpec((RC, RW), lambda s: (cm(s), 0))
    stspec = pl.BlockSpec((None, H, DH, DH), lambda s: (NC - 1 - s, 0, 0, 0))
    tab = pltpu.VMEM((2 * H, RC, DH), F32)
    big = jax.ShapeDtypeStruct((T, RW), F32)
    return pl.pallas_call(
        body, grid=(NC,),
        in_specs=[pspec(0, cf), pspec(1, cf), pspec(2, cf), pspec(0, cb), pspec(1, cb), pspec(2, cb),
                  ospec(cf), ospec(cb), tspec(cf), tspec(cf), tspec(cb), tspec(cb),
                  pl.BlockSpec((2 * H, DH), lambda s: (0, 0)), stspec, stspec],
        out_specs=[ospec(cf), ospec(cf), ospec(cf), ospec(cb), ospec(cb), ospec(cb),
                   pl.BlockSpec((2 * H, RC, DH), lambda s: (0, 0, 0))],
        out_shape=[big, big, big, big, big, big, jax.ShapeDtypeStruct((2 * H, RC, DH), F32)],
        scratch_shapes=[tab, tab, tab, tab, tab], compiler_params=_cp(1), name=name)(
            p, p, p, p, p, p, do, do, cosT, sinT, cosT, sinT, lgt, stf, stb)


def mix_fwd(o_f, o_b, p, uc, lng, lnb, H, name):
    T, RW = o_f.shape
    CW = uc.shape[1]

    def body(of_ref, ob_ref, g_ref, uc_ref, lg_ref, lb_ref, out_ref):
        for h in range(H):
            hs = slice(h * DH, (h + 1) * DH)
            o = of_ref[:, hs] + ob_ref[:, hs]
            on = o * lax.rsqrt(jnp.mean(o * o, axis=-1, keepdims=True) + EPS)
            gv = g_ref[:, hs].astype(F32)
            out_ref[:, hs] = (gv * _sig(gv) * on).astype(BF16)
        u = uc_ref[...]
        mu = jnp.mean(u, axis=-1, keepdims=True)
        var = jnp.mean(jnp.square(u - mu), axis=-1, keepdims=True)
        z = (u - mu) * lax.rsqrt(var + EPS) * lg_ref[...] + lb_ref[...]
        out_ref[:, RW:] = (z * _sig(z)).astype(BF16)

    rw = pl.BlockSpec((R, RW), lambda i: (i, 0))
    vec = pl.BlockSpec((1, CW), lambda i: (0, 0))
    return pl.pallas_call(
        body, grid=(T // R,),
        in_specs=[rw, rw, pl.BlockSpec((R, RW), lambda i: (i, 3)), pl.BlockSpec((R, CW), lambda i: (i, 0)), vec, vec],
        out_specs=pl.BlockSpec((R, RW + CW), lambda i: (i, 0)),
        out_shape=jax.ShapeDtypeStruct((T, RW + CW), BF16), compiler_params=_cp(1), name=name)(o_f, o_b, p, uc, lng, lnb)


def mix_bwd(dmix, o_f, o_b, p, uc, lng, lnb, H, name):
    T, RW = o_f.shape
    CW = uc.shape[1]

    def body(dm_ref, of_ref, ob_ref, g_ref, uc_ref, lg_ref, lb_ref, do_ref, dg_ref, duc_ref, acc_ref):
        i = pl.program_id(0)

        @pl.when(i == 0)
        def _():
            acc_ref[...] = jnp.zeros_like(acc_ref)

        for h in range(H):
            hs = slice(h * DH, (h + 1) * DH)
            o = of_ref[:, hs] + ob_ref[:, hs]
            r = lax.rsqrt(jnp.mean(o * o, axis=-1, keepdims=True) + EPS)
            on = o * r
            gv = g_ref[:, hs].astype(F32)
            sg = _sig(gv)
            dmr = dm_ref[:, hs].astype(F32)
            dg_ref[:, hs] = (dmr * on * (sg * (1.0 + gv * (1.0 - sg)))).astype(BF16)
            don = dmr * (gv * sg)
            do_ref[:, hs] = (r * (don - on * jnp.mean(don * on, axis=-1, keepdims=True))).astype(BF16)
        u = uc_ref[...]
        mu = jnp.mean(u, axis=-1, keepdims=True)
        rs = lax.rsqrt(jnp.mean(jnp.square(u - mu), axis=-1, keepdims=True) + EPS)
        zh = (u - mu) * rs
        lg = lg_ref[...]
        z = zh * lg + lb_ref[...]
        sz = _sig(z)
        dz = dm_ref[:, RW:].astype(F32) * (sz * (1.0 + z * (1.0 - sz)))
        acc_ref[0:1, :] += jnp.sum(dz * zh, axis=0, keepdims=True)
        acc_ref[1:2, :] += jnp.sum(dz, axis=0, keepdims=True)
        dzh = dz * lg
        duc_ref[...] = rs * (dzh - jnp.mean(dzh, axis=-1, keepdims=True)
                             - zh * jnp.mean(dzh * zh, axis=-1, keepdims=True))

    rw = pl.BlockSpec((R, RW), lambda i: (i, 0))
    cw = pl.BlockSpec((R, CW), lambda i: (i, 0))
    vec = pl.BlockSpec((1, CW), lambda i: (0, 0))
    return pl.pallas_call(
        body, grid=(T // R,),
        in_specs=[pl.BlockSpec((R, RW + CW), lambda i: (i, 0)), rw, rw, pl.BlockSpec((R, RW), lambda i: (i, 3)), cw, vec, vec],
        out_specs=[rw, rw, cw, pl.BlockSpec((8, CW), lambda i: (0, 0))],
        out_shape=[jax.ShapeDtypeStruct((T, RW), BF16), jax.ShapeDtypeStruct((T, RW), BF16),
                   jax.ShapeDtypeStruct((T, CW), F32), jax.ShapeDtypeStruct((8, CW), F32)],
        compiler_params=_cp(1), name=name)(dmix, o_f, o_b, p, uc, lng, lnb)


def assemble_dp(dqf, dqb, dkf, dkb, dvf, dvb, dg, da, dbg, name):
    T, RW = dqf.shape
    CW = da.shape[1]

    def body(qf, qb, kf, kb, vf, vb, g, a, b, out):
        out[:, 0:RW] = (qf[...] + qb[...]).astype(BF16)
        out[:, RW:2 * RW] = (kf[...] + kb[...]).astype(BF16)
        out[:, 2 * RW:3 * RW] = (vf[...] + vb[...]).astype(BF16)
        out[:, 3 * RW:4 * RW] = g[...]
        out[:, 4 * RW:4 * RW + CW] = a[...]
        out[:, 4 * RW + CW:] = b[...]

    rw = pl.BlockSpec((R, RW), lambda i: (i, 0))
    cw = pl.BlockSpec((R, CW), lambda i: (i, 0))
    W = 4 * RW + 2 * CW
    return pl.pallas_call(
        body, grid=(T // R,), in_specs=[rw] * 7 + [cw, cw], out_specs=pl.BlockSpec((R, W), lambda i: (i, 0)),
        out_shape=jax.ShapeDtypeStruct((T, W), BF16), compiler_params=_cp(1), name=name)(
            dqf, dqb, dkf, dkb, dvf, dvb, dg, da, dbg)


FPAD = 8


def _fill_plain(plain, cur, prv, nxt, has_prev, has_next):
    n, cb = plain.shape[0] - 2 * FPAD, plain.shape[1]
    pv, nv = prv[...], nxt[...]
    plain[0:FPAD, :] = jnp.zeros((FPAD, cb), F32)
    plain[FPAD + n:, :] = jnp.zeros((FPAD, cb), F32)
    plain[FPAD:FPAD + GRID_W, :] = jnp.where(has_prev, pv, jnp.zeros_like(pv)).astype(F32)
    plain[FPAD + GRID_W:FPAD + GRID_W + R, :] = cur[...].astype(F32)
    plain[FPAD + GRID_W + R:FPAD + n, :] = jnp.where(has_next, nv, jnp.zeros_like(nv)).astype(F32)


def _fill_ext(bufs, cur, prv, nxt, has_prev, has_next, is_ctx):
    left, plain, right = bufs
    n, cb = left.shape
    _fill_plain(plain, cur, prv, nxt, has_prev, has_next)
    left[...] = plain[pl.ds(FPAD - 1, n), :]
    right[...] = plain[pl.ds(FPAD + 1, n), :]
    if not is_ctx:
        for r in range(0, n, GRID_W):
            left[r:r + 1, :] = jnp.zeros((1, cb), F32)
            right[r + GRID_W - 1:r + GRID_W, :] = jnp.zeros((1, cb), F32)


def _taps(is_ctx):
    return [(dr, dc) for dr in ((0,) if is_ctx else (-1, 0, 1)) for dc in (-1, 0, 1)]


def _tap_src(bufs, r0, c0, dr, dc):
    off = (FPAD if dc == 0 else 0) + GRID_W + r0 + GRID_W * dr
    return bufs[dc + 1][pl.ds(off, SUB), pl.ds(c0, LANES)]


def _conv9(bufs, w, r0, c0, is_ctx, flip):
    acc = jnp.zeros((SUB, LANES), F32)
    for dr, dc in _taps(is_ctx):
        widx = (dr + 1) * 3 + dc + 1
        src = _tap_src(bufs, r0, c0, -dr, -dc) if flip else _tap_src(bufs, r0, c0, dr, dc)
        acc = acc + w[widx:widx + 1, pl.ds(c0, LANES)] * src
    return acc


def _ffn_specs(T, cb, order):
    nq = R // GRID_W
    pv = lambda i: jnp.maximum(i * nq - 1, 0)
    nx = lambda i: jnp.minimum((i + 1) * nq, T // GRID_W - 1)
    if order == 'ij':
        mk = lambda blk, rf, c0: pl.BlockSpec(blk, lambda i, j: (rf(i), c0 + j))
    else:
        mk = lambda blk, rf, c0: pl.BlockSpec(blk, lambda j, i: (rf(i), c0 + j))
    cur = lambda c0: mk((R, cb), lambda i: i, c0)
    hp = lambda c0: mk((GRID_W, cb), pv, c0)
    hn = lambda c0: mk((GRID_W, cb), nx, c0)
    vec = lambda rows: mk((rows, cb), lambda i: 0, 0)
    return cur, hp, hn, vec


def _ffn_flags(i, ncr, nT):
    return i > ncr, jnp.logical_and(i >= ncr, i != nT - 1)


def _ffn_scratch(cb, sets):
    n = R + 2 * GRID_W
    return [pltpu.VMEM((n, cb), F32), pltpu.VMEM((n + 2 * FPAD, cb), F32), pltpu.VMEM((n, cb), F32)] * sets


def ffn_act(up, w16, b, CF, ncr, name):
    T = up.shape[0]
    nT, cb = T // R, 2 * LANES
    ncb = CF // cb
    cur, hp, hn, vec = _ffn_specs(T, cb, 'ij')

    def body(g, v, gp, gn, w, bb, out, gc_out, e0, e1, e2):
        i = pl.program_id(0)
        has_prev, has_next = _ffn_flags(i, ncr, nT)
        bufs = (e0, e1, e2)

        def run(is_ctx):
            _fill_ext(bufs, g, gp, gn, has_prev, has_next, is_ctx)
            for r0 in range(0, R, SUB):
                for c0 in range(0, cb, LANES):
                    gc = _conv9(bufs, w, r0, c0, is_ctx, False) + bb[:, pl.ds(c0, LANES)]
                    val = v[r0:r0 + SUB, c0:c0 + LANES].astype(F32)
                    out[r0:r0 + SUB, c0:c0 + LANES] = (gc * _sig(gc) * val).astype(BF16)
                    gc_out[r0:r0 + SUB, c0:c0 + LANES] = gc.astype(BF16)

        pl.when(i < ncr)(lambda: run(True))
        pl.when(i >= ncr)(lambda: run(False))

    sds = jax.ShapeDtypeStruct((T, CF), BF16)
    return pl.pallas_call(
        body, grid=(nT, ncb), in_specs=[cur(0), cur(ncb), hp(0), hn(0), vec(16), vec(1)], out_specs=[cur(0), cur(0)],
        out_shape=[sds, sds],
        scratch_shapes=_ffn_scratch(cb, 1), compiler_params=_cp(2), name=name)(up, up, up, up, w16, b)


def ffn_act_bwd1(up, gcs, dact, CF, name):
    T = up.shape[0]
    cb = 2 * LANES
    ncb = CF // cb

    def body(v, gc_ref, da, dgc, dup, accb):
        @pl.when(pl.program_id(1) == 0)
        def _():
            accb[...] = jnp.zeros_like(accb)

        gc = gc_ref[...].astype(F32)
        sg = _sig(gc)
        dav = da[...].astype(F32)
        dup[...] = (dav * gc * sg).astype(BF16)
        d = dav * v[...].astype(F32) * (sg * (1.0 + gc * (1.0 - sg)))
        dgc[...] = d.astype(BF16)
        accb[0:1, :] += jnp.sum(d, axis=0, keepdims=True)

    blk = lambda c0: pl.BlockSpec((R, cb), lambda j, i: (i, c0 + j))
    return pl.pallas_call(
        body, grid=(ncb, T // R), in_specs=[blk(ncb), blk(0), blk(0)],
        out_specs=[blk(0), blk(ncb), pl.BlockSpec((8, cb), lambda j, i: (0, j))],
        out_shape=[jax.ShapeDtypeStruct((T, CF), BF16), jax.ShapeDtypeStruct((T, 2 * CF), BF16),
                   jax.ShapeDtypeStruct((8, CF), F32)],
        compiler_params=_cp(2), name=name)(up, gcs, dact)


def ffn_act_bwd2(dgc, up, dup, w16, CF, ncr, name):
    T = up.shape[0]
    nT, cb = T // R, 2 * LANES
    ncb = CF // cb
    cur, hp, hn, vec = _ffn_specs(T, cb, 'ji')

    def body(d, dp_, dn_, g, gp, gn, w, dup_in, dgate, accw, d0, d1, d2, gplain):
        i = pl.program_id(1)

        @pl.when(i == 0)
        def _():
            accw[...] = jnp.zeros_like(accw)

        has_prev, has_next = _ffn_flags(i, ncr, nT)
        dbufs = (d0, d1, d2)
        _fill_plain(gplain, g, gp, gn, has_prev, has_next)

        def run(is_ctx):
            _fill_ext(dbufs, d, dp_, dn_, has_prev, has_next, is_ctx)
            for r0 in range(0, R, SUB):
                for c0 in range(0, cb, LANES):
                    dgate[r0:r0 + SUB, c0:c0 + LANES] = _conv9(dbufs, w, r0, c0, is_ctx, True).astype(BF16)
                    for dr, dc in _taps(is_ctx):
                        widx = (dr + 1) * 3 + dc + 1
                        dmov = _tap_src(dbufs, r0, c0, 0, -dc)
                        gsrc = gplain[pl.ds(FPAD + GRID_W + r0 + GRID_W * dr, SUB), pl.ds(c0, LANES)]
                        accw[widx:widx + 1, pl.ds(c0, LANES)] += jnp.sum(dmov * gsrc, axis=0, keepdims=True)

        pl.when(i < ncr)(lambda: run(True))
        pl.when(i >= ncr)(lambda: run(False))

    return pl.pallas_call(
        body, grid=(ncb, nT),
        in_specs=[cur(0), hp(0), hn(0), cur(0), hp(0), hn(0), vec(16), pl.BlockSpec(memory_space=pl.ANY)],
        out_specs=[cur(0), vec(16)],
        out_shape=[jax.ShapeDtypeStruct((T, 2 * CF), BF16), jax.ShapeDtypeStruct((16, CF), F32)],
        input_output_aliases={7: 0},
        scratch_shapes=_ffn_scratch(cb, 1) + [pltpu.VMEM((R + 2 * GRID_W + 2 * FPAD, cb), F32)],
        compiler_params=_cp(2), name=name)(
            dgc, dgc, dgc, up, up, up, w16, dup)


def mod_fwd(cs, w_mod, name):
    L, D, Ns = w_mod.shape
    tn = _lane_tile(Ns, 768)

    def body(c_ref, w_ref, o_ref):
        cv = c_ref[...]
        o_ref[...] = _dot((cv * _sig(cv)).astype(BF16), w_ref[...].astype(BF16))

    return pl.pallas_call(
        body, grid=(L, Ns // tn),
        in_specs=[pl.BlockSpec((16, D), lambda l, j: (0, 0)), pl.BlockSpec((None, D, tn), lambda l, j: (l, 0, j))],
        out_specs=pl.BlockSpec((None, 16, tn), lambda l, j: (l, 0, j)),
        out_shape=jax.ShapeDtypeStruct((L, 16, Ns), F32), compiler_params=_cp(2), name=name)(cs, w_mod)


def mod_bwd(cs, w_mod, dmod, name):
    L, D, Ns = w_mod.shape
    tn = _lane_tile(Ns, 768)

    def body(c_ref, w_ref, dm_ref, gw_ref, ds_ref):
        @pl.when(jnp.logical_and(pl.program_id(0) == 0, pl.program_id(1) == 0))
        def _():
            ds_ref[...] = jnp.zeros_like(ds_ref)

        cv = c_ref[...]
        dm = dm_ref[...].astype(BF16)
        gw_ref[...] = _dot_tn((cv * _sig(cv)).astype(BF16), dm)
        ds_ref[...] += _dot_nt(dm, w_ref[...].astype(BF16))

    return pl.pallas_call(
        body, grid=(L, Ns // tn),
        in_specs=[pl.BlockSpec((16, D), lambda l, j: (0, 0)), pl.BlockSpec((None, D, tn), lambda l, j: (l, 0, j)),
                  pl.BlockSpec((None, 16, tn), lambda l, j: (l, 0, j))],
        out_specs=[pl.BlockSpec((None, D, tn), lambda l, j: (l, 0, j)), pl.BlockSpec((16, D), lambda l, j: (0, 0))],
        out_shape=[jax.ShapeDtypeStruct((L, D, Ns), F32), jax.ShapeDtypeStruct((16, D), F32)],
        compiler_params=_cp(2), name=name)(cs, w_mod, dmod)


def cast_bf16(w, name):
    L, Kb, Nb = w.shape
    w2 = w.reshape(L * Kb, Nb)
    tr = _row_tile(L * Kb, Nb, 1 << 19)

    def body(w_ref, o_ref):
        o_ref[...] = w_ref[...].astype(BF16)

    spec = pl.BlockSpec((tr, Nb), lambda i: (i, 0))
    out = pl.pallas_call(body, grid=(L * Kb // tr,), in_specs=[spec], out_specs=spec,
                         out_shape=jax.ShapeDtypeStruct((L * Kb, Nb), BF16), compiler_params=_cp(1), name=name)(w2)
    return out.reshape(L, Kb, Nb)


def add_half(dw, recv, c_idx, name):
    S, Kb, Nb = dw.shape
    Kh = Kb // 2
    tr = _row_tile(Kh, Nb, 1 << 19)
    nb = Kh // tr

    def body(c_ref, a_ref, b_ref, o_ref):
        o_ref[...] = (a_ref[...].astype(F32) + b_ref[...].astype(F32)).astype(BF16)

    return pl.pallas_call(
        body,
        grid_spec=pltpu.PrefetchScalarGridSpec(
            num_scalar_prefetch=1, grid=(S, nb),
            in_specs=[pl.BlockSpec((None, tr, Nb), lambda s, i, c: (s, c[0] * nb + i, 0)),
                      pl.BlockSpec((None, tr, Nb), lambda s, i, c: (s, i, 0))],
            out_specs=pl.BlockSpec((None, tr, Nb), lambda s, i, c: (s, i, 0))),
        out_shape=jax.ShapeDtypeStruct((S, Kh, Nb), BF16), compiler_params=_cp(2), name=name)(c_idx, dw, recv)


def add_shards(mine, sib, c_idx, name):
    S, Kh, Nb = mine.shape
    tr = _row_tile(Kh, Nb, 1 << 18)
    nb = Kh // tr

    def body(c_ref, m_ref, s_ref, o_ref):
        def total(p_ref):
            acc = p_ref[0].astype(F32)
            for s in range(1, S):
                acc = acc + p_ref[s].astype(F32)
            o_ref[...] = acc

        pl.when(pl.program_id(0) == 0)(lambda: total(m_ref))
        pl.when(pl.program_id(0) == 1)(lambda: total(s_ref))

    out = pl.pallas_call(
        body,
        grid_spec=pltpu.PrefetchScalarGridSpec(
            num_scalar_prefetch=1, grid=(2, nb),
            in_specs=[pl.BlockSpec((S, tr, Nb), lambda h, i, c: (0, jnp.where(h == 0, i, nb - 1), 0)),
                      pl.BlockSpec((S, tr, Nb), lambda h, i, c: (0, jnp.where(h == 0, 0, i), 0))],
            out_specs=pl.BlockSpec((None, tr, Nb), lambda h, i, c: (jnp.where(h == 0, c[0], 1 - c[0]), i, 0))),
        out_shape=jax.ShapeDtypeStruct((2, Kh, Nb), F32), compiler_params=_cp(2), name=name)(c_idx, mine, sib)
    return out.reshape(2 * Kh, Nb)


def adamw(w, g, m, v, name):
    shape = w.shape
    cols = shape[-1]
    rows = w.size // cols
    w2, g2, m2, v2 = (t.reshape(rows, cols) for t in (w, g, m, v))
    tr = _row_tile(rows, cols, 3 << 17)

    def body(w_ref, g_ref, m_ref, v_ref, d_ref, nm_ref, nv_ref):
        gv = g_ref[...]
        nm = ADAM_B1 * m_ref[...] + (1.0 - ADAM_B1) * gv
        nv = ADAM_B2 * v_ref[...] + (1.0 - ADAM_B2) * jnp.square(gv)
        m_hat = nm / (1.0 - ADAM_B1 ** ADAM_STEP)
        v_hat = nv / (1.0 - ADAM_B2 ** ADAM_STEP)
        d_ref[...] = -ADAM_LR * (m_hat / (jnp.sqrt(v_hat) + ADAM_EPS) + ADAM_WD * w_ref[...])
        nm_ref[...] = nm
        nv_ref[...] = nv

    spec = pl.BlockSpec((tr, cols), lambda i: (i, 0))
    sds = jax.ShapeDtypeStruct((rows, cols), F32)
    d, nm, nv = pl.pallas_call(body, grid=(rows // tr,), in_specs=[spec] * 4, out_specs=[spec] * 3,
                               out_shape=[sds, sds, sds], compiler_params=_cp(1), name=name)(w2, g2, m2, v2)
    return d.reshape(shape), nm.reshape(shape), nv.reshape(shape)


def gather_sum(v, name):
    r, cols = v.shape

    def body(v_ref, g_ref, s_ref, send_sems, recv_sems, local_sem):
        x, y, c = _place()
        me = 4 * x + 2 * y + c
        mine = pltpu.make_async_copy(v_ref, g_ref.at[me], local_sem)
        mine.start()
        sends, peers = [], []
        for k in range(1, 8):
            px = 1 - x if k & 4 else x
            py = 1 - y if k & 2 else y
            pc = 1 - c if k & 1 else c
            cp = pltpu.make_async_remote_copy(src_ref=v_ref, dst_ref=g_ref.at[me], send_sem=send_sems.at[k - 1],
                                              recv_sem=recv_sems.at[k - 1], device_id=(px, py, pc), device_id_type=MESH)
            cp.start()
            sends.append(cp)
            peers.append((px, py, pc))
        for k, (px, py, pc) in enumerate(peers):
            pltpu.make_async_remote_copy(src_ref=v_ref, dst_ref=g_ref.at[4 * px + 2 * py + pc], send_sem=send_sems.at[k],
                                         recv_sem=recv_sems.at[k], device_id=(px, py, pc), device_id_type=MESH).wait_recv()
        for cp in sends:
            cp.wait_send()
        mine.wait()
        acc = g_ref[0]
        for d in range(1, 8):
            acc = acc + g_ref[d]
        s_ref[...] = acc

    vm = pl.BlockSpec(memory_space=pltpu.VMEM)
    return pl.pallas_call(
        body, in_specs=[vm], out_specs=[vm, vm],
        out_shape=[jax.ShapeDtypeStruct((8, r, cols), F32), jax.ShapeDtypeStruct((r, cols), F32)],
        scratch_shapes=[pltpu.SemaphoreType.DMA((7,)), pltpu.SemaphoreType.DMA((7,)), pltpu.SemaphoreType.DMA],
        compiler_params=_cp0(), name=name)(v)


def gather_weights(wb, name):
    L, Kb, Nb = wb.shape
    Kh = Kb // 2

    def body(w_ref, *rest):
        outs, (send_sems, recv_sems, local_sems) = rest[:L], rest[L:]
        x, y, c = _place()
        jm = 2 * x + y
        sib = (x, y, 1 - c)
        chips = _other_chips(x, y)

        def cp(l, t, src, dst, to):
            return pltpu.make_async_remote_copy(src_ref=src, dst_ref=dst, send_sem=send_sems.at[7 * l + t],
                                                recv_sem=recv_sems.at[7 * l + t], device_id=to, device_id_type=MESH)

        started, local = [], []
        for l in range(L):
            src = w_ref.at[l, pl.ds(c * Kh, Kh), :]
            dst = outs[l].at[jm, c]
            lc = pltpu.make_async_copy(src, dst, local_sems.at[l])
            lc.start()
            local.append(lc)
            for t, (px, py) in enumerate(chips):
                started.append(cp(l, t, src, dst, (px, py, c)))
            started.append(cp(l, 3, src, dst, sib))
            for s in started[-4:]:
                s.start()
        for l in range(L):
            for t, (px, py) in enumerate(chips):
                blk = outs[l].at[2 * px + py, c]
                cp(l, t, blk, blk, (px, py, c)).wait_recv()
                fwd = cp(l, 4 + t, blk, blk, sib)
                fwd.start()
                started.append(fwd)
        for l in range(L):
            blk = outs[l].at[jm, 1 - c]
            cp(l, 3, blk, blk, sib).wait_recv()
            for t, (px, py) in enumerate(chips):
                blk = outs[l].at[2 * px + py, 1 - c]
                cp(l, 4 + t, blk, blk, sib).wait_recv()
        for s in started:
            s.wait_send()
        for lc in local:
            lc.wait()

    hbm = pl.BlockSpec(memory_space=pl.ANY)
    outs = pl.pallas_call(
        body, in_specs=[hbm], out_specs=[hbm] * L,
        out_shape=[jax.ShapeDtypeStruct((4, 2, Kh, Nb), BF16)] * L,
        scratch_shapes=[pltpu.SemaphoreType.DMA((7 * L,)), pltpu.SemaphoreType.DMA((7 * L,)), pltpu.SemaphoreType.DMA((L,))],
        compiler_params=_cp0(), name=name)(wb)
    return [o.reshape(4, Kb, Nb) for o in outs]


def swap_halves(dw, name):
    S, Kb, Nb = dw.shape
    Kh = Kb // 2

    def body(d_ref, o_ref, send_sems, recv_sems):
        x, y, c = _place()
        sib = (x, y, 1 - c)
        cps = [pltpu.make_async_remote_copy(src_ref=d_ref.at[s, pl.ds((1 - c) * Kh, Kh), :], dst_ref=o_ref.at[s],
                                            send_sem=send_sems.at[s], recv_sem=recv_sems.at[s], device_id=sib,
                                            device_id_type=MESH) for s in range(S)]
        for cpy in cps:
            cpy.start()
        for cpy in cps:
            cpy.wait_recv()
        for cpy in cps:
            cpy.wait_send()

    hbm = pl.BlockSpec(memory_space=pl.ANY)
    return pl.pallas_call(
        body, in_specs=[hbm], out_specs=hbm, out_shape=jax.ShapeDtypeStruct((S, Kh, Nb), dw.dtype),
        scratch_shapes=[pltpu.SemaphoreType.DMA((S,)), pltpu.SemaphoreType.DMA((S,))],
        compiler_params=_cp0(), name=name)(dw)


def share_parts(parts, name):
    S = parts.shape[0]

    def body(p_ref, o_ref, send_sems, recv_sems):
        x, y, c = _place()
        cps = [pltpu.make_async_remote_copy(src_ref=p_ref.at[s], dst_ref=o_ref.at[s], send_sem=send_sems.at[s],
                                            recv_sem=recv_sems.at[s], device_id=(x, y, 1 - c), device_id_type=MESH)
               for s in range(S)]
        for cpy in cps:
            cpy.start()
        for cpy in cps:
            cpy.wait_recv()
        for cpy in cps:
            cpy.wait_send()

    hbm = pl.BlockSpec(memory_space=pl.ANY)
    return pl.pallas_call(
        body, in_specs=[hbm], out_specs=hbm, out_shape=jax.ShapeDtypeStruct(parts.shape, parts.dtype),
        scratch_shapes=[pltpu.SemaphoreType.DMA((S,)), pltpu.SemaphoreType.DMA((S,))],
        compiler_params=_cp0(), name=name)(parts)


def chip_sums(dw, c_idx, tag):
    return add_half(dw, swap_halves(dw, f"rs_swap_{tag}"), c_idx, f"rs_add_half_{tag}")


def finish_grad(parts, c_idx, tag):
    return add_shards(parts, share_parts(parts, f"rs_share_{tag}"), c_idx, f"rs_add_shards_{tag}")


def _pack(parts):
    flat = jnp.concatenate([t.reshape(-1).astype(F32) for t in parts])
    n = flat.shape[0]
    pad = (-n) % 1024
    return jnp.pad(flat, (0, pad)).reshape(-1, 128)


def _unpack(buf, shapes):
    flat = buf.reshape(buf.shape[:-2] + (-1,))
    out, o = [], 0
    for s in shapes:
        n = 1
        for d in s:
            n *= d
        out.append(flat[..., o:o + n].reshape(buf.shape[:-2] + tuple(s)))
        o += n
    return out


def _rope_tables(seq, ctx_len):
    t = jnp.arange(seq)
    inv = 1.0 / (ROPE_THETA ** (jnp.arange(0, DH // 4, dtype=F32) / (DH // 4)))
    ar = (t // GRID_W).astype(F32)[:, None] * inv[None, :]
    ac = (t % GRID_W).astype(F32)[:, None] * inv[None, :]
    cos = jnp.concatenate([jnp.cos(ar), jnp.cos(ar), jnp.cos(ac), jnp.cos(ac)], axis=-1)
    sin = jnp.concatenate([-jnp.sin(ar), jnp.sin(ar), -jnp.sin(ac), jnp.sin(ac)], axis=-1)
    return (jnp.concatenate([jnp.ones((ctx_len, DH), F32), cos], axis=0),
            jnp.concatenate([jnp.zeros((ctx_len, DH), F32), sin], axis=0))


def kernel(x, c, ctx, c_ctx, w_mod, b_mod, norm1_g, norm2_g, w_in, ret_decay_f, ret_decay_b, conv_dw_w, conv_dw_b, conv_ln_g, conv_ln_b, w_out, ffn_w_up, ffn_dw_w, ffn_dw_b, ffn_w_down, final_norm_g, loss_target, m_c_ctx, m_w_mod, m_b_mod, m_norm1_g, m_norm2_g, m_w_in, m_ret_decay_f, m_ret_decay_b, m_conv_dw_w, m_conv_dw_b, m_conv_ln_g, m_conv_ln_b, m_w_out, m_ffn_w_up, m_ffn_dw_w, m_ffn_dw_b, m_ffn_w_down, m_final_norm_g, v_c_ctx, v_w_mod, v_b_mod, v_norm1_g, v_norm2_g, v_w_in, v_ret_decay_f, v_ret_decay_b, v_conv_dw_w, v_conv_dw_b, v_conv_ln_g, v_conv_ln_b, v_w_out, v_ffn_w_up, v_ffn_dw_w, v_ffn_dw_b, v_ffn_w_down, v_final_norm_g):
    _, SEQ, D = x.shape
    CTX = ctx.shape[1]
    L = w_in.shape[0]
    CWs = conv_dw_w.shape[2]
    CW = 4 * CWs
    RW = 4 * w_out.shape[1] - CW
    H = RW // DH
    CFs = ffn_dw_w.shape[-1]
    CF = 4 * CFs
    NMs = w_mod.shape[2]
    T = CTX + SEQ
    ncr, ncc = CTX // R, CTX // RC
    assert CTX == R and RW == CW and RW % DH == 0 and SEQ % R == 0 and R % GRID_W == 0
    assert w_in.shape[2] * 4 == 4 * RW + 2 * CW and NMs * 4 == N_MOD * D

    mx, my, mc = _place()
    me = 4 * mx + 2 * my + mc
    jm = 2 * mx + my
    c_idx = jnp.reshape(mc, (1,)).astype(jnp.int32)

    shapes0 = [(D,), (L, CONV_K, CWs), (L, 9, CFs)]
    g0, _ = gather_sum(_pack([c[0], conv_dw_w, ffn_dw_w.reshape(L, 9, CFs)]), "gather_cond")
    c_all, cw_all, fw_all = _unpack(g0, shapes0)
    conv_w = jnp.concatenate([cw_all[2 * j] for j in range(4)], axis=-1)
    ffn_w = jnp.concatenate([fw_all[2 * j] for j in range(4)], axis=-1)
    conv_w32 = jnp.pad(conv_w, ((0, 0), (0, 32 - CONV_K), (0, 0)))
    ffn_w16 = jnp.pad(ffn_w, ((0, 0), (0, 7), (0, 0)))
    cs = jnp.concatenate([c_all, c_ctx[None, :], jnp.zeros((7, D), F32)], axis=0)
    mod_shard = mod_fwd(cs, w_mod, "mod_fwd")
    g1, _ = gather_sum(mod_shard.reshape(-1, 128), "gather_mod")
    mod_all = g1.reshape(8, L, 16, NMs)
    mod_full = jnp.concatenate([mod_all[2 * j] for j in range(4)], axis=-1) + b_mod[:, None, :]
    mod_mine = lax.dynamic_index_in_dim(mod_full, me, axis=1, keepdims=False)
    modv = jnp.stack([mod_full[:, 8], mod_mine], axis=1).reshape(L, 2, N_MOD, D)

    big = {"w_in": (w_in, True), "w_out": (w_out, False), "w_up": (ffn_w_up, True), "w_down": (ffn_w_down, False)}
    wb = {k: cast_bf16(w, f"cast_{k}") for k, (w, _) in big.items()}
    wg = {k: [gather_weights(wb[k][0:1], f"gather_{k}")[0]] for k in big}
    nxt = lambda k, l: wb[k][l + 1] if l + 1 < L else None

    def wmat(k, l):
        w = wg[k][l]
        return w if big[k][1] else w.reshape(1, w.shape[0] * w.shape[1], w.shape[2])

    cosT, sinT = _rope_tables(SEQ, CTX)
    lgt = [jnp.broadcast_to(jnp.concatenate([jax.nn.log_sigmoid(ret_decay_f[l]), jax.nn.log_sigmoid(ret_decay_b[l])])[:, None],
                            (2 * H, DH)) for l in range(L)]
    row = lambda t: t.reshape(1, -1)

    def project(a, k, l, out_dtype, name, res=None):
        y, xn, gathered = mm_nn(a, wmat(k, l), out_dtype, name, nxt(k, l), res)
        if gathered is not None:
            wg[k].append(gathered)
        return y, xn

    def back(a, dy, k, l, out_dtype, tag):
        dw = mm_tn(a, dy, 4 if big[k][1] else 1, f"mm_{tag}_dw")
        dw = dw if big[k][1] else dw.reshape(4, dw.shape[1] // 4, dw.shape[2])
        da, parts = mm_nt(dy, wmat(k, l), out_dtype, f"mm_{tag}_dx", chip_sums(dw, c_idx, k))
        return da, finish_grad(parts, c_idx, k)

    xs = jnp.concatenate([ctx[0], x[0]], axis=0)
    saved = []
    for l in range(L):
        h1 = norm_mod(xs, row(norm1_g[l]), modv[l], 0, 1, ncr, "norm_mod")
        p, _ = project(h1, "w_in", l, BF16, "mm_in")
        uc = conv31_fwd(p, conv_w32[l], row(conv_dw_b[l]), RW, CW, ncr, "conv31_fwd")
        o_f, o_b, stf, stb = ret_fwd(p, cosT, sinT, lgt[l], H, ncc, "ret_fwd")
        mix = mix_fwd(o_f, o_b, p, uc, row(conv_ln_g[l]), row(conv_ln_b[l]), H, "mix_fwd")
        y1, x2 = project(mix, "w_out", l, BF16, "mm_out", (xs, modv[l][:, 2], CTX))
        h2 = norm_mod(x2, row(norm2_g[l]), modv[l], 3, 4, ncr, "norm_mod")
        up, _ = project(h2, "w_up", l, BF16, "mm_up")
        act, gcs = ffn_act(up, ffn_w16[l], row(ffn_dw_b[l]), CF, ncr, "ffn_act")
        y2, x3 = project(act, "w_down", l, BF16, "mm_down", (x2, modv[l][:, 5], CTX))
        saved.append((xs, h1, p, uc, o_f, o_b, stf, stb, mix, y1, x2, h2, up, act, gcs, y2))
        xs = x3

    dx, acc_loss = loss_head(xs, loss_target[0], row(final_norm_g), ncr, "loss_head")
    loss = lax.psum(0.5 / D * jnp.sum(acc_loss[1]), ("x", "y", "c"))

    small, gbig = [None] * L, {k: [None] * L for k in big}
    for l in reversed(range(L)):
        x1, h1, p, uc, o_f, o_b, stf, stb, mix, y1, x2, h2, up, act, gcs, y2 = saved[l]
        dy2, ag2 = gate_res_bwd(dx, y2, modv[l], 5, ncr, "gate_res_bwd")
        dact, gbig["w_down"][l] = back(act, dy2, "w_down", l, BF16, "down")
        dgc, dup, accb = ffn_act_bwd1(up, gcs, dact, CF, "ffn_act_bwd1")
        dup, accfw = ffn_act_bwd2(dgc, up, dup, ffn_w16[l], CF, ncr, "ffn_act_bwd2")
        dh2, gbig["w_up"][l] = back(h2, dup, "w_up", l, F32, "up")
        dx2, an2 = norm_mod_bwd(x2, row(norm2_g[l]), modv[l], dh2, dx, 4, ncr, "norm_mod_bwd")
        dy1, ag1 = gate_res_bwd(dx2, y1, modv[l], 2, ncr, "gate_res_bwd")
        dmix, gbig["w_out"][l] = back(mix, dy1, "w_out", l, BF16, "out")
        do, dg, duc, accln = mix_bwd(dmix, o_f, o_b, p, uc, row(conv_ln_g[l]), row(conv_ln_b[l]), H, "mix_bwd")
        da, dbg, acccw = conv31_bwd(duc, p, conv_w32[l], RW, CW, ncr, "conv31_bwd")
        dqf, dkf, dvf, dqb, dkb, dvb, glg = ret_bwd(p, do, cosT, sinT, lgt[l], stf, stb, H, ncc, "ret_bwd")
        dp = assemble_dp(dqf, dqb, dkf, dkb, dvf, dvb, dg, da, dbg, "assemble_dp")
        dh1, gbig["w_in"][l] = back(h1, dp, "w_in", l, F32, "in")
        dx, an1 = norm_mod_bwd(x1, row(norm1_g[l]), modv[l], dh1, dx2, 1, ncr, "norm_mod_bwd")
        dmod = jnp.stack([jnp.stack([an1[0], an1[1], ag1[0], an2[0], an2[1], ag2[0]]),
                          jnp.stack([an1[2], an1[3], ag1[1], an2[2], an2[3], ag2[1]])])
        dlg = jnp.sum(glg, axis=(1, 2))
        dth = dlg * jnp.concatenate([jax.nn.sigmoid(-ret_decay_f[l]), jax.nn.sigmoid(-ret_decay_b[l])])
        small[l] = [dmod, an1[4], an2[4], acccw[31], accln[0], accln[1], acccw[:CONV_K], accfw[:9], accb[0], dth]
    grad_x = dx[CTX:][None]

    shapes1 = [(2, N_MOD, D), (D,), (D,), (CW,), (CW,), (CW,), (CONV_K, CW), (9, CF), (CF,), (2 * H,)]
    flat_parts = [t for l in range(L) for t in small[l]] + [acc_loss[0]]
    g2, s2 = gather_sum(_pack(flat_parts), "gather_small_grads")
    sums = _unpack(s2, shapes1 * L + [(D,)])
    per_dev = _unpack(g2, shapes1 * L + [(D,)])
    nS = len(shapes1)
    col = lambda i: jnp.stack([sums[l * nS + i] for l in range(L)])
    dmod_sum = col(0)
    dmod_dev = jnp.stack([per_dev[l * nS] for l in range(L)], axis=0)
    g_b_mod = (dmod_sum[:, 0] + dmod_sum[:, 1]).reshape(L, N_MOD * D)
    g_norm1, g_norm2 = col(1), col(2)
    g_conv_b, g_ln_g, g_ln_b = col(3), col(4), col(5)
    g_conv_w = lax.dynamic_slice_in_dim(col(6), jm * CWs, CWs, axis=2)
    g_ffn_w = lax.dynamic_slice_in_dim(col(7), jm * CFs, CFs, axis=2).reshape(L, 3, 3, CFs)
    g_ffn_b = col(8)
    g_ret = col(9)
    g_final = sums[-1]

    dmod_rows = jnp.concatenate([dmod_dev[:, :, 1].reshape(L, 8, N_MOD * D), dmod_sum[:, 0].reshape(L, 1, N_MOD * D),
                                 jnp.zeros((L, 7, N_MOD * D), F32)], axis=1)
    dmod_shard = lax.dynamic_slice_in_dim(dmod_rows, jm * NMs, NMs, axis=2)
    g_w_mod, ds_part = mod_bwd(cs, w_mod, dmod_shard, "mod_bwd")
    _, ds_sum = gather_sum(ds_part.reshape(-1, 128), "gather_dsilu")
    ds_ctx = 0.5 * ds_sum.reshape(16, D)[8]
    sg = jax.nn.sigmoid(c_ctx)
    g_c_ctx = ds_ctx * (sg * (1.0 + c_ctx * (1.0 - sg)))

    grads = {
        "c_ctx": g_c_ctx, "w_mod": g_w_mod, "b_mod": g_b_mod, "norm1_g": g_norm1, "norm2_g": g_norm2,
        "w_in": jnp.stack(gbig["w_in"]), "ret_decay_f": g_ret[:, :H], "ret_decay_b": g_ret[:, H:],
        "conv_dw_w": g_conv_w, "conv_dw_b": g_conv_b, "conv_ln_g": g_ln_g, "conv_ln_b": g_ln_b,
        "w_out": jnp.stack(gbig["w_out"]), "ffn_w_up": jnp.stack(gbig["w_up"]), "ffn_dw_w": g_ffn_w,
        "ffn_dw_b": g_ffn_b, "ffn_w_down": jnp.stack(gbig["w_down"]), "final_norm_g": g_final,
    }
    params = {
        "c_ctx": (c_ctx, m_c_ctx, v_c_ctx), "w_mod": (w_mod, m_w_mod, v_w_mod), "b_mod": (b_mod, m_b_mod, v_b_mod),
        "norm1_g": (norm1_g, m_norm1_g, v_norm1_g), "norm2_g": (norm2_g, m_norm2_g, v_norm2_g),
        "w_in": (w_in, m_w_in, v_w_in), "ret_decay_f": (ret_decay_f, m_ret_decay_f, v_ret_decay_f),
        "ret_decay_b": (ret_decay_b, m_ret_decay_b, v_ret_decay_b),
        "conv_dw_w": (conv_dw_w, m_conv_dw_w, v_conv_dw_w), "conv_dw_b": (conv_dw_b, m_conv_dw_b, v_conv_dw_b),
        "conv_ln_g": (conv_ln_g, m_conv_ln_g, v_conv_ln_g), "conv_ln_b": (conv_ln_b, m_conv_ln_b, v_conv_ln_b),
        "w_out": (w_out, m_w_out, v_w_out), "ffn_w_up": (ffn_w_up, m_ffn_w_up, v_ffn_w_up),
        "ffn_dw_w": (ffn_dw_w, m_ffn_dw_w, v_ffn_dw_w), "ffn_dw_b": (ffn_dw_b, m_ffn_dw_b, v_ffn_dw_b),
        "ffn_w_down": (ffn_w_down, m_ffn_w_down, v_ffn_w_down),
        "final_norm_g": (final_norm_g, m_final_norm_g, v_final_norm_g),
    }
    names = list(params)
    upd = {n: adamw(params[n][0], grads[n], params[n][1], params[n][2], f"adamw_{n}") for n in names}
    return (loss, grad_x, *[grads[n] for n in names], *[upd[n][0] for n in names],
            *[upd[n][1] for n in names], *[upd[n][2] for n in names])
```

```python
import jax
import jax.numpy as jnp
from jax import lax
from jax.experimental import pallas as pl
from jax.experimental.pallas import tpu as pltpu

F32 = jnp.float32
BF16 = jnp.bfloat16
EPS = 1e-6
DH = 128
RC = 128
GRID_W = 64
ROPE_THETA = 10000.0
N_MOD = 6
R = 256
HALO = 16
CONV_K = 31
SUB = 64
LANES = 256
FLANES = 128
FFN_COLS = 1408
VMEM_LIMIT = 48 * 1024 * 1024
MESH = pl.DeviceIdType.MESH
ADAM_LR, ADAM_B1, ADAM_B2, ADAM_EPS, ADAM_WD, ADAM_STEP = 0.001, 0.9, 0.999, 1e-08, 0.01, 10


def _cp(n):
    return pltpu.CompilerParams(dimension_semantics=("arbitrary",) * n, vmem_limit_bytes=VMEM_LIMIT)


def _cp0():
    return pltpu.CompilerParams(vmem_limit_bytes=VMEM_LIMIT)


def _sig(v):
    return 1.0 / (1.0 + jnp.exp(-v))


def _dot(a, b):
    return jnp.dot(a, b, preferred_element_type=F32)


def _dot_nt(a, b):
    return lax.dot_general(a, b, (((1,), (1,)), ((), ())), preferred_element_type=F32)


def _dot_tn(a, b):
    return lax.dot_general(a, b, (((0,), (0,)), ((), ())), preferred_element_type=F32)


def _lane_tile(n, cap):
    t = (min(n, cap) // 128) * 128
    while t >= 128:
        if n % t == 0:
            return t
        t -= 128
    raise ValueError(f"no lane tile for {n}")


def _row_tile(rows, cols, max_elems):
    if rows * cols <= max_elems:
        return rows
    t = (min(rows, max(8, max_elems // cols)) // 8) * 8
    while t >= 8:
        if rows % t == 0:
            return t
        t -= 8
    raise ValueError(f"no row tile for {rows}x{cols}")


def _place():
    return lax.axis_index("x"), lax.axis_index("y"), lax.axis_index("c")


def _other_chips(x, y):
    return [(1 - x, y), (x, 1 - y), (1 - x, 1 - y)]


def _first_last(grid):
    ids = [pl.program_id(k) for k in range(len(grid))]
    first, last = ids[0] == 0, ids[0] == grid[0] - 1
    for k in range(1, len(grid)):
        first = jnp.logical_and(first, ids[k] == 0)
        last = jnp.logical_and(last, ids[k] == grid[k] - 1)
    return first, last


def _side_copies(src_slab, dst_ref, send_sems, recv_sems, local_sem):
    x, y, c = _place()
    jm = 2 * x + y
    chips = _other_chips(x, y)
    mine = pltpu.make_async_copy(src_slab(jm), dst_ref.at[jm], local_sem)
    sends = [pltpu.make_async_remote_copy(src_ref=src_slab(2 * px + py), dst_ref=dst_ref.at[jm], send_sem=send_sems.at[t],
                                          recv_sem=recv_sems.at[t], device_id=(px, py, c), device_id_type=MESH)
             for t, (px, py) in enumerate(chips)]
    recvs = [pltpu.make_async_remote_copy(src_ref=dst_ref.at[2 * px + py], dst_ref=dst_ref.at[2 * px + py],
                                          send_sem=send_sems.at[t], recv_sem=recv_sems.at[t], device_id=(px, py, c),
                                          device_id_type=MESH) for t, (px, py) in enumerate(chips)]
    return mine, sends, recvs


def _side_start(mine, sends, recvs):
    mine.start()
    for s in sends:
        s.start()


def _side_finish(mine, sends, recvs):
    for r in recvs:
        r.wait_recv()
    for s in sends:
        s.wait_send()
    mine.wait()


_SIDE_SEMS = [pltpu.SemaphoreType.DMA((3,)), pltpu.SemaphoreType.DMA((3,)), pltpu.SemaphoreType.DMA]


def mm_nn(a, b, out_dtype, name, bcast=None, res=None):
    M, K = a.shape
    S, Kb, Nb = b.shape
    N = S * Nb
    assert K == Kb
    tm = _lane_tile(M, 768 if res is not None else 1408)
    tn, tk = _lane_tile(Nb, 1536), _lane_tile(K, 2048)
    nk = K // tk
    grid = (M // tm, N // tn, nk)
    tps = Nb // tn
    b_map = lambda i, j, k: (j // tps, k, j % tps)
    n_in = 2 + (2 if res is not None else 0) + (1 if bcast is not None else 0)
    n_out = 1 + (1 if res is not None else 0) + (1 if bcast is not None else 0)

    def body(*refs):
        ins, outs, scr = list(refs[:n_in]), list(refs[n_in:n_in + n_out]), list(refs[n_in + n_out:])
        a_ref, b_ref = ins[0], ins[1]
        o_ref = outs[0]
        if nk > 1:
            acc_ref = scr.pop(0)
        if bcast is not None:
            first, last = _first_last(grid)
            side = _side_copies(lambda j: ins[-1], outs[-1], *scr)
            pl.when(first)(lambda: _side_start(*side))
        k = pl.program_id(2)

        def finish(acc):
            o_ref[...] = acc.astype(o_ref.dtype)
            if res is not None:
                x_ref, g_ref = ins[2], ins[3]
                rows = pl.program_id(0) * tm + lax.broadcasted_iota(jnp.int32, (tm, 1), 0)
                gate = jnp.where(rows < res[2], g_ref[0:1, :], g_ref[1:2, :])
                outs[1][...] = x_ref[...] + gate * acc

        if nk == 1:
            finish(_dot(a_ref[...], b_ref[...]))
        else:
            @pl.when(k == 0)
            def _():
                acc_ref[...] = jnp.zeros_like(acc_ref)

            acc_ref[...] += _dot(a_ref[...], b_ref[...])
            pl.when(k == nk - 1)(lambda: finish(acc_ref[...]))

        if bcast is not None:
            pl.when(last)(lambda: _side_finish(*side))

    hbm = pl.BlockSpec(memory_space=pl.ANY)
    tile = pl.BlockSpec((tm, tn), lambda i, j, k: (i, j))
    in_specs = [pl.BlockSpec((tm, tk), lambda i, j, k: (i, k)), pl.BlockSpec((None, tk, tn), b_map)]
    out_specs = [tile]
    out_shape = [jax.ShapeDtypeStruct((M, N), out_dtype)]
    scratch = [pltpu.VMEM((tm, tn), F32)] if nk > 1 else []
    args = [a, b]
    if res is not None:
        in_specs += [tile, pl.BlockSpec((2, tn), lambda i, j, k: (0, j))]
        out_specs.append(tile)
        out_shape.append(jax.ShapeDtypeStruct((M, N), F32))
        args += [res[0], res[1]]
    if bcast is not None:
        in_specs.append(hbm)
        out_specs.append(hbm)
        out_shape.append(jax.ShapeDtypeStruct((4,) + bcast.shape, bcast.dtype))
        scratch += _SIDE_SEMS
        args.append(bcast)
    got = pl.pallas_call(body, grid=grid, in_specs=in_specs, out_specs=out_specs, out_shape=out_shape,
                         scratch_shapes=scratch, compiler_params=_cp(3), name=name)(*args)
    return got[0], (got[1] if res is not None else None), (got[-1] if bcast is not None else None)


def mm_nt(a, b, out_dtype, name, exch=None):
    M, N = a.shape
    S, K, Nb = b.shape
    assert N == S * Nb
    tm, tko, tnr = _lane_tile(M, 1408), _lane_tile(K, 1536), _lane_tile(Nb, 1536)
    nr = N // tnr
    grid = (M // tm, K // tko, nr)
    tps = Nb // tnr
    b_map = lambda i, j, r: (r // tps, j, r % tps)

    def body(a_ref, b_ref, *rest):
        if exch is None:
            o_ref, acc_ref = rest
        else:
            e_ref, o_ref, p_ref, acc_ref, send_sems, recv_sems, local_sem = rest
            first, last = _first_last(grid)
            side = _side_copies(lambda j: e_ref.at[j], p_ref, send_sems, recv_sems, local_sem)
            pl.when(first)(lambda: _side_start(*side))
        r = pl.program_id(2)

        @pl.when(r == 0)
        def _():
            acc_ref[...] = jnp.zeros_like(acc_ref)

        acc_ref[...] += _dot_nt(a_ref[...], b_ref[...])

        @pl.when(r == nr - 1)
        def _():
            o_ref[...] = acc_ref[...].astype(o_ref.dtype)

        if exch is not None:
            pl.when(last)(lambda: _side_finish(*side))

    hbm = pl.BlockSpec(memory_space=pl.ANY)
    in_specs = [pl.BlockSpec((tm, tnr), lambda i, j, r: (i, r)), pl.BlockSpec((None, tko, tnr), b_map)]
    out_specs = [pl.BlockSpec((tm, tko), lambda i, j, r: (i, j))]
    out_shape = [jax.ShapeDtypeStruct((M, K), out_dtype)]
    scratch = [pltpu.VMEM((tm, tko), F32)]
    args = [a, b]
    if exch is not None:
        in_specs.append(hbm)
        out_specs.append(hbm)
        out_shape.append(jax.ShapeDtypeStruct(exch.shape, exch.dtype))
        scratch += _SIDE_SEMS
        args.append(exch)
    res = pl.pallas_call(body, grid=grid, in_specs=in_specs, out_specs=out_specs, out_shape=out_shape,
                         scratch_shapes=scratch, compiler_params=_cp(3), name=name)(*args)
    return res[0] if exch is None else res


def mm_tn(a, c, S, name):
    M, K = a.shape
    N = c.shape[1]
    Kb, Nb = K, N // S
    tm, tk, tn = _lane_tile(M, 1408), _lane_tile(Kb, 1536), _lane_tile(Nb, 1536)
    nm = M // tm
    tps = Nb // tn
    o_map = lambda i, j, m: (j // tps, i, j % tps)

    def body(a_ref, c_ref, o_ref, acc_ref):
        m = pl.program_id(2)

        @pl.when(m == 0)
        def _():
            acc_ref[...] = jnp.zeros_like(acc_ref)

        acc_ref[...] += _dot_tn(a_ref[...], c_ref[...])

        @pl.when(m == nm - 1)
        def _():
            o_ref[...] = acc_ref[...].astype(BF16)

    return pl.pallas_call(
        body, grid=(K // tk, N // tn, nm),
        in_specs=[pl.BlockSpec((tm, tk), lambda i, j, m: (m, i)), pl.BlockSpec((tm, tn), lambda i, j, m: (m, j))],
        out_specs=pl.BlockSpec((None, tk, tn), o_map),
        out_shape=jax.ShapeDtypeStruct((S, Kb, Nb), BF16),
        scratch_shapes=[pltpu.VMEM((tk, tn), F32)], compiler_params=_cp(3), name=name)(a, c)


def _mod_spec(D, ncr):
    return pl.BlockSpec((None, N_MOD, D), lambda i: (jnp.where(i < ncr, 0, 1), 0, 0))


def norm_mod(x, g, modv, i_sh, i_sc, ncr, name):
    T, D = x.shape

    def body(x_ref, g_ref, m_ref, h_ref):
        xv = x_ref[...]
        r = lax.rsqrt(jnp.mean(xv * xv, axis=-1, keepdims=True) + EPS)
        n = xv * r * g_ref[...]
        h_ref[...] = (n * (1.0 + m_ref[i_sc:i_sc + 1, :]) + m_ref[i_sh:i_sh + 1, :]).astype(BF16)

    return pl.pallas_call(
        body, grid=(T // R,),
        in_specs=[pl.BlockSpec((R, D), lambda i: (i, 0)), pl.BlockSpec((1, D), lambda i: (0, 0)), _mod_spec(D, ncr)],
        out_specs=pl.BlockSpec((R, D), lambda i: (i, 0)),
        out_shape=jax.ShapeDtypeStruct((T, D), BF16), compiler_params=_cp(1), name=name)(x, g, modv)


def norm_mod_bwd(x, g, modv, dh, dres, i_sc, ncr, name):
    T, D = x.shape

    def body(x_ref, g_ref, m_ref, dh_ref, dr_ref, dx_ref, acc_ref):
        i = pl.program_id(0)

        @pl.when(i == 0)
        def _():
            acc_ref[...] = jnp.zeros_like(acc_ref)

        xv = x_ref[...]
        r = lax.rsqrt(jnp.mean(xv * xv, axis=-1, keepdims=True) + EPS)
        xh = xv * r
        gv = g_ref[...]
        dhv = dh_ref[...]
        dn = dhv * (1.0 + m_ref[i_sc:i_sc + 1, :])
        s_sh = jnp.sum(dhv, axis=0, keepdims=True)
        s_sc = jnp.sum(dhv * (xh * gv), axis=0, keepdims=True)
        acc_ref[4:5, :] += jnp.sum(dn * xh, axis=0, keepdims=True)
        dxh = dn * gv
        dx_ref[...] = dr_ref[...] + r * (dxh - xh * jnp.mean(dxh * xh, axis=-1, keepdims=True))

        @pl.when(i < ncr)
        def _():
            acc_ref[0:1, :] += s_sh
            acc_ref[1:2, :] += s_sc

        @pl.when(i >= ncr)
        def _():
            acc_ref[2:3, :] += s_sh
            acc_ref[3:4, :] += s_sc

    row = pl.BlockSpec((R, D), lambda i: (i, 0))
    return pl.pallas_call(
        body, grid=(T // R,),
        in_specs=[row, pl.BlockSpec((1, D), lambda i: (0, 0)), _mod_spec(D, ncr), row, row],
        out_specs=[row, pl.BlockSpec((8, D), lambda i: (0, 0))],
        out_shape=[jax.ShapeDtypeStruct((T, D), F32), jax.ShapeDtypeStruct((8, D), F32)],
        compiler_params=_cp(1), name=name)(x, g, modv, dh, dres)


def gate_res_bwd(dx, y, modv, i_g, ncr, name):
    T, D = dx.shape

    def body(dx_ref, y_ref, m_ref, dy_ref, acc_ref):
        i = pl.program_id(0)

        @pl.when(i == 0)
        def _():
            acc_ref[...] = jnp.zeros_like(acc_ref)

        dxv = dx_ref[...]
        dy_ref[...] = (m_ref[i_g:i_g + 1, :] * dxv).astype(BF16)
        s = jnp.sum(dxv * y_ref[...], axis=0, keepdims=True)

        @pl.when(i < ncr)
        def _():
            acc_ref[0:1, :] += s

        @pl.when(i >= ncr)
        def _():
            acc_ref[1:2, :] += s

    row = pl.BlockSpec((R, D), lambda i: (i, 0))
    return pl.pallas_call(
        body, grid=(T // R,), in_specs=[row, row, _mod_spec(D, ncr)],
        out_specs=[row, pl.BlockSpec((8, D), lambda i: (0, 0))],
        out_shape=[jax.ShapeDtypeStruct((T, D), BF16), jax.ShapeDtypeStruct((8, D), F32)],
        compiler_params=_cp(1), name=name)(dx, y, modv)


def loss_head(x, target, g, ncr, name):
    T, D = x.shape

    def body(x_ref, t_ref, g_ref, dx_ref, acc_ref):
        i = pl.program_id(0)

        @pl.when(i == 0)
        def _():
            acc_ref[...] = jnp.zeros_like(acc_ref)

        @pl.when(i < ncr)
        def _():
            dx_ref[...] = jnp.zeros_like(dx_ref)

        @pl.when(i >= ncr)
        def _():
            xv = x_ref[...]
            r = lax.rsqrt(jnp.mean(xv * xv, axis=-1, keepdims=True) + EPS)
            xh = xv * r
            gv = g_ref[...]
            e = xh * gv - t_ref[...]
            acc_ref[1:2, :] += jnp.sum(e * e, axis=0, keepdims=True)
            dy = e * (1.0 / D)
            acc_ref[0:1, :] += jnp.sum(dy * xh, axis=0, keepdims=True)
            dxh = dy * gv
            dx_ref[...] = r * (dxh - xh * jnp.mean(dxh * xh, axis=-1, keepdims=True))

    row = pl.BlockSpec((R, D), lambda i: (i, 0))
    return pl.pallas_call(
        body, grid=(T // R,),
        in_specs=[row, pl.BlockSpec((R, D), lambda i: (jnp.maximum(i - ncr, 0), 0)), pl.BlockSpec((1, D), lambda i: (0, 0))],
        out_specs=[row, pl.BlockSpec((8, D), lambda i: (0, 0))],
        out_shape=[jax.ShapeDtypeStruct((T, D), F32), jax.ShapeDtypeStruct((8, D), F32)],
        compiler_params=_cp(1), name=name)(x, target, g)


def _conv31_specs(T):
    nh = R // HALO
    pv = lambda i: jnp.maximum(i * nh - 1, 0)
    nx = lambda i: jnp.minimum((i + 1) * nh, T // HALO - 1)
    return pv, nx


def _shifted_copies(E):
    n = E.shape[1]
    for s in range(1, 8):
        E[s, 0:n - 8, :] = E[0, pl.ds(s, n - 8), :]


def _tap31(E, r0, c0, o):
    return E[o % 8, pl.ds(r0 + 8 * (o // 8), SUB), pl.ds(c0, LANES)]


def conv31_fwd(p, w32, b, RW, CW, ncr, name):
    T = p.shape[0]
    nT, cbk = T // R, _lane_tile(CW, 1024)
    n = R + 2 * HALO
    a0, g0 = 4 * RW // cbk, (4 * RW + CW) // cbk
    pv, nx = _conv31_specs(T)

    def body(a, g, ap, gp, an, gn, w, bb, uc, E):
        i = pl.program_id(0)
        has_prev = jnp.logical_and(i != 0, i != ncr)
        has_next = jnp.logical_and(i != ncr - 1, i != nT - 1)
        glu = lambda u, v: u.astype(F32) * _sig(v.astype(F32))
        E[0, 0:HALO, :] = jnp.where(has_prev, glu(ap[...], gp[...]), 0.0)
        E[0, HALO:HALO + R, :] = glu(a[...], g[...])
        E[0, HALO + R:, :] = jnp.where(has_next, glu(an[...], gn[...]), 0.0)
        _shifted_copies(E)
        for r0 in range(0, R, SUB):
            for c0 in range(0, cbk, LANES):
                cols = pl.ds(c0, LANES)
                acc = jnp.broadcast_to(bb[:, cols], (SUB, LANES))
                for k in range(CONV_K):
                    acc = acc + w[k:k + 1, cols] * _tap31(E, r0, c0, 1 + k)
                uc[r0:r0 + SUB, c0:c0 + LANES] = acc

    cur = lambda c0: pl.BlockSpec((R, cbk), lambda i, j: (i, c0 + j))
    hp = lambda c0: pl.BlockSpec((HALO, cbk), lambda i, j: (pv(i), c0 + j))
    hn = lambda c0: pl.BlockSpec((HALO, cbk), lambda i, j: (nx(i), c0 + j))
    return pl.pallas_call(
        body, grid=(nT, CW // cbk),
        in_specs=[cur(a0), cur(g0), hp(a0), hp(g0), hn(a0), hn(g0),
                  pl.BlockSpec((32, cbk), lambda i, j: (0, j)), pl.BlockSpec((1, cbk), lambda i, j: (0, j))],
        out_specs=pl.BlockSpec((R, cbk), lambda i, j: (i, j)),
        out_shape=jax.ShapeDtypeStruct((T, CW), F32),
        scratch_shapes=[pltpu.VMEM((8, n, cbk), F32)],
        compiler_params=_cp(2), name=name)(p, p, p, p, p, p, w32, b)


def conv31_bwd(duc, p, w32, RW, CW, ncr, name):
    T = p.shape[0]
    nT, cbk = T // R, _lane_tile(CW, 1024)
    n = R + 2 * HALO
    a0, g0 = 4 * RW // cbk, (4 * RW + CW) // cbk
    pv, nx = _conv31_specs(T)

    def body(d, dp_, dn_, a, g, ap, gp, an, gn, w, da, dg, accw, U, Dd):
        i = pl.program_id(1)

        @pl.when(i == 0)
        def _():
            accw[...] = jnp.zeros_like(accw)

        has_prev = jnp.logical_and(i != 0, i != ncr)
        has_next = jnp.logical_and(i != ncr - 1, i != nT - 1)
        glu = lambda u, v: u.astype(F32) * _sig(v.astype(F32))
        U[0, 0:HALO, :] = jnp.where(has_prev, glu(ap[...], gp[...]), 0.0)
        U[0, HALO:HALO + R, :] = glu(a[...], g[...])
        U[0, HALO + R:, :] = jnp.where(has_next, glu(an[...], gn[...]), 0.0)
        Dd[0, 0:HALO, :] = jnp.where(has_prev, dp_[...], 0.0)
        Dd[0, HALO:HALO + R, :] = d[...]
        Dd[0, HALO + R:, :] = jnp.where(has_next, dn_[...], 0.0)
        _shifted_copies(U)
        _shifted_copies(Dd)
        for r0 in range(0, R, SUB):
            for c0 in range(0, cbk, LANES):
                cols, blk = pl.ds(c0, LANES), (slice(r0, r0 + SUB), slice(c0, c0 + LANES))
                du = jnp.zeros((SUB, LANES), F32)
                for k in range(CONV_K):
                    du = du + w[k:k + 1, cols] * _tap31(Dd, r0, c0, HALO + 15 - k)
                av = a[blk].astype(F32)
                sg = _sig(g[blk].astype(F32))
                da[blk] = (du * sg).astype(BF16)
                dg[blk] = (du * av * sg * (1.0 - sg)).astype(BF16)
                dcur = d[blk]
                for k in range(CONV_K):
                    accw[k:k + 1, cols] += jnp.sum(dcur * _tap31(U, r0, c0, 1 + k), axis=0, keepdims=True)
                accw[31:32, cols] += jnp.sum(dcur, axis=0, keepdims=True)

    cur = lambda c0: pl.BlockSpec((R, cbk), lambda j, i: (i, c0 + j))
    hp = lambda c0: pl.BlockSpec((HALO, cbk), lambda j, i: (pv(i), c0 + j))
    hn = lambda c0: pl.BlockSpec((HALO, cbk), lambda j, i: (nx(i), c0 + j))
    out = pl.BlockSpec((R, cbk), lambda j, i: (i, j))
    ext = pltpu.VMEM((8, n, cbk), F32)
    return pl.pallas_call(
        body, grid=(CW // cbk, nT),
        in_specs=[cur(0), hp(0), hn(0), cur(a0), cur(g0), hp(a0), hp(g0), hn(a0), hn(g0),
                  pl.BlockSpec((32, cbk), lambda j, i: (0, j))],
        out_specs=[out, out, pl.BlockSpec((32, cbk), lambda j, i: (0, j))],
        out_shape=[jax.ShapeDtypeStruct((T, CW), BF16), jax.ShapeDtypeStruct((T, CW), BF16),
                   jax.ShapeDtypeStruct((32, CW), F32)],
        scratch_shapes=[ext, ext],
        compiler_params=_cp(2), name=name)(duc, duc, duc, p, p, p, p, p, p, w32)


def _rope(v, cosv, sinv, first):
    swapped = jnp.where(first, pltpu.roll(v, 96, 1), pltpu.roll(v, 32, 1))
    return v * cosv + swapped * sinv


def _unrope(v, cosv, sinv, first):
    z = v * sinv
    return v * cosv + jnp.where(first, pltpu.roll(z, 96, 1), pltpu.roll(z, 32, 1))


def _decay_tables(lg_ref, H, DM, QD, KD, CD):
    n = lax.broadcasted_iota(jnp.int32, (RC, DH), 0).astype(F32)
    m = lax.broadcasted_iota(jnp.int32, (RC, DH), 1).astype(F32)
    for d in range(2):
        for h in range(H):
            i = d * H + h
            lg = lg_ref[i:i + 1, :]
            diff = (n - m) if d == 0 else (m - n)
            DM[i] = jnp.where(diff >= 0, jnp.exp(lg * jnp.maximum(diff, 0.0)), 0.0)
            QD[i] = jnp.exp(lg * ((n + 1.0) if d == 0 else (RC - n)))
            KD[i] = jnp.exp(lg * ((RC - 1.0 - n) if d == 0 else n))
            CD[i] = jnp.exp(lg * float(RC)) + jnp.zeros((RC, DH), F32)


def _chunk_orders(NC, ncc):
    cf = lambda s: s
    cb = lambda s: jnp.where(s < ncc, ncc - 1 - s, NC - 1 - (s - ncc))
    return cf, cb


def ret_fwd(p, cosT, sinT, lgt, H, ncc, name):
    T = p.shape[0]
    RW, NC = H * DH, T // RC
    cf, cb = _chunk_orders(NC, ncc)
    scale = DH ** -0.5

    def body(qf, kf, vf, qb, kb, vb, cosf, sinf, cosb, sinb, lg_ref, of_ref, ob_ref, sf_ref, sb_ref, S, DM, QD, KD, CD):
        s = pl.program_id(0)

        @pl.when(s == 0)
        def _():
            S[...] = jnp.zeros_like(S)
            _decay_tables(lg_ref, H, DM, QD, KD, CD)

        first = (lax.broadcasted_iota(jnp.int32, (RC, DH), 1) % 64) < 32
        for d, (q_ref, k_ref, v_ref, c_ref, s_ref, o_ref, st_ref) in enumerate(
                ((qf, kf, vf, cosf, sinf, of_ref, sf_ref), (qb, kb, vb, cosb, sinb, ob_ref, sb_ref))):
            cosv, sinv = c_ref[...], s_ref[...]
            for h in range(H):
                hs, i = slice(h * DH, (h + 1) * DH), d * H + h
                q16 = _rope(q_ref[:, hs].astype(F32), cosv, sinv, first).astype(BF16)
                k = _rope(k_ref[:, hs].astype(F32), cosv, sinv, first) * scale
                k16 = k.astype(BF16)
                v = v_ref[:, hs]
                s_in = S[i]
                s16 = s_in.astype(BF16)
                st_ref[h] = s16
                sc = _dot_nt(q16, k16) * DM[i]
                o_ref[:, hs] = _dot(sc.astype(BF16), v) + _dot(q16, s16) * QD[i]
                S[i] = s_in * CD[i] + _dot_tn((k * KD[i]).astype(BF16), v)

    pspec = lambda col, cm: pl.BlockSpec((RC, RW), lambda s: (cm(s), col))
    tspec = lambda cm: pl.BlockSpec((RC, DH), lambda s: (cm(s), 0))
    ospec = lambda cm: pl.BlockSpec((RC, RW), lambda s: (cm(s), 0))
    stspec = pl.BlockSpec((None, H, DH, DH), lambda s: (s, 0, 0, 0))
    tab = pltpu.VMEM((2 * H, RC, DH), F32)
    return pl.pallas_call(
        body, grid=(NC,),
        in_specs=[pspec(0, cf), pspec(1, cf), pspec(2, cf), pspec(0, cb), pspec(1, cb), pspec(2, cb),
                  tspec(cf), tspec(cf), tspec(cb), tspec(cb), pl.BlockSpec((2 * H, DH), lambda s: (0, 0))],
        out_specs=[ospec(cf), ospec(cb), stspec, stspec],
        out_shape=[jax.ShapeDtypeStruct((T, RW), F32), jax.ShapeDtypeStruct((T, RW), F32),
                   jax.ShapeDtypeStruct((NC, H, DH, DH), BF16), jax.ShapeDtypeStruct((NC, H, DH, DH), BF16)],
        scratch_shapes=[tab, tab, tab, tab, tab], compiler_params=_cp(1), name=name)(
            p, p, p, p, p, p, cosT, sinT, cosT, sinT, lgt)


def ret_bwd(p, do, cosT, sinT, lgt, stf, stb, H, ncc, name):
    T = p.shape[0]
    RW, NC = H * DH, T // RC
    cf0, cb0 = _chunk_orders(NC, ncc)
    cf = lambda sp: cf0(NC - 1 - sp)
    cb = lambda sp: cb0(NC - 1 - sp)
    scale = DH ** -0.5

    def body(qf, kf, vf, qb, kb, vb, dof, dob, cosf, sinf, cosb, sinb, lg_ref, stf_ref, stb_ref,
             dqf, dkf, dvf, dqb, dkb, dvb, glg, dS, DM, QD, KD, CD):
        sp = pl.program_id(0)

        @pl.when(sp == 0)
        def _():
            dS[...] = jnp.zeros_like(dS)
            glg[...] = jnp.zeros_like(glg)
            _decay_tables(lg_ref, H, DM, QD, KD, CD)

        first = (lax.broadcasted_iota(jnp.int32, (RC, DH), 1) % 64) < 32
        n = lax.broadcasted_iota(jnp.int32, (RC, DH), 0).astype(F32)
        m = lax.broadcasted_iota(jnp.int32, (RC, DH), 1).astype(F32)
        for d, (q_ref, k_ref, v_ref, do_ref, c_ref, s_ref, st_ref, dq_ref, dk_ref, dv_ref) in enumerate(
                ((qf, kf, vf, dof, cosf, sinf, stf_ref, dqf, dkf, dvf),
                 (qb, kb, vb, dob, cosb, sinb, stb_ref, dqb, dkb, dvb))):
            cosv, sinv = c_ref[...], s_ref[...]
            diff = (n - m) if d == 0 else (m - n)
            posq = (n + 1.0) if d == 0 else (RC - n)
            posk = (RC - 1.0 - n) if d == 0 else n
            for h in range(H):
                hs, i = slice(h * DH, (h + 1) * DH), d * H + h
                q = _rope(q_ref[:, hs].astype(F32), cosv, sinv, first)
                k = _rope(k_ref[:, hs].astype(F32), cosv, sinv, first) * scale
                q16, k16 = q.astype(BF16), k.astype(BF16)
                v = v_ref[:, hs]
                s_in = st_ref[h]
                ds_out = dS[i]
                ds16 = ds_out.astype(BF16)
                do16 = do_ref[:, hs]
                doq = (do16.astype(F32) * QD[i]).astype(BF16)
                a = _dot_nt(q16, k16) * DM[i]
                da_raw = _dot_nt(do16, v)
                da16 = (da_raw * DM[i]).astype(BF16)
                dq_state = _dot_nt(doq, s_in)
                dk_state = _dot_nt(v, ds16) * KD[i]
                dqr = _dot(da16, k16) + dq_state
                dkr = _dot_tn(da16, q16) + dk_state
                dv_ref[:, hs] = _dot_tn(a.astype(BF16), do16) + _dot((k * KD[i]).astype(BF16), ds16)
                dS[i] = ds_out * CD[i] + _dot_tn(q16, doq)
                glg[i] += (da_raw * a * diff + posq * q * dq_state + posk * k * dk_state
                           + float(RC) * CD[i] * ds_out * s_in.astype(F32))
                dq_ref[:, hs] = _unrope(dqr, cosv, sinv, first)
                dk_ref[:, hs] = _unrope(dkr, cosv, sinv, first) * scale

    pspec = lambda col, cm: pl.BlockSpec((RC, RW), lambda s: (cm(s), col))
    tspec = lambda cm: pl.BlockSpec((RC, DH), lambda s: (cm(s), 0))
    ospec = lambda cm: pl.BlockSpec((RC, RW), lambda s: (cm(s), 0))
    stspec = pl.BlockSpec((None, H, DH, DH), lambda s: (NC - 1 - s, 0, 0, 0))
    tab = pltpu.VMEM((2 * H, RC, DH), F32)
    big = jax.ShapeDtypeStruct((T, RW), F32)
    return pl.pallas_call(
        body, grid=(NC,),
        in_specs=[pspec(0, cf), pspec(1, cf), pspec(2, cf), pspec(0, cb), pspec(1, cb), pspec(2, cb),
                  ospec(cf), ospec(cb), tspec(cf), tspec(cf), tspec(cb), tspec(cb),
                  pl.BlockSpec((2 * H, DH), lambda s: (0, 0)), stspec, stspec],
        out_specs=[ospec(cf), ospec(cf), ospec(cf), ospec(cb), ospec(cb), ospec(cb),
                   pl.BlockSpec((2 * H, RC, DH), lambda s: (0, 0, 0))],
        out_shape=[big, big, big, big, big, big, jax.ShapeDtypeStruct((2 * H, RC, DH), F32)],
        scratch_shapes=[tab, tab, tab, tab, tab], compiler_params=_cp(1), name=name)(
            p, p, p, p, p, p, do, do, cosT, sinT, cosT, sinT, lgt, stf, stb)


def mix_fwd(o_f, o_b, p, uc, lng, lnb, H, name):
    T, RW = o_f.shape
    CW = uc.shape[1]

    def body(of_ref, ob_ref, g_ref, uc_ref, lg_ref, lb_ref, out_ref):
        for h in range(H):
            hs = slice(h * DH, (h + 1) * DH)
            o = of_ref[:, hs] + ob_ref[:, hs]
            on = o * lax.rsqrt(jnp.mean(o * o, axis=-1, keepdims=True) + EPS)
            gv = g_ref[:, hs].astype(F32)
            out_ref[:, hs] = (gv * _sig(gv) * on).astype(BF16)
        u = uc_ref[...]
        mu = jnp.mean(u, axis=-1, keepdims=True)
        var = jnp.mean(jnp.square(u - mu), axis=-1, keepdims=True)
        z = (u - mu) * lax.rsqrt(var + EPS) * lg_ref[...] + lb_ref[...]
        out_ref[:, RW:] = (z * _sig(z)).astype(BF16)

    rw = pl.BlockSpec((R, RW), lambda i: (i, 0))
    vec = pl.BlockSpec((1, CW), lambda i: (0, 0))
    return pl.pallas_call(
        body, grid=(T // R,),
        in_specs=[rw, rw, pl.BlockSpec((R, RW), lambda i: (i, 3)), pl.BlockSpec((R, CW), lambda i: (i, 0)), vec, vec],
        out_specs=pl.BlockSpec((R, RW + CW), lambda i: (i, 0)),
        out_shape=jax.ShapeDtypeStruct((T, RW + CW), BF16), compiler_params=_cp(1), name=name)(o_f, o_b, p, uc, lng, lnb)


def mix_bwd(dmix, o_f, o_b, p, uc, lng, lnb, H, name):
    T, RW = o_f.shape
    CW = uc.shape[1]

    def body(dm_ref, of_ref, ob_ref, g_ref, uc_ref, lg_ref, lb_ref, do_ref, dg_ref, duc_ref, acc_ref):
        i = pl.program_id(0)

        @pl.when(i == 0)
        def _():
            acc_ref[...] = jnp.zeros_like(acc_ref)

        for h in range(H):
            hs = slice(h * DH, (h + 1) * DH)
            o = of_ref[:, hs] + ob_ref[:, hs]
            r = lax.rsqrt(jnp.mean(o * o, axis=-1, keepdims=True) + EPS)
            on = o * r
            gv = g_ref[:, hs].astype(F32)
            sg = _sig(gv)
            dmr = dm_ref[:, hs].astype(F32)
            dg_ref[:, hs] = (dmr * on * (sg * (1.0 + gv * (1.0 - sg)))).astype(BF16)
            don = dmr * (gv * sg)
            do_ref[:, hs] = (r * (don - on * jnp.mean(don * on, axis=-1, keepdims=True))).astype(BF16)
        u = uc_ref[...]
        mu = jnp.mean(u, axis=-1, keepdims=True)
        rs = lax.rsqrt(jnp.mean(jnp.square(u - mu), axis=-1, keepdims=True) + EPS)
        zh = (u - mu) * rs
        lg = lg_ref[...]
        z = zh * lg + lb_ref[...]
        sz = _sig(z)
        dz = dm_ref[:, RW:].astype(F32) * (sz * (1.0 + z * (1.0 - sz)))
        acc_ref[0:1, :] += jnp.sum(dz * zh, axis=0, keepdims=True)
        acc_ref[1:2, :] += jnp.sum(dz, axis=0, keepdims=True)
        dzh = dz * lg
        duc_ref[...] = rs * (dzh - jnp.mean(dzh, axis=-1, keepdims=True)
                             - zh * jnp.mean(dzh * zh, axis=-1, keepdims=True))

    rw = pl.BlockSpec((R, RW), lambda i: (i, 0))
    cw = pl.BlockSpec((R, CW), lambda i: (i, 0))
    vec = pl.BlockSpec((1, CW), lambda i: (0, 0))
    return pl.pallas_call(
        body, grid=(T // R,),
        in_specs=[pl.BlockSpec((R, RW + CW), lambda i: (i, 0)), rw, rw, pl.BlockSpec((R, RW), lambda i: (i, 3)), cw, vec, vec],
        out_specs=[rw, rw, cw, pl.BlockSpec((8, CW), lambda i: (0, 0))],
        out_shape=[jax.ShapeDtypeStruct((T, RW), BF16), jax.ShapeDtypeStruct((T, RW), BF16),
                   jax.ShapeDtypeStruct((T, CW), F32), jax.ShapeDtypeStruct((8, CW), F32)],
        compiler_params=_cp(1), name=name)(dmix, o_f, o_b, p, uc, lng, lnb)


def assemble_dp(dqf, dqb, dkf, dkb, dvf, dvb, dg, da, dbg, name):
    T, RW = dqf.shape
    CW = da.shape[1]

    def body(qf, qb, kf, kb, vf, vb, g, a, b, out):
        out[:, 0:RW] = (qf[...] + qb[...]).astype(BF16)
        out[:, RW:2 * RW] = (kf[...] + kb[...]).astype(BF16)
        out[:, 2 * RW:3 * RW] = (vf[...] + vb[...]).astype(BF16)
        out[:, 3 * RW:4 * RW] = g[...]
        out[:, 4 * RW:4 * RW + CW] = a[...]
        out[:, 4 * RW + CW:] = b[...]

    rw = pl.BlockSpec((R, RW), lambda i: (i, 0))
    cw = pl.BlockSpec((R, CW), lambda i: (i, 0))
    W = 4 * RW + 2 * CW
    return pl.pallas_call(
        body, grid=(T // R,), in_specs=[rw] * 7 + [cw, cw], out_specs=pl.BlockSpec((R, W), lambda i: (i, 0)),
        out_shape=jax.ShapeDtypeStruct((T, W), BF16), compiler_params=_cp(1), name=name)(
            dqf, dqb, dkf, dkb, dvf, dvb, dg, da, dbg)


FPAD = 8


def _fill_plain(plain, cur, prv, nxt, has_prev, has_next):
    n, cb = plain.shape[0] - 2 * FPAD, plain.shape[1]
    pv, nv = prv[...], nxt[...]
    plain[0:FPAD, :] = jnp.zeros((FPAD, cb), F32)
    plain[FPAD + n:, :] = jnp.zeros((FPAD, cb), F32)
    plain[FPAD:FPAD + GRID_W, :] = jnp.where(has_prev, pv, jnp.zeros_like(pv)).astype(F32)
    plain[FPAD + GRID_W:FPAD + GRID_W + R, :] = cur[...].astype(F32)
    plain[FPAD + GRID_W + R:FPAD + n, :] = jnp.where(has_next, nv, jnp.zeros_like(nv)).astype(F32)


def _fill_ext(bufs, cur, prv, nxt, has_prev, has_next, is_ctx):
    left, plain, right = bufs
    n, cb = left.shape
    _fill_plain(plain, cur, prv, nxt, has_prev, has_next)
    left[...] = plain[pl.ds(FPAD - 1, n), :]
    right[...] = plain[pl.ds(FPAD + 1, n), :]
    if not is_ctx:
        for r in range(0, n, GRID_W):
            left[r:r + 1, :] = jnp.zeros((1, cb), F32)
            right[r + GRID_W - 1:r + GRID_W, :] = jnp.zeros((1, cb), F32)


def _taps(is_ctx):
    return [(dr, dc) for dr in ((0,) if is_ctx else (-1, 0, 1)) for dc in (-1, 0, 1)]


def _tap_src(bufs, r0, c0, dr, dc):
    off = (FPAD if dc == 0 else 0) + GRID_W + r0 + GRID_W * dr
    return bufs[dc + 1][pl.ds(off, SUB), pl.ds(c0, FLANES)]


def _conv9(bufs, w, r0, c0, is_ctx, flip):
    acc = jnp.zeros((SUB, FLANES), F32)
    for dr, dc in _taps(is_ctx):
        widx = (dr + 1) * 3 + dc + 1
        src = _tap_src(bufs, r0, c0, -dr, -dc) if flip else _tap_src(bufs, r0, c0, dr, dc)
        acc = acc + w[widx:widx + 1, pl.ds(c0, FLANES)] * src
    return acc


def _ffn_specs(T, cb, order):
    nq = R // GRID_W
    pv = lambda i: jnp.maximum(i * nq - 1, 0)
    nx = lambda i: jnp.minimum((i + 1) * nq, T // GRID_W - 1)
    if order == 'ij':
        mk = lambda blk, rf, c0: pl.BlockSpec(blk, lambda i, j: (rf(i), c0 + j))
    else:
        mk = lambda blk, rf, c0: pl.BlockSpec(blk, lambda j, i: (rf(i), c0 + j))
    cur = lambda c0: mk((R, cb), lambda i: i, c0)
    hp = lambda c0: mk((GRID_W, cb), pv, c0)
    hn = lambda c0: mk((GRID_W, cb), nx, c0)
    vec = lambda rows: mk((rows, cb), lambda i: 0, 0)
    return cur, hp, hn, vec


def _ffn_flags(i, ncr, nT):
    return i > ncr, jnp.logical_and(i >= ncr, i != nT - 1)


def _ffn_scratch(cb, sets):
    n = R + 2 * GRID_W
    return [pltpu.VMEM((n, cb), F32), pltpu.VMEM((n + 2 * FPAD, cb), F32), pltpu.VMEM((n, cb), F32)] * sets


def ffn_act(up, w16, b, CF, ncr, name):
    T = up.shape[0]
    nT, cb = T // R, _lane_tile(CF, FFN_COLS)
    ncb = CF // cb
    cur, hp, hn, vec = _ffn_specs(T, cb, 'ij')

    def body(g, v, gp, gn, w, bb, out, gc_out, e0, e1, e2):
        i = pl.program_id(0)
        has_prev, has_next = _ffn_flags(i, ncr, nT)
        bufs = (e0, e1, e2)

        def run(is_ctx):
            _fill_ext(bufs, g, gp, gn, has_prev, has_next, is_ctx)
            for r0 in range(0, R, SUB):
                for c0 in range(0, cb, FLANES):
                    gc = _conv9(bufs, w, r0, c0, is_ctx, False) + bb[:, pl.ds(c0, FLANES)]
                    val = v[r0:r0 + SUB, c0:c0 + FLANES].astype(F32)
                    out[r0:r0 + SUB, c0:c0 + FLANES] = (gc * _sig(gc) * val).astype(BF16)
                    gc_out[r0:r0 + SUB, c0:c0 + FLANES] = gc.astype(BF16)

        pl.when(i < ncr)(lambda: run(True))
        pl.when(i >= ncr)(lambda: run(False))

    sds = jax.ShapeDtypeStruct((T, CF), BF16)
    return pl.pallas_call(
        body, grid=(nT, ncb), in_specs=[cur(0), cur(ncb), hp(0), hn(0), vec(16), vec(1)], out_specs=[cur(0), cur(0)],
        out_shape=[sds, sds],
        scratch_shapes=_ffn_scratch(cb, 1), compiler_params=_cp(2), name=name)(up, up, up, up, w16, b)


def ffn_act_bwd1(up, gcs, dact, CF, name):
    T = up.shape[0]
    cb = _lane_tile(CF, FFN_COLS)
    ncb = CF // cb

    def body(v, gc_ref, da, dgc, dup, accb):
        @pl.when(pl.program_id(1) == 0)
        def _():
            accb[...] = jnp.zeros_like(accb)

        for r0 in range(0, R, SUB):
            for c0 in range(0, cb, FLANES):
                blk = (slice(r0, r0 + SUB), slice(c0, c0 + FLANES))
                gc = gc_ref[blk].astype(F32)
                sg = _sig(gc)
                dav = da[blk].astype(F32)
                dup[blk] = (dav * gc * sg).astype(BF16)
                d = dav * v[blk].astype(F32) * (sg * (1.0 + gc * (1.0 - sg)))
                dgc[blk] = d.astype(BF16)
                accb[0:1, c0:c0 + FLANES] += jnp.sum(d, axis=0, keepdims=True)

    blk = lambda c0: pl.BlockSpec((R, cb), lambda j, i: (i, c0 + j))
    return pl.pallas_call(
        body, grid=(ncb, T // R), in_specs=[blk(ncb), blk(0), blk(0)],
        out_specs=[blk(0), blk(ncb), pl.BlockSpec((8, cb), lambda j, i: (0, j))],
        out_shape=[jax.ShapeDtypeStruct((T, CF), BF16), jax.ShapeDtypeStruct((T, 2 * CF), BF16),
                   jax.ShapeDtypeStruct((8, CF), F32)],
        compiler_params=_cp(2), name=name)(up, gcs, dact)


def ffn_act_bwd2(dgc, up, dup, w16, CF, ncr, name):
    T = up.shape[0]
    nT, cb = T // R, _lane_tile(CF, FFN_COLS)
    ncb = CF // cb
    cur, hp, hn, vec = _ffn_specs(T, cb, 'ji')

    def body(d, dp_, dn_, g, gp, gn, w, dup_in, dgate, accw, d0, d1, d2, gplain):
        i = pl.program_id(1)

        @pl.when(i == 0)
        def _():
            accw[...] = jnp.zeros_like(accw)

        has_prev, has_next = _ffn_flags(i, ncr, nT)
        dbufs = (d0, d1, d2)
        _fill_plain(gplain, g, gp, gn, has_prev, has_next)

        def run(is_ctx):
            _fill_ext(dbufs, d, dp_, dn_, has_prev, has_next, is_ctx)
            for r0 in range(0, R, SUB):
                for c0 in range(0, cb, FLANES):
                    dgate[r0:r0 + SUB, c0:c0 + FLANES] = _conv9(dbufs, w, r0, c0, is_ctx, True).astype(BF16)
                    for dr, dc in _taps(is_ctx):
                        widx = (dr + 1) * 3 + dc + 1
                        dmov = _tap_src(dbufs, r0, c0, 0, -dc)
                        gsrc = gplain[pl.ds(FPAD + GRID_W + r0 + GRID_W * dr, SUB), pl.ds(c0, FLANES)]
                        accw[widx:widx + 1, pl.ds(c0, FLANES)] += jnp.sum(dmov * gsrc, axis=0, keepdims=True)

        pl.when(i < ncr)(lambda: run(True))
        pl.when(i >= ncr)(lambda: run(False))

    return pl.pallas_call(
        body, grid=(ncb, nT),
        in_specs=[cur(0), hp(0), hn(0), cur(0), hp(0), hn(0), vec(16), pl.BlockSpec(memory_space=pl.ANY)],
        out_specs=[cur(0), vec(16)],
        out_shape=[jax.ShapeDtypeStruct((T, 2 * CF), BF16), jax.ShapeDtypeStruct((16, CF), F32)],
        input_output_aliases={7: 0},
        scratch_shapes=_ffn_scratch(cb, 1) + [pltpu.VMEM((R + 2 * GRID_W + 2 * FPAD, cb), F32)],
        compiler_params=_cp(2), name=name)(
            dgc, dgc, dgc, up, up, up, w16, dup)


def mod_fwd(cs, w_mod, name):
    L, D, Ns = w_mod.shape
    tn = _lane_tile(Ns, 768)

    def body(c_ref, w_ref, o_ref):
        cv = c_ref[...]
        o_ref[...] = _dot((cv * _sig(cv)).astype(BF16), w_ref[...].astype(BF16))

    return pl.pallas_call(
        body, grid=(L, Ns // tn),
        in_specs=[pl.BlockSpec((16, D), lambda l, j: (0, 0)), pl.BlockSpec((None, D, tn), lambda l, j: (l, 0, j))],
        out_specs=pl.BlockSpec((None, 16, tn), lambda l, j: (l, 0, j)),
        out_shape=jax.ShapeDtypeStruct((L, 16, Ns), F32), compiler_params=_cp(2), name=name)(cs, w_mod)


def mod_bwd(cs, w_mod, dmod, name):
    L, D, Ns = w_mod.shape
    tn = _lane_tile(Ns, 768)

    def body(c_ref, w_ref, dm_ref, gw_ref, ds_ref):
        @pl.when(jnp.logical_and(pl.program_id(0) == 0, pl.program_id(1) == 0))
        def _():
            ds_ref[...] = jnp.zeros_like(ds_ref)

        cv = c_ref[...]
        dm = dm_ref[...].astype(BF16)
        gw_ref[...] = _dot_tn((cv * _sig(cv)).astype(BF16), dm)
        ds_ref[...] += _dot_nt(dm, w_ref[...].astype(BF16))

    return pl.pallas_call(
        body, grid=(L, Ns // tn),
        in_specs=[pl.BlockSpec((16, D), lambda l, j: (0, 0)), pl.BlockSpec((None, D, tn), lambda l, j: (l, 0, j)),
                  pl.BlockSpec((None, 16, tn), lambda l, j: (l, 0, j))],
        out_specs=[pl.BlockSpec((None, D, tn), lambda l, j: (l, 0, j)), pl.BlockSpec((16, D), lambda l, j: (0, 0))],
        out_shape=[jax.ShapeDtypeStruct((L, D, Ns), F32), jax.ShapeDtypeStruct((16, D), F32)],
        compiler_params=_cp(2), name=name)(cs, w_mod, dmod)


def cast_bf16(w, name):
    L, Kb, Nb = w.shape
    w2 = w.reshape(L * Kb, Nb)
    tr = _row_tile(L * Kb, Nb, 1 << 19)

    def body(w_ref, o_ref):
        o_ref[...] = w_ref[...].astype(BF16)

    spec = pl.BlockSpec((tr, Nb), lambda i: (i, 0))
    out = pl.pallas_call(body, grid=(L * Kb // tr,), in_specs=[spec], out_specs=spec,
                         out_shape=jax.ShapeDtypeStruct((L * Kb, Nb), BF16), compiler_params=_cp(1), name=name)(w2)
    return out.reshape(L, Kb, Nb)


def add_half(dw, recv, c_idx, name):
    S, Kb, Nb = dw.shape
    Kh = Kb // 2
    tr = _row_tile(Kh, Nb, 1 << 19)
    nb = Kh // tr

    def body(c_ref, a_ref, b_ref, o_ref):
        o_ref[...] = (a_ref[...].astype(F32) + b_ref[...].astype(F32)).astype(BF16)

    return pl.pallas_call(
        body,
        grid_spec=pltpu.PrefetchScalarGridSpec(
            num_scalar_prefetch=1, grid=(S, nb),
            in_specs=[pl.BlockSpec((None, tr, Nb), lambda s, i, c: (s, c[0] * nb + i, 0)),
                      pl.BlockSpec((None, tr, Nb), lambda s, i, c: (s, i, 0))],
            out_specs=pl.BlockSpec((None, tr, Nb), lambda s, i, c: (s, i, 0))),
        out_shape=jax.ShapeDtypeStruct((S, Kh, Nb), BF16), compiler_params=_cp(2), name=name)(c_idx, dw, recv)


def add_shards(mine, sib, c_idx, name):
    S, Kh, Nb = mine.shape
    tr = _row_tile(Kh, Nb, 1 << 18)
    nb = Kh // tr

    def body(c_ref, m_ref, s_ref, o_ref):
        def total(p_ref):
            acc = p_ref[0].astype(F32)
            for s in range(1, S):
                acc = acc + p_ref[s].astype(F32)
            o_ref[...] = acc

        pl.when(pl.program_id(0) == 0)(lambda: total(m_ref))
        pl.when(pl.program_id(0) == 1)(lambda: total(s_ref))

    out = pl.pallas_call(
        body,
        grid_spec=pltpu.PrefetchScalarGridSpec(
            num_scalar_prefetch=1, grid=(2, nb),
            in_specs=[pl.BlockSpec((S, tr, Nb), lambda h, i, c: (0, jnp.where(h == 0, i, nb - 1), 0)),
                      pl.BlockSpec((S, tr, Nb), lambda h, i, c: (0, jnp.where(h == 0, 0, i), 0))],
            out_specs=pl.BlockSpec((None, tr, Nb), lambda h, i, c: (jnp.where(h == 0, c[0], 1 - c[0]), i, 0))),
        out_shape=jax.ShapeDtypeStruct((2, Kh, Nb), F32), compiler_params=_cp(2), name=name)(c_idx, mine, sib)
    return out.reshape(2 * Kh, Nb)


def adamw(w, g, m, v, name):
    shape = w.shape
    cols = shape[-1]
    rows = w.size // cols
    w2, g2, m2, v2 = (t.reshape(rows, cols) for t in (w, g, m, v))
    tr = _row_tile(rows, cols, 3 << 17)

    def body(w_ref, g_ref, m_ref, v_ref, d_ref, nm_ref, nv_ref):
        gv = g_ref[...]
        nm = ADAM_B1 * m_ref[...] + (1.0 - ADAM_B1) * gv
        nv = ADAM_B2 * v_ref[...] + (1.0 - ADAM_B2) * jnp.square(gv)
        m_hat = nm / (1.0 - ADAM_B1 ** ADAM_STEP)
        v_hat = nv / (1.0 - ADAM_B2 ** ADAM_STEP)
        d_ref[...] = -ADAM_LR * (m_hat / (jnp.sqrt(v_hat) + ADAM_EPS) + ADAM_WD * w_ref[...])
        nm_ref[...] = nm
        nv_ref[...] = nv

    spec = pl.BlockSpec((tr, cols), lambda i: (i, 0))
    sds = jax.ShapeDtypeStruct((rows, cols), F32)
    d, nm, nv = pl.pallas_call(body, grid=(rows // tr,), in_specs=[spec] * 4, out_specs=[spec] * 3,
                               out_shape=[sds, sds, sds], compiler_params=_cp(1), name=name)(w2, g2, m2, v2)
    return d.reshape(shape), nm.reshape(shape), nv.reshape(shape)


def gather_sum(v, name):
    r, cols = v.shape

    def body(v_ref, g_ref, s_ref, send_sems, recv_sems, local_sem):
        x, y, c = _place()
        me = 4 * x + 2 * y + c
        mine = pltpu.make_async_copy(v_ref, g_ref.at[me], local_sem)
        mine.start()
        sends, peers = [], []
        for k in range(1, 8):
            px = 1 - x if k & 4 else x
            py = 1 - y if k & 2 else y
            pc = 1 - c if k & 1 else c
            cp = pltpu.make_async_remote_copy(src_ref=v_ref, dst_ref=g_ref.at[me], send_sem=send_sems.at[k - 1],
                                              recv_sem=recv_sems.at[k - 1], device_id=(px, py, pc), device_id_type=MESH)
            cp.start()
            sends.append(cp)
            peers.append((px, py, pc))
        for k, (px, py, pc) in enumerate(peers):
            pltpu.make_async_remote_copy(src_ref=v_ref, dst_ref=g_ref.at[4 * px + 2 * py + pc], send_sem=send_sems.at[k],
                                         recv_sem=recv_sems.at[k], device_id=(px, py, pc), device_id_type=MESH).wait_recv()
        for cp in sends:
            cp.wait_send()
        mine.wait()
        acc = g_ref[0]
        for d in range(1, 8):
            acc = acc + g_ref[d]
        s_ref[...] = acc

    vm = pl.BlockSpec(memory_space=pltpu.VMEM)
    return pl.pallas_call(
        body, in_specs=[vm], out_specs=[vm, vm],
        out_shape=[jax.ShapeDtypeStruct((8, r, cols), F32), jax.ShapeDtypeStruct((r, cols), F32)],
        scratch_shapes=[pltpu.SemaphoreType.DMA((7,)), pltpu.SemaphoreType.DMA((7,)), pltpu.SemaphoreType.DMA],
        compiler_params=_cp0(), name=name)(v)


def gather_weights(wb, name):
    L, Kb, Nb = wb.shape
    Kh = Kb // 2

    def body(w_ref, *rest):
        outs, (send_sems, recv_sems, local_sems) = rest[:L], rest[L:]
        x, y, c = _place()
        jm = 2 * x + y
        sib = (x, y, 1 - c)
        chips = _other_chips(x, y)

        def cp(l, t, src, dst, to):
            return pltpu.make_async_remote_copy(src_ref=src, dst_ref=dst, send_sem=send_sems.at[7 * l + t],
                                                recv_sem=recv_sems.at[7 * l + t], device_id=to, device_id_type=MESH)

        started, local = [], []
        for l in range(L):
            src = w_ref.at[l, pl.ds(c * Kh, Kh), :]
            dst = outs[l].at[jm, c]
            lc = pltpu.make_async_copy(src, dst, local_sems.at[l])
            lc.start()
            local.append(lc)
            for t, (px, py) in enumerate(chips):
                started.append(cp(l, t, src, dst, (px, py, c)))
            started.append(cp(l, 3, src, dst, sib))
            for s in started[-4:]:
                s.start()
        for l in range(L):
            for t, (px, py) in enumerate(chips):
                blk = outs[l].at[2 * px + py, c]
                cp(l, t, blk, blk, (px, py, c)).wait_recv()
                fwd = cp(l, 4 + t, blk, blk, sib)
                fwd.start()
                started.append(fwd)
        for l in range(L):
            blk = outs[l].at[jm, 1 - c]
            cp(l, 3, blk, blk, sib).wait_recv()
            for t, (px, py) in enumerate(chips):
                blk = outs[l].at[2 * px + py, 1 - c]
                cp(l, 4 + t, blk, blk, sib).wait_recv()
        for s in started:
            s.wait_send()
        for lc in local:
            lc.wait()

    hbm = pl.BlockSpec(memory_space=pl.ANY)
    outs = pl.pallas_call(
        body, in_specs=[hbm], out_specs=[hbm] * L,
        out_shape=[jax.ShapeDtypeStruct((4, 2, Kh, Nb), BF16)] * L,
        scratch_shapes=[pltpu.SemaphoreType.DMA((7 * L,)), pltpu.SemaphoreType.DMA((7 * L,)), pltpu.SemaphoreType.DMA((L,))],
        compiler_params=_cp0(), name=name)(wb)
    return [o.reshape(4, Kb, Nb) for o in outs]


def swap_halves(dw, name):
    S, Kb, Nb = dw.shape
    Kh = Kb // 2

    def body(d_ref, o_ref, send_sems, recv_sems):
        x, y, c = _place()
        sib = (x, y, 1 - c)
        cps = [pltpu.make_async_remote_copy(src_ref=d_ref.at[s, pl.ds((1 - c) * Kh, Kh), :], dst_ref=o_ref.at[s],
                                            send_sem=send_sems.at[s], recv_sem=recv_sems.at[s], device_id=sib,
                                            device_id_type=MESH) for s in range(S)]
        for cpy in cps:
            cpy.start()
        for cpy in cps:
            cpy.wait_recv()
        for cpy in cps:
            cpy.wait_send()

    hbm = pl.BlockSpec(memory_space=pl.ANY)
    return pl.pallas_call(
        body, in_specs=[hbm], out_specs=hbm, out_shape=jax.ShapeDtypeStruct((S, Kh, Nb), dw.dtype),
        scratch_shapes=[pltpu.SemaphoreType.DMA((S,)), pltpu.SemaphoreType.DMA((S,))],
        compiler_params=_cp0(), name=name)(dw)


def share_parts(parts, name):
    S = parts.shape[0]

    def body(p_ref, o_ref, send_sems, recv_sems):
        x, y, c = _place()
        cps = [pltpu.make_async_remote_copy(src_ref=p_ref.at[s], dst_ref=o_ref.at[s], send_sem=send_sems.at[s],
                                            recv_sem=recv_sems.at[s], device_id=(x, y, 1 - c), device_id_type=MESH)
               for s in range(S)]
        for cpy in cps:
            cpy.start()
        for cpy in cps:
            cpy.wait_recv()
        for cpy in cps:
            cpy.wait_send()

    hbm = pl.BlockSpec(memory_space=pl.ANY)
    return pl.pallas_call(
        body, in_specs=[hbm], out_specs=hbm, out_shape=jax.ShapeDtypeStruct(parts.shape, parts.dtype),
        scratch_shapes=[pltpu.SemaphoreType.DMA((S,)), pltpu.SemaphoreType.DMA((S,))],
        compiler_params=_cp0(), name=name)(parts)


def chip_sums(dw, c_idx, tag):
    return add_half(dw, swap_halves(dw, f"rs_swap_{tag}"), c_idx, f"rs_add_half_{tag}")


def finish_grad(parts, c_idx, tag):
    return add_shards(parts, share_parts(parts, f"rs_share_{tag}"), c_idx, f"rs_add_shards_{tag}")


def _pack(parts):
    flat = jnp.concatenate([t.reshape(-1).astype(F32) for t in parts])
    n = flat.shape[0]
    pad = (-n) % 1024
    return jnp.pad(flat, (0, pad)).reshape(-1, 128)


def _unpack(buf, shapes):
    flat = buf.reshape(buf.shape[:-2] + (-1,))
    out, o = [], 0
    for s in shapes:
        n = 1
        for d in s:
            n *= d
        out.append(flat[..., o:o + n].reshape(buf.shape[:-2] + tuple(s)))
        o += n
    return out


def _rope_tables(seq, ctx_len):
    t = jnp.arange(seq)
    inv = 1.0 / (ROPE_THETA ** (jnp.arange(0, DH // 4, dtype=F32) / (DH // 4)))
    ar = (t // GRID_W).astype(F32)[:, None] * inv[None, :]
    ac = (t % GRID_W).astype(F32)[:, None] * inv[None, :]
    cos = jnp.concatenate([jnp.cos(ar), jnp.cos(ar), jnp.cos(ac), jnp.cos(ac)], axis=-1)
    sin = jnp.concatenate([-jnp.sin(ar), jnp.sin(ar), -jnp.sin(ac), jnp.sin(ac)], axis=-1)
    return (jnp.concatenate([jnp.ones((ctx_len, DH), F32), cos], axis=0),
            jnp.concatenate([jnp.zeros((ctx_len, DH), F32), sin], axis=0))


def kernel(x, c, ctx, c_ctx, w_mod, b_mod, norm1_g, norm2_g, w_in, ret_decay_f, ret_decay_b, conv_dw_w, conv_dw_b, conv_ln_g, conv_ln_b, w_out, ffn_w_up, ffn_dw_w, ffn_dw_b, ffn_w_down, final_norm_g, loss_target, m_c_ctx, m_w_mod, m_b_mod, m_norm1_g, m_norm2_g, m_w_in, m_ret_decay_f, m_ret_decay_b, m_conv_dw_w, m_conv_dw_b, m_conv_ln_g, m_conv_ln_b, m_w_out, m_ffn_w_up, m_ffn_dw_w, m_ffn_dw_b, m_ffn_w_down, m_final_norm_g, v_c_ctx, v_w_mod, v_b_mod, v_norm1_g, v_norm2_g, v_w_in, v_ret_decay_f, v_ret_decay_b, v_conv_dw_w, v_conv_dw_b, v_conv_ln_g, v_conv_ln_b, v_w_out, v_ffn_w_up, v_ffn_dw_w, v_ffn_dw_b, v_ffn_w_down, v_final_norm_g):
    _, SEQ, D = x.shape
    CTX = ctx.shape[1]
    L = w_in.shape[0]
    CWs = conv_dw_w.shape[2]
    CW = 4 * CWs
    RW = 4 * w_out.shape[1] - CW
    H = RW // DH
    CFs = ffn_dw_w.shape[-1]
    CF = 4 * CFs
    NMs = w_mod.shape[2]
    T = CTX + SEQ
    ncr, ncc = CTX // R, CTX // RC
    assert CTX == R and RW == CW and RW % DH == 0 and SEQ % R == 0 and R % GRID_W == 0
    assert w_in.shape[2] * 4 == 4 * RW + 2 * CW and NMs * 4 == N_MOD * D

    mx, my, mc = _place()
    me = 4 * mx + 2 * my + mc
    jm = 2 * mx + my
    c_idx = jnp.reshape(mc, (1,)).astype(jnp.int32)

    shapes0 = [(D,), (L, CONV_K, CWs), (L, 9, CFs)]
    g0, _ = gather_sum(_pack([c[0], conv_dw_w, ffn_dw_w.reshape(L, 9, CFs)]), "gather_cond")
    c_all, cw_all, fw_all = _unpack(g0, shapes0)
    conv_w = jnp.concatenate([cw_all[2 * j] for j in range(4)], axis=-1)
    ffn_w = jnp.concatenate([fw_all[2 * j] for j in range(4)], axis=-1)
    conv_w32 = jnp.pad(conv_w, ((0, 0), (0, 32 - CONV_K), (0, 0)))
    ffn_w16 = jnp.pad(ffn_w, ((0, 0), (0, 7), (0, 0)))
    cs = jnp.concatenate([c_all, c_ctx[None, :], jnp.zeros((7, D), F32)], axis=0)
    mod_shard = mod_fwd(cs, w_mod, "mod_fwd")
    g1, _ = gather_sum(mod_shard.reshape(-1, 128), "gather_mod")
    mod_all = g1.reshape(8, L, 16, NMs)
    mod_full = jnp.concatenate([mod_all[2 * j] for j in range(4)], axis=-1) + b_mod[:, None, :]
    mod_mine = lax.dynamic_index_in_dim(mod_full, me, axis=1, keepdims=False)
    modv = jnp.stack([mod_full[:, 8], mod_mine], axis=1).reshape(L, 2, N_MOD, D)

    big = {"w_in": (w_in, True), "w_out": (w_out, False), "w_up": (ffn_w_up, True), "w_down": (ffn_w_down, False)}
    wb = {k: cast_bf16(w, f"cast_{k}") for k, (w, _) in big.items()}
    wg = {k: [gather_weights(wb[k][0:1], f"gather_{k}")[0]] for k in big}
    nxt = lambda k, l: wb[k][l + 1] if l + 1 < L else None

    def wmat(k, l):
        w = wg[k][l]
        return w if big[k][1] else w.reshape(1, w.shape[0] * w.shape[1], w.shape[2])

    cosT, sinT = _rope_tables(SEQ, CTX)
    lgt = [jnp.broadcast_to(jnp.concatenate([jax.nn.log_sigmoid(ret_decay_f[l]), jax.nn.log_sigmoid(ret_decay_b[l])])[:, None],
                            (2 * H, DH)) for l in range(L)]
    row = lambda t: t.reshape(1, -1)

    def project(a, k, l, out_dtype, name, res=None):
        y, xn, gathered = mm_nn(a, wmat(k, l), out_dtype, name, nxt(k, l), res)
        if gathered is not None:
            wg[k].append(gathered)
        return y, xn

    def back(a, dy, k, l, out_dtype, tag):
        dw = mm_tn(a, dy, 4 if big[k][1] else 1, f"mm_{tag}_dw")
        dw = dw if big[k][1] else dw.reshape(4, dw.shape[1] // 4, dw.shape[2])
        da, parts = mm_nt(dy, wmat(k, l), out_dtype, f"mm_{tag}_dx", chip_sums(dw, c_idx, k))
        return da, finish_grad(parts, c_idx, k)

    xs = jnp.concatenate([ctx[0], x[0]], axis=0)
    saved = []
    for l in range(L):
        h1 = norm_mod(xs, row(norm1_g[l]), modv[l], 0, 1, ncr, "norm_mod")
        p, _ = project(h1, "w_in", l, BF16, "mm_in")
        uc = conv31_fwd(p, conv_w32[l], row(conv_dw_b[l]), RW, CW, ncr, "conv31_fwd")
        o_f, o_b, stf, stb = ret_fwd(p, cosT, sinT, lgt[l], H, ncc, "ret_fwd")
        mix = mix_fwd(o_f, o_b, p, uc, row(conv_ln_g[l]), row(conv_ln_b[l]), H, "mix_fwd")
        y1, x2 = project(mix, "w_out", l, BF16, "mm_out", (xs, modv[l][:, 2], CTX))
        h2 = norm_mod(x2, row(norm2_g[l]), modv[l], 3, 4, ncr, "norm_mod")
        up, _ = project(h2, "w_up", l, BF16, "mm_up")
        act, gcs = ffn_act(up, ffn_w16[l], row(ffn_dw_b[l]), CF, ncr, "ffn_act")
        y2, x3 = project(act, "w_down", l, BF16, "mm_down", (x2, modv[l][:, 5], CTX))
        saved.append((xs, h1, p, uc, o_f, o_b, stf, stb, mix, y1, x2, h2, up, act, gcs, y2))
        xs = x3

    dx, acc_loss = loss_head(xs, loss_target[0], row(final_norm_g), ncr, "loss_head")
    loss = lax.psum(0.5 / D * jnp.sum(acc_loss[1]), ("x", "y", "c"))

    small, gbig = [None] * L, {k: [None] * L for k in big}
    for l in reversed(range(L)):
        x1, h1, p, uc, o_f, o_b, stf, stb, mix, y1, x2, h2, up, act, gcs, y2 = saved[l]
        dy2, ag2 = gate_res_bwd(dx, y2, modv[l], 5, ncr, "gate_res_bwd")
        dact, gbig["w_down"][l] = back(act, dy2, "w_down", l, BF16, "down")
        dgc, dup, accb = ffn_act_bwd1(up, gcs, dact, CF, "ffn_act_bwd1")
        dup, accfw = ffn_act_bwd2(dgc, up, dup, ffn_w16[l], CF, ncr, "ffn_act_bwd2")
        dh2, gbig["w_up"][l] = back(h2, dup, "w_up", l, F32, "up")
        dx2, an2 = norm_mod_bwd(x2, row(norm2_g[l]), modv[l], dh2, dx, 4, ncr, "norm_mod_bwd")
        dy1, ag1 = gate_res_bwd(dx2, y1, modv[l], 2, ncr, "gate_res_bwd")
        dmix, gbig["w_out"][l] = back(mix, dy1, "w_out", l, BF16, "out")
        do, dg, duc, accln = mix_bwd(dmix, o_f, o_b, p, uc, row(conv_ln_g[l]), row(conv_ln_b[l]), H, "mix_bwd")
        da, dbg, acccw = conv31_bwd(duc, p, conv_w32[l], RW, CW, ncr, "conv31_bwd")
        dqf, dkf, dvf, dqb, dkb, dvb, glg = ret_bwd(p, do, cosT, sinT, lgt[l], stf, stb, H, ncc, "ret_bwd")
        dp = assemble_dp(dqf, dqb, dkf, dkb, dvf, dvb, dg, da, dbg, "assemble_dp")
        dh1, gbig["w_in"][l] = back(h1, dp, "w_in", l, F32, "in")
        dx, an1 = norm_mod_bwd(x1, row(norm1_g[l]), modv[l], dh1, dx2, 1, ncr, "norm_mod_bwd")
        dmod = jnp.stack([jnp.stack([an1[0], an1[1], ag1[0], an2[0], an2[1], ag2[0]]),
                          jnp.stack([an1[2], an1[3], ag1[1], an2[2], an2[3], ag2[1]])])
        dlg = jnp.sum(glg, axis=(1, 2))
        dth = dlg * jnp.concatenate([jax.nn.sigmoid(-ret_decay_f[l]), jax.nn.sigmoid(-ret_decay_b[l])])
        small[l] = [dmod, an1[4], an2[4], acccw[31], accln[0], accln[1], acccw[:CONV_K], accfw[:9], accb[0], dth]
    grad_x = dx[CTX:][None]

    shapes1 = [(2, N_MOD, D), (D,), (D,), (CW,), (CW,), (CW,), (CONV_K, CW), (9, CF), (CF,), (2 * H,)]
    flat_parts = [t for l in range(L) for t in small[l]] + [acc_loss[0]]
    g2, s2 = gather_sum(_pack(flat_parts), "gather_small_grads")
    sums = _unpack(s2, shapes1 * L + [(D,)])
    per_dev = _unpack(g2, shapes1 * L + [(D,)])
    nS = len(shapes1)
    col = lambda i: jnp.stack([sums[l * nS + i] for l in range(L)])
    dmod_sum = col(0)
    dmod_dev = jnp.stack([per_dev[l * nS] for l in range(L)], axis=0)
    g_b_mod = (dmod_sum[:, 0] + dmod_sum[:, 1]).reshape(L, N_MOD * D)
    g_norm1, g_norm2 = col(1), col(2)
    g_conv_b, g_ln_g, g_ln_b = col(3), col(4), col(5)
    g_conv_w = lax.dynamic_slice_in_dim(col(6), jm * CWs, CWs, axis=2)
    g_ffn_w = lax.dynamic_slice_in_dim(col(7), jm * CFs, CFs, axis=2).reshape(L, 3, 3, CFs)
    g_ffn_b = col(8)
    g_ret = col(9)
    g_final = sums[-1]

    dmod_rows = jnp.concatenate([dmod_dev[:, :, 1].reshape(L, 8, N_MOD * D), dmod_sum[:, 0].reshape(L, 1, N_MOD * D),
                                 jnp.zeros((L, 7, N_MOD * D), F32)], axis=1)
    dmod_shard = lax.dynamic_slice_in_dim(dmod_rows, jm * NMs, NMs, axis=2)
    g_w_mod, ds_part = mod_bwd(cs, w_mod, dmod_shard, "mod_bwd")
    _, ds_sum = gather_sum(ds_part.reshape(-1, 128), "gather_dsilu")
    ds_ctx = 0.5 * ds_sum.reshape(16, D)[8]
    sg = jax.nn.sigmoid(c_ctx)
    g_c_ctx = ds_ctx * (sg * (1.0 + c_ctx * (1.0 - sg)))

    grads = {
        "c_ctx": g_c_ctx, "w_mod": g_w_mod, "b_mod": g_b_mod, "norm1_g": g_norm1, "norm2_g": g_norm2,
        "w_in": jnp.stack(gbig["w_in"]), "ret_decay_f": g_ret[:, :H], "ret_decay_b": g_ret[:, H:],
        "conv_dw_w": g_conv_w, "conv_dw_b": g_conv_b, "conv_ln_g": g_ln_g, "conv_ln_b": g_ln_b,
        "w_out": jnp.stack(gbig["w_out"]), "ffn_w_up": jnp.stack(gbig["w_up"]), "ffn_dw_w": g_ffn_w,
        "ffn_dw_b": g_ffn_b, "ffn_w_down": jnp.stack(gbig["w_down"]), "final_norm_g": g_final,
    }
    params = {
        "c_ctx": (c_ctx, m_c_ctx, v_c_ctx), "w_mod": (w_mod, m_w_mod, v_w_mod), "b_mod": (b_mod, m_b_mod, v_b_mod),
        "norm1_g": (norm1_g, m_norm1_g, v_norm1_g), "norm2_g": (norm2_g, m_norm2_g, v_norm2_g),
        "w_in": (w_in, m_w_in, v_w_in), "ret_decay_f": (ret_decay_f, m_ret_decay_f, v_ret_decay_f),
        "ret_decay_b": (ret_decay_b, m_ret_decay_b, v_ret_decay_b),
        "conv_dw_w": (conv_dw_w, m_conv_dw_w, v_conv_dw_w), "conv_dw_b": (conv_dw_b, m_conv_dw_b, v_conv_dw_b),
        "conv_ln_g": (conv_ln_g, m_conv_ln_g, v_conv_ln_g), "conv_ln_b": (conv_ln_b, m_conv_ln_b, v_conv_ln_b),
        "w_out": (w_out, m_w_out, v_w_out), "ffn_w_up": (ffn_w_up, m_ffn_w_up, v_ffn_w_up),
        "ffn_dw_w": (ffn_dw_w, m_ffn_dw_w, v_ffn_dw_w), "ffn_dw_b": (ffn_dw_b, m_ffn_dw_b, v_ffn_dw_b),
        "ffn_w_down": (ffn_w_down, m_ffn_w_down, v_ffn_w_down),
        "final_norm_g": (final_norm_g, m_final_norm_g, v_final_norm_g),
    }
    names = list(params)
    upd = {n: adamw(params[n][0], grads[n], params[n][1], params[n][2], f"adamw_{n}") for n in names}
    return (loss, grad_x, *[grads[n] for n in names], *[upd[n][0] for n in names],
            *[upd[n][1] for n in names], *[upd[n][2] for n in names])
```

```python
import jax
import jax.numpy as jnp
from jax import lax
from jax.experimental import pallas as pl
from jax.experimental.pallas import tpu as pltpu

F32 = jnp.float32
BF16 = jnp.bfloat16
EPS = 1e-6
DH = 128
RC = 128
GRID_W = 64
ROPE_THETA = 10000.0
N_MOD = 6
R = 256
HALO = 16
CONV_K = 31
SUB = 64
LANES = 256
FLANES = 128
FFN_COLS = 1408
VMEM_LIMIT = 48 * 1024 * 1024
MESH = pl.DeviceIdType.MESH
ADAM_LR, ADAM_B1, ADAM_B2, ADAM_EPS, ADAM_WD, ADAM_STEP = 0.001, 0.9, 0.999, 1e-08, 0.01, 10


def _cp(n):
    return pltpu.CompilerParams(dimension_semantics=("arbitrary",) * n, vmem_limit_bytes=VMEM_LIMIT)


def _cp0():
    return pltpu.CompilerParams(vmem_limit_bytes=VMEM_LIMIT)


def _sig(v):
    return 1.0 / (1.0 + jnp.exp(-v))


def _dot(a, b):
    return jnp.dot(a, b, preferred_element_type=F32)


def _dot_nt(a, b):
    return lax.dot_general(a, b, (((1,), (1,)), ((), ())), preferred_element_type=F32)


def _dot_tn(a, b):
    return lax.dot_general(a, b, (((0,), (0,)), ((), ())), preferred_element_type=F32)


def _lane_tile(n, cap):
    t = (min(n, cap) // 128) * 128
    while t >= 128:
        if n % t == 0:
            return t
        t -= 128
    raise ValueError(f"no lane tile for {n}")


def _row_tile(rows, cols, max_elems):
    if rows * cols <= max_elems:
        return rows
    t = (min(rows, max(8, max_elems // cols)) // 8) * 8
    while t >= 8:
        if rows % t == 0:
            return t
        t -= 8
    raise ValueError(f"no row tile for {rows}x{cols}")


def _place():
    return lax.axis_index("x"), lax.axis_index("y"), lax.axis_index("c")


def _other_chips(x, y):
    return [(1 - x, y), (x, 1 - y), (1 - x, 1 - y)]


def _first_last(grid):
    ids = [pl.program_id(k) for k in range(len(grid))]
    first, last = ids[0] == 0, ids[0] == grid[0] - 1
    for k in range(1, len(grid)):
        first = jnp.logical_and(first, ids[k] == 0)
        last = jnp.logical_and(last, ids[k] == grid[k] - 1)
    return first, last


def _side_copies(src_slab, dst_ref, send_sems, recv_sems, local_sem):
    x, y, c = _place()
    jm = 2 * x + y
    chips = _other_chips(x, y)
    mine = pltpu.make_async_copy(src_slab(jm), dst_ref.at[jm], local_sem)
    sends = [pltpu.make_async_remote_copy(src_ref=src_slab(2 * px + py), dst_ref=dst_ref.at[jm], send_sem=send_sems.at[t],
                                          recv_sem=recv_sems.at[t], device_id=(px, py, c), device_id_type=MESH)
             for t, (px, py) in enumerate(chips)]
    recvs = [pltpu.make_async_remote_copy(src_ref=dst_ref.at[2 * px + py], dst_ref=dst_ref.at[2 * px + py],
                                          send_sem=send_sems.at[t], recv_sem=recv_sems.at[t], device_id=(px, py, c),
                                          device_id_type=MESH) for t, (px, py) in enumerate(chips)]
    return mine, sends, recvs


def _side_start(mine, sends, recvs):
    mine.start()
    for s in sends:
        s.start()


def _side_finish(mine, sends, recvs):
    for r in recvs:
        r.wait_recv()
    for s in sends:
        s.wait_send()
    mine.wait()


_SIDE_SEMS = [pltpu.SemaphoreType.DMA((3,)), pltpu.SemaphoreType.DMA((3,)), pltpu.SemaphoreType.DMA]


def mm_nn(a, b, out_dtype, name, bcast=None, res=None):
    M, K = a.shape
    S, Kb, Nb = b.shape
    N = S * Nb
    assert K == Kb
    tm = _lane_tile(M, 768 if res is not None else 1408)
    tn, tk = _lane_tile(Nb, 1536), _lane_tile(K, 2816)
    nk = K // tk
    grid = (M // tm, N // tn, nk)
    tps = Nb // tn
    b_map = lambda i, j, k: (j // tps, k, j % tps)
    n_in = 2 + (2 if res is not None else 0) + (1 if bcast is not None else 0)
    n_out = 1 + (1 if res is not None else 0) + (1 if bcast is not None else 0)

    def body(*refs):
        ins, outs, scr = list(refs[:n_in]), list(refs[n_in:n_in + n_out]), list(refs[n_in + n_out:])
        a_ref, b_ref = ins[0], ins[1]
        o_ref = outs[0]
        if nk > 1:
            acc_ref = scr.pop(0)
        if bcast is not None:
            first, last = _first_last(grid)
            side = _side_copies(lambda j: ins[-1], outs[-1], *scr)
            pl.when(first)(lambda: _side_start(*side))
        k = pl.program_id(2)

        def finish(acc):
            o_ref[...] = acc.astype(o_ref.dtype)
            if res is not None:
                x_ref, g_ref = ins[2], ins[3]
                rows = pl.program_id(0) * tm + lax.broadcasted_iota(jnp.int32, (tm, 1), 0)
                gate = jnp.where(rows < res[2], g_ref[0:1, :], g_ref[1:2, :])
                outs[1][...] = x_ref[...] + gate * acc

        if nk == 1:
            finish(_dot(a_ref[...], b_ref[...]))
        else:
            @pl.when(k == 0)
            def _():
                acc_ref[...] = jnp.zeros_like(acc_ref)

            acc_ref[...] += _dot(a_ref[...], b_ref[...])
            pl.when(k == nk - 1)(lambda: finish(acc_ref[...]))

        if bcast is not None:
            pl.when(last)(lambda: _side_finish(*side))

    hbm = pl.BlockSpec(memory_space=pl.ANY)
    tile = pl.BlockSpec((tm, tn), lambda i, j, k: (i, j))
    in_specs = [pl.BlockSpec((tm, tk), lambda i, j, k: (i, k)), pl.BlockSpec((None, tk, tn), b_map)]
    out_specs = [tile]
    out_shape = [jax.ShapeDtypeStruct((M, N), out_dtype)]
    scratch = [pltpu.VMEM((tm, tn), F32)] if nk > 1 else []
    args = [a, b]
    if res is not None:
        in_specs += [tile, pl.BlockSpec((2, tn), lambda i, j, k: (0, j))]
        out_specs.append(tile)
        out_shape.append(jax.ShapeDtypeStruct((M, N), F32))
        args += [res[0], res[1]]
    if bcast is not None:
        in_specs.append(hbm)
        out_specs.append(hbm)
        out_shape.append(jax.ShapeDtypeStruct((4,) + bcast.shape, bcast.dtype))
        scratch += _SIDE_SEMS
        args.append(bcast)
    got = pl.pallas_call(body, grid=grid, in_specs=in_specs, out_specs=out_specs, out_shape=out_shape,
                         scratch_shapes=scratch, compiler_params=_cp(3), name=name)(*args)
    return got[0], (got[1] if res is not None else None), (got[-1] if bcast is not None else None)


def mm_nt(a, b, out_dtype, name, exch, share=None):
    M, N = a.shape
    S, K, Nb = b.shape
    assert N == S * Nb
    tm, tko, tnr = _lane_tile(M, 1408), _lane_tile(K, 1536), _lane_tile(Nb, 1536)
    nr = N // tnr
    grid = (M // tm, K // tko, nr)
    tps = Nb // tnr
    b_map = lambda i, j, r: (r // tps, j, r % tps)
    n_side = 1 if share is None else 2

    def body(*refs):
        a_ref, b_ref, e_ref = refs[:3]
        o_ref, p_ref = refs[2 + n_side], refs[3 + n_side]
        scr = refs[2 + 2 * n_side + 1:]
        acc_ref, send_sems, recv_sems, local_sem = scr[:4]
        first, last = _first_last(grid)
        side = _side_copies(lambda j: e_ref.at[j], p_ref, send_sems, recv_sems, local_sem)
        if share is not None:
            s_ref, q_ref, ssend, srecv = refs[3], refs[4 + n_side], scr[4], scr[5]
            x, y, c = _place()
            to_sib = [pltpu.make_async_remote_copy(src_ref=s_ref.at[s], dst_ref=q_ref.at[s], send_sem=ssend.at[s],
                                                   recv_sem=srecv.at[s], device_id=(x, y, 1 - c), device_id_type=MESH)
                      for s in range(share.shape[0])]

        @pl.when(first)
        def _():
            _side_start(*side)
            if share is not None:
                for cpy in to_sib:
                    cpy.start()

        r = pl.program_id(2)

        @pl.when(r == 0)
        def _():
            acc_ref[...] = jnp.zeros_like(acc_ref)

        acc_ref[...] += _dot_nt(a_ref[...], b_ref[...])

        @pl.when(r == nr - 1)
        def _():
            o_ref[...] = acc_ref[...].astype(o_ref.dtype)

        @pl.when(last)
        def _():
            _side_finish(*side)
            if share is not None:
                for cpy in to_sib:
                    cpy.wait_recv()
                for cpy in to_sib:
                    cpy.wait_send()

    hbm = pl.BlockSpec(memory_space=pl.ANY)
    sides = [exch] + ([] if share is None else [share])
    in_specs = [pl.BlockSpec((tm, tnr), lambda i, j, r: (i, r)), pl.BlockSpec((None, tko, tnr), b_map)] + [hbm] * n_side
    out_specs = [pl.BlockSpec((tm, tko), lambda i, j, r: (i, j))] + [hbm] * n_side
    out_shape = [jax.ShapeDtypeStruct((M, K), out_dtype)] + [jax.ShapeDtypeStruct(t.shape, t.dtype) for t in sides]
    scratch = [pltpu.VMEM((tm, tko), F32)] + _SIDE_SEMS
    if share is not None:
        scratch += [pltpu.SemaphoreType.DMA((share.shape[0],)), pltpu.SemaphoreType.DMA((share.shape[0],))]
    got = pl.pallas_call(body, grid=grid, in_specs=in_specs, out_specs=out_specs, out_shape=out_shape,
                         scratch_shapes=scratch, compiler_params=_cp(3), name=name)(a, b, *sides)
    return got[0], got[1], (got[2] if share is not None else None)


def mm_tn(a, c, S, name):
    M, K = a.shape
    N = c.shape[1]
    Kb, Nb = K, N // S
    tm, tk, tn = _lane_tile(M, 1408), _lane_tile(Kb, 1536), _lane_tile(Nb, 1536)
    nm = M // tm
    tps = Nb // tn
    o_map = lambda i, j, m: (j // tps, i, j % tps)

    def body(a_ref, c_ref, o_ref, acc_ref):
        m = pl.program_id(2)

        @pl.when(m == 0)
        def _():
            acc_ref[...] = jnp.zeros_like(acc_ref)

        acc_ref[...] += _dot_tn(a_ref[...], c_ref[...])

        @pl.when(m == nm - 1)
        def _():
            o_ref[...] = acc_ref[...].astype(BF16)

    return pl.pallas_call(
        body, grid=(K // tk, N // tn, nm),
        in_specs=[pl.BlockSpec((tm, tk), lambda i, j, m: (m, i)), pl.BlockSpec((tm, tn), lambda i, j, m: (m, j))],
        out_specs=pl.BlockSpec((None, tk, tn), o_map),
        out_shape=jax.ShapeDtypeStruct((S, Kb, Nb), BF16),
        scratch_shapes=[pltpu.VMEM((tk, tn), F32)], compiler_params=_cp(3), name=name)(a, c)


def _mod_spec(D, ncr):
    return pl.BlockSpec((None, N_MOD, D), lambda i: (jnp.where(i < ncr, 0, 1), 0, 0))


def norm_mod(x, g, modv, i_sh, i_sc, ncr, name):
    T, D = x.shape

    def body(x_ref, g_ref, m_ref, h_ref):
        xv = x_ref[...]
        r = lax.rsqrt(jnp.mean(xv * xv, axis=-1, keepdims=True) + EPS)
        n = xv * r * g_ref[...]
        h_ref[...] = (n * (1.0 + m_ref[i_sc:i_sc + 1, :]) + m_ref[i_sh:i_sh + 1, :]).astype(BF16)

    return pl.pallas_call(
        body, grid=(T // R,),
        in_specs=[pl.BlockSpec((R, D), lambda i: (i, 0)), pl.BlockSpec((1, D), lambda i: (0, 0)), _mod_spec(D, ncr)],
        out_specs=pl.BlockSpec((R, D), lambda i: (i, 0)),
        out_shape=jax.ShapeDtypeStruct((T, D), BF16), compiler_params=_cp(1), name=name)(x, g, modv)


def norm_mod_bwd(x, g, modv, dh, dres, i_sc, ncr, name):
    T, D = x.shape

    def body(x_ref, g_ref, m_ref, dh_ref, dr_ref, dx_ref, acc_ref):
        i = pl.program_id(0)

        @pl.when(i == 0)
        def _():
            acc_ref[...] = jnp.zeros_like(acc_ref)

        xv = x_ref[...]
        r = lax.rsqrt(jnp.mean(xv * xv, axis=-1, keepdims=True) + EPS)
        xh = xv * r
        gv = g_ref[...]
        dhv = dh_ref[...]
        dn = dhv * (1.0 + m_ref[i_sc:i_sc + 1, :])
        s_sh = jnp.sum(dhv, axis=0, keepdims=True)
        s_sc = jnp.sum(dhv * (xh * gv), axis=0, keepdims=True)
        acc_ref[4:5, :] += jnp.sum(dn * xh, axis=0, keepdims=True)
        dxh = dn * gv
        dx_ref[...] = dr_ref[...] + r * (dxh - xh * jnp.mean(dxh * xh, axis=-1, keepdims=True))

        @pl.when(i < ncr)
        def _():
            acc_ref[0:1, :] += s_sh
            acc_ref[1:2, :] += s_sc

        @pl.when(i >= ncr)
        def _():
            acc_ref[2:3, :] += s_sh
            acc_ref[3:4, :] += s_sc

    row = pl.BlockSpec((R, D), lambda i: (i, 0))
    return pl.pallas_call(
        body, grid=(T // R,),
        in_specs=[row, pl.BlockSpec((1, D), lambda i: (0, 0)), _mod_spec(D, ncr), row, row],
        out_specs=[row, pl.BlockSpec((8, D), lambda i: (0, 0))],
        out_shape=[jax.ShapeDtypeStruct((T, D), F32), jax.ShapeDtypeStruct((8, D), F32)],
        compiler_params=_cp(1), name=name)(x, g, modv, dh, dres)


def gate_res_bwd(dx, y, modv, i_g, ncr, name):
    T, D = dx.shape

    def body(dx_ref, y_ref, m_ref, dy_ref, acc_ref):
        i = pl.program_id(0)

        @pl.when(i == 0)
        def _():
            acc_ref[...] = jnp.zeros_like(acc_ref)

        dxv = dx_ref[...]
        dy_ref[...] = (m_ref[i_g:i_g + 1, :] * dxv).astype(BF16)
        s = jnp.sum(dxv * y_ref[...], axis=0, keepdims=True)

        @pl.when(i < ncr)
        def _():
            acc_ref[0:1, :] += s

        @pl.when(i >= ncr)
        def _():
            acc_ref[1:2, :] += s

    row = pl.BlockSpec((R, D), lambda i: (i, 0))
    return pl.pallas_call(
        body, grid=(T // R,), in_specs=[row, row, _mod_spec(D, ncr)],
        out_specs=[row, pl.BlockSpec((8, D), lambda i: (0, 0))],
        out_shape=[jax.ShapeDtypeStruct((T, D), BF16), jax.ShapeDtypeStruct((8, D), F32)],
        compiler_params=_cp(1), name=name)(dx, y, modv)


def loss_head(x, target, g, ncr, name):
    T, D = x.shape

    def body(x_ref, t_ref, g_ref, dx_ref, acc_ref):
        i = pl.program_id(0)

        @pl.when(i == 0)
        def _():
            acc_ref[...] = jnp.zeros_like(acc_ref)

        @pl.when(i < ncr)
        def _():
            dx_ref[...] = jnp.zeros_like(dx_ref)

        @pl.when(i >= ncr)
        def _():
            xv = x_ref[...]
            r = lax.rsqrt(jnp.mean(xv * xv, axis=-1, keepdims=True) + EPS)
            xh = xv * r
            gv = g_ref[...]
            e = xh * gv - t_ref[...]
            acc_ref[1:2, :] += jnp.sum(e * e, axis=0, keepdims=True)
            dy = e * (1.0 / D)
            acc_ref[0:1, :] += jnp.sum(dy * xh, axis=0, keepdims=True)
            dxh = dy * gv
            dx_ref[...] = r * (dxh - xh * jnp.mean(dxh * xh, axis=-1, keepdims=True))

    row = pl.BlockSpec((R, D), lambda i: (i, 0))
    return pl.pallas_call(
        body, grid=(T // R,),
        in_specs=[row, pl.BlockSpec((R, D), lambda i: (jnp.maximum(i - ncr, 0), 0)), pl.BlockSpec((1, D), lambda i: (0, 0))],
        out_specs=[row, pl.BlockSpec((8, D), lambda i: (0, 0))],
        out_shape=[jax.ShapeDtypeStruct((T, D), F32), jax.ShapeDtypeStruct((8, D), F32)],
        compiler_params=_cp(1), name=name)(x, target, g)


def _conv31_specs(T):
    nh = R // HALO
    pv = lambda i: jnp.maximum(i * nh - 1, 0)
    nx = lambda i: jnp.minimum((i + 1) * nh, T // HALO - 1)
    return pv, nx


def _shifted_copies(E):
    n = E.shape[1]
    for s in range(1, 8):
        E[s, 0:n - 8, :] = E[0, pl.ds(s, n - 8), :]


def _tap31(E, r0, c0, o):
    return E[o % 8, pl.ds(r0 + 8 * (o // 8), SUB), pl.ds(c0, LANES)]


def conv31_fwd(p, w32, b, RW, CW, ncr, name):
    T = p.shape[0]
    nT, cbk = T // R, _lane_tile(CW, 1024)
    n = R + 2 * HALO
    a0, g0 = 4 * RW // cbk, (4 * RW + CW) // cbk
    pv, nx = _conv31_specs(T)

    def body(a, g, ap, gp, an, gn, w, bb, uc, E):
        i = pl.program_id(0)
        has_prev = jnp.logical_and(i != 0, i != ncr)
        has_next = jnp.logical_and(i != ncr - 1, i != nT - 1)
        glu = lambda u, v: u.astype(F32) * _sig(v.astype(F32))
        E[0, 0:HALO, :] = jnp.where(has_prev, glu(ap[...], gp[...]), 0.0)
        E[0, HALO:HALO + R, :] = glu(a[...], g[...])
        E[0, HALO + R:, :] = jnp.where(has_next, glu(an[...], gn[...]), 0.0)
        _shifted_copies(E)
        for r0 in range(0, R, SUB):
            for c0 in range(0, cbk, LANES):
                cols = pl.ds(c0, LANES)
                acc = jnp.broadcast_to(bb[:, cols], (SUB, LANES))
                for k in range(CONV_K):
                    acc = acc + w[k:k + 1, cols] * _tap31(E, r0, c0, 1 + k)
                uc[r0:r0 + SUB, c0:c0 + LANES] = acc

    cur = lambda c0: pl.BlockSpec((R, cbk), lambda i, j: (i, c0 + j))
    hp = lambda c0: pl.BlockSpec((HALO, cbk), lambda i, j: (pv(i), c0 + j))
    hn = lambda c0: pl.BlockSpec((HALO, cbk), lambda i, j: (nx(i), c0 + j))
    return pl.pallas_call(
        body, grid=(nT, CW // cbk),
        in_specs=[cur(a0), cur(g0), hp(a0), hp(g0), hn(a0), hn(g0),
                  pl.BlockSpec((32, cbk), lambda i, j: (0, j)), pl.BlockSpec((1, cbk), lambda i, j: (0, j))],
        out_specs=pl.BlockSpec((R, cbk), lambda i, j: (i, j)),
        out_shape=jax.ShapeDtypeStruct((T, CW), F32),
        scratch_shapes=[pltpu.VMEM((8, n, cbk), F32)],
        compiler_params=_cp(2), name=name)(p, p, p, p, p, p, w32, b)


def conv31_bwd(duc, p, w32, RW, CW, ncr, name):
    T = p.shape[0]
    nT, cbk = T // R, _lane_tile(CW, 1024)
    n = R + 2 * HALO
    a0, g0 = 4 * RW // cbk, (4 * RW + CW) // cbk
    pv, nx = _conv31_specs(T)

    def body(d, dp_, dn_, a, g, ap, gp, an, gn, w, da, dg, accw, U, Dd):
        i = pl.program_id(1)

        @pl.when(i == 0)
        def _():
            accw[...] = jnp.zeros_like(accw)

        has_prev = jnp.logical_and(i != 0, i != ncr)
        has_next = jnp.logical_and(i != ncr - 1, i != nT - 1)
        glu = lambda u, v: u.astype(F32) * _sig(v.astype(F32))
        U[0, 0:HALO, :] = jnp.where(has_prev, glu(ap[...], gp[...]), 0.0)
        U[0, HALO:HALO + R, :] = glu(a[...], g[...])
        U[0, HALO + R:, :] = jnp.where(has_next, glu(an[...], gn[...]), 0.0)
        Dd[0, 0:HALO, :] = jnp.where(has_prev, dp_[...], 0.0)
        Dd[0, HALO:HALO + R, :] = d[...]
        Dd[0, HALO + R:, :] = jnp.where(has_next, dn_[...], 0.0)
        _shifted_copies(U)
        _shifted_copies(Dd)
        for r0 in range(0, R, SUB):
            for c0 in range(0, cbk, LANES):
                cols, blk = pl.ds(c0, LANES), (slice(r0, r0 + SUB), slice(c0, c0 + LANES))
                du = jnp.zeros((SUB, LANES), F32)
                for k in range(CONV_K):
                    du = du + w[k:k + 1, cols] * _tap31(Dd, r0, c0, HALO + 15 - k)
                av = a[blk].astype(F32)
                sg = _sig(g[blk].astype(F32))
                da[blk] = (du * sg).astype(BF16)
                dg[blk] = (du * av * sg * (1.0 - sg)).astype(BF16)
                dcur = d[blk]
                for k in range(CONV_K):
                    accw[k:k + 1, cols] += jnp.sum(dcur * _tap31(U, r0, c0, 1 + k), axis=0, keepdims=True)
                accw[31:32, cols] += jnp.sum(dcur, axis=0, keepdims=True)

    cur = lambda c0: pl.BlockSpec((R, cbk), lambda j, i: (i, c0 + j))
    hp = lambda c0: pl.BlockSpec((HALO, cbk), lambda j, i: (pv(i), c0 + j))
    hn = lambda c0: pl.BlockSpec((HALO, cbk), lambda j, i: (nx(i), c0 + j))
    out = pl.BlockSpec((R, cbk), lambda j, i: (i, j))
    ext = pltpu.VMEM((8, n, cbk), F32)
    return pl.pallas_call(
        body, grid=(CW // cbk, nT),
        in_specs=[cur(0), hp(0), hn(0), cur(a0), cur(g0), hp(a0), hp(g0), hn(a0), hn(g0),
                  pl.BlockSpec((32, cbk), lambda j, i: (0, j))],
        out_specs=[out, out, pl.BlockSpec((32, cbk), lambda j, i: (0, j))],
        out_shape=[jax.ShapeDtypeStruct((T, CW), BF16), jax.ShapeDtypeStruct((T, CW), BF16),
                   jax.ShapeDtypeStruct((32, CW), F32)],
        scratch_shapes=[ext, ext],
        compiler_params=_cp(2), name=name)(duc, duc, duc, p, p, p, p, p, p, w32)


def _rope(v, cosv, sinv, first):
    swapped = jnp.where(first, pltpu.roll(v, 96, 1), pltpu.roll(v, 32, 1))
    return v * cosv + swapped * sinv


def _unrope(v, cosv, sinv, first):
    z = v * sinv
    return v * cosv + jnp.where(first, pltpu.roll(z, 96, 1), pltpu.roll(z, 32, 1))


def _decay_tables(lg_ref, H, DM, QD, KD, CD):
    n = lax.broadcasted_iota(jnp.int32, (RC, DH), 0).astype(F32)
    m = lax.broadcasted_iota(jnp.int32, (RC, DH), 1).astype(F32)
    for d in range(2):
        for h in range(H):
            i = d * H + h
            lg = lg_ref[i:i + 1, :]
            diff = (n - m) if d == 0 else (m - n)
            DM[i] = jnp.where(diff >= 0, jnp.exp(lg * jnp.maximum(diff, 0.0)), 0.0)
            QD[i] = jnp.exp(lg * ((n + 1.0) if d == 0 else (RC - n)))
            KD[i] = jnp.exp(lg * ((RC - 1.0 - n) if d == 0 else n))
            CD[i] = jnp.exp(lg * float(RC)) + jnp.zeros((RC, DH), F32)


def _chunk_orders(NC, ncc):
    cf = lambda s: s
    cb = lambda s: jnp.where(s < ncc, ncc - 1 - s, NC - 1 - (s - ncc))
    return cf, cb


def ret_fwd(p, cosT, sinT, lgt, H, ncc, name):
    T = p.shape[0]
    RW, NC = H * DH, T // RC
    cf, cb = _chunk_orders(NC, ncc)
    scale = DH ** -0.5

    def body(qf, kf, vf, qb, kb, vb, cosf, sinf, cosb, sinb, lg_ref, of_ref, ob_ref, sf_ref, sb_ref, S, DM, QD, KD, CD):
        s = pl.program_id(0)

        @pl.when(s == 0)
        def _():
            S[...] = jnp.zeros_like(S)
            _decay_tables(lg_ref, H, DM, QD, KD, CD)

        first = (lax.broadcasted_iota(jnp.int32, (RC, DH), 1) % 64) < 32
        for d, (q_ref, k_ref, v_ref, c_ref, s_ref, o_ref, st_ref) in enumerate(
                ((qf, kf, vf, cosf, sinf, of_ref, sf_ref), (qb, kb, vb, cosb, sinb, ob_ref, sb_ref))):
            cosv, sinv = c_ref[...], s_ref[...]
            for h in range(H):
                hs, i = slice(h * DH, (h + 1) * DH), d * H + h
                q16 = _rope(q_ref[:, hs].astype(F32), cosv, sinv, first).astype(BF16)
                k = _rope(k_ref[:, hs].astype(F32), cosv, sinv, first) * scale
                k16 = k.astype(BF16)
                v = v_ref[:, hs]
                s_in = S[i]
                s16 = s_in.astype(BF16)
                st_ref[h] = s16
                sc = _dot_nt(q16, k16) * DM[i]
                o_ref[:, hs] = _dot(sc.astype(BF16), v) + _dot(q16, s16) * QD[i]
                S[i] = s_in * CD[i] + _dot_tn((k * KD[i]).astype(BF16), v)

    pspec = lambda col, cm: pl.BlockSpec((RC, RW), lambda s: (cm(s), col))
    tspec = lambda cm: pl.BlockSpec((RC, DH), lambda s: (cm(s), 0))
    ospec = lambda cm: pl.BlockSpec((RC, RW), lambda s: (cm(s), 0))
    stspec = pl.BlockSpec((None, H, DH, DH), lambda s: (s, 0, 0, 0))
    tab = pltpu.VMEM((2 * H, RC, DH), F32)
    return pl.pallas_call(
        body, grid=(NC,),
        in_specs=[pspec(0, cf), pspec(1, cf), pspec(2, cf), pspec(0, cb), pspec(1, cb), pspec(2, cb),
                  tspec(cf), tspec(cf), tspec(cb), tspec(cb), pl.BlockSpec((2 * H, DH), lambda s: (0, 0))],
        out_specs=[ospec(cf), ospec(cb), stspec, stspec],
        out_shape=[jax.ShapeDtypeStruct((T, RW), F32), jax.ShapeDtypeStruct((T, RW), F32),
                   jax.ShapeDtypeStruct((NC, H, DH, DH), BF16), jax.ShapeDtypeStruct((NC, H, DH, DH), BF16)],
        scratch_shapes=[tab, tab, tab, tab, tab], compiler_params=_cp(1), name=name)(
            p, p, p, p, p, p, cosT, sinT, cosT, sinT, lgt)


def ret_bwd(p, do, cosT, sinT, lgt, stf, stb, H, ncc, name):
    T = p.shape[0]
    RW, NC = H * DH, T // RC
    cf0, cb0 = _chunk_orders(NC, ncc)
    cf = lambda sp: cf0(NC - 1 - sp)
    cb = lambda sp: cb0(NC - 1 - sp)
    scale = DH ** -0.5

    def body(qf, kf, vf, qb, kb, vb, dof, dob, cosf, sinf, cosb, sinb, lg_ref, stf_ref, stb_ref,
             dqf, dkf, dvf, dqb, dkb, dvb, glg, dS, DM, QD, KD, CD):
        sp = pl.program_id(0)

        @pl.when(sp == 0)
        def _():
            dS[...] = jnp.zeros_like(dS)
            glg[...] = jnp.zeros_like(glg)
            _decay_tables(lg_ref, H, DM, QD, KD, CD)

        first = (lax.broadcasted_iota(jnp.int32, (RC, DH), 1) % 64) < 32
        n = lax.broadcasted_iota(jnp.int32, (RC, DH), 0).astype(F32)
        m = lax.broadcasted_iota(jnp.int32, (RC, DH), 1).astype(F32)
        for d, (q_ref, k_ref, v_ref, do_ref, c_ref, s_ref, st_ref, dq_ref, dk_ref, dv_ref) in enumerate(
                ((qf, kf, vf, dof, cosf, sinf, stf_ref, dqf, dkf, dvf),
                 (qb, kb, vb, dob, cosb, sinb, stb_ref, dqb, dkb, dvb))):
            cosv, sinv = c_ref[...], s_ref[...]
            diff = (n - m) if d == 0 else (m - n)
            posq = (n + 1.0) if d == 0 else (RC - n)
            posk = (RC - 1.0 - n) if d == 0 else n
            for h in range(H):
                hs, i = slice(h * DH, (h + 1) * DH), d * H + h
                q = _rope(q_ref[:, hs].astype(F32), cosv, sinv, first)
                k = _rope(k_ref[:, hs].astype(F32), cosv, sinv, first) * scale
                q16, k16 = q.astype(BF16), k.astype(BF16)
                v = v_ref[:, hs]
                s_in = st_ref[h]
                ds_out = dS[i]
                ds16 = ds_out.astype(BF16)
                do16 = do_ref[:, hs]
                doq = (do16.astype(F32) * QD[i]).astype(BF16)
                a = _dot_nt(q16, k16) * DM[i]
                da_raw = _dot_nt(do16, v)
                da16 = (da_raw * DM[i]).astype(BF16)
                dq_state = _dot_nt(doq, s_in)
                dk_state = _dot_nt(v, ds16) * KD[i]
                dqr = _dot(da16, k16) + dq_state
                dkr = _dot_tn(da16, q16) + dk_state
                dv_ref[:, hs] = _dot_tn(a.astype(BF16), do16) + _dot((k * KD[i]).astype(BF16), ds16)
                dS[i] = ds_out * CD[i] + _dot_tn(q16, doq)
                glg[i] += (da_raw * a * diff + posq * q * dq_state + posk * k * dk_state
                           + float(RC) * CD[i] * ds_out * s_in.astype(F32))
                dq_ref[:, hs] = _unrope(dqr, cosv, sinv, first)
                dk_ref[:, hs] = _unrope(dkr, cosv, sinv, first) * scale

    pspec = lambda col, cm: pl.BlockSpec((RC, RW), lambda s: (cm(s), col))
    tspec = lambda cm: pl.BlockSpec((RC, DH), lambda s: (cm(s), 0))
    ospec = lambda cm: pl.BlockSpec((RC, RW), lambda s: (cm(s), 0))
    stspec = pl.BlockSpec((None, H, DH, DH), lambda s: (NC - 1 - s, 0, 0, 0))
    tab = pltpu.VMEM((2 * H, RC, DH), F32)
    big = jax.ShapeDtypeStruct((T, RW), F32)
    return pl.pallas_call(
        body, grid=(NC,),
        in_specs=[pspec(0, cf), pspec(1, cf), pspec(2, cf), pspec(0, cb), pspec(1, cb), pspec(2, cb),
                  ospec(cf), ospec(cb), tspec(cf), tspec(cf), tspec(cb), tspec(cb),
                  pl.BlockSpec((2 * H, DH), lambda s: (0, 0)), stspec, stspec],
        out_specs=[ospec(cf), ospec(cf), ospec(cf), ospec(cb), ospec(cb), ospec(cb),
                   pl.BlockSpec((2 * H, RC, DH), lambda s: (0, 0, 0))],
        out_shape=[big, big, big, big, big, big, jax.ShapeDtypeStruct((2 * H, RC, DH), F32)],
        scratch_shapes=[tab, tab, tab, tab, tab], compiler_params=_cp(1), name=name)(
            p, p, p, p, p, p, do, do, cosT, sinT, cosT, sinT, lgt, stf, stb)


def mix_fwd(o_f, o_b, p, uc, lng, lnb, H, name):
    T, RW = o_f.shape
    CW = uc.shape[1]

    def body(of_ref, ob_ref, g_ref, uc_ref, lg_ref, lb_ref, out_ref):
        for h in range(H):
            hs = slice(h * DH, (h + 1) * DH)
            o = of_ref[:, hs] + ob_ref[:, hs]
            on = o * lax.rsqrt(jnp.mean(o * o, axis=-1, keepdims=True) + EPS)
            gv = g_ref[:, hs].astype(F32)
            out_ref[:, hs] = (gv * _sig(gv) * on).astype(BF16)
        u = uc_ref[...]
        mu = jnp.mean(u, axis=-1, keepdims=True)
        var = jnp.mean(jnp.square(u - mu), axis=-1, keepdims=True)
        z = (u - mu) * lax.rsqrt(var + EPS) * lg_ref[...] + lb_ref[...]
        out_ref[:, RW:] = (z * _sig(z)).astype(BF16)

    rw = pl.BlockSpec((R, RW), lambda i: (i, 0))
    vec = pl.BlockSpec((1, CW), lambda i: (0, 0))
    return pl.pallas_call(
        body, grid=(T // R,),
        in_specs=[rw, rw, pl.BlockSpec((R, RW), lambda i: (i, 3)), pl.BlockSpec((R, CW), lambda i: (i, 0)), vec, vec],
        out_specs=pl.BlockSpec((R, RW + CW), lambda i: (i, 0)),
        out_shape=jax.ShapeDtypeStruct((T, RW + CW), BF16), compiler_params=_cp(1), name=name)(o_f, o_b, p, uc, lng, lnb)


def mix_bwd(dmix, o_f, o_b, p, uc, lng, lnb, H, name):
    T, RW = o_f.shape
    CW = uc.shape[1]

    def body(dm_ref, of_ref, ob_ref, g_ref, uc_ref, lg_ref, lb_ref, do_ref, dg_ref, duc_ref, acc_ref):
        i = pl.program_id(0)

        @pl.when(i == 0)
        def _():
            acc_ref[...] = jnp.zeros_like(acc_ref)

        for h in range(H):
            hs = slice(h * DH, (h + 1) * DH)
            o = of_ref[:, hs] + ob_ref[:, hs]
            r = lax.rsqrt(jnp.mean(o * o, axis=-1, keepdims=True) + EPS)
            on = o * r
            gv = g_ref[:, hs].astype(F32)
            sg = _sig(gv)
            dmr = dm_ref[:, hs].astype(F32)
            dg_ref[:, hs] = (dmr * on * (sg * (1.0 + gv * (1.0 - sg)))).astype(BF16)
            don = dmr * (gv * sg)
            do_ref[:, hs] = (r * (don - on * jnp.mean(don * on, axis=-1, keepdims=True))).astype(BF16)
        u = uc_ref[...]
        mu = jnp.mean(u, axis=-1, keepdims=True)
        rs = lax.rsqrt(jnp.mean(jnp.square(u - mu), axis=-1, keepdims=True) + EPS)
        zh = (u - mu) * rs
        lg = lg_ref[...]
        z = zh * lg + lb_ref[...]
        sz = _sig(z)
        dz = dm_ref[:, RW:].astype(F32) * (sz * (1.0 + z * (1.0 - sz)))
        acc_ref[0:1, :] += jnp.sum(dz * zh, axis=0, keepdims=True)
        acc_ref[1:2, :] += jnp.sum(dz, axis=0, keepdims=True)
        dzh = dz * lg
        duc_ref[...] = rs * (dzh - jnp.mean(dzh, axis=-1, keepdims=True)
                             - zh * jnp.mean(dzh * zh, axis=-1, keepdims=True))

    rw = pl.BlockSpec((R, RW), lambda i: (i, 0))
    cw = pl.BlockSpec((R, CW), lambda i: (i, 0))
    vec = pl.BlockSpec((1, CW), lambda i: (0, 0))
    return pl.pallas_call(
        body, grid=(T // R,),
        in_specs=[pl.BlockSpec((R, RW + CW), lambda i: (i, 0)), rw, rw, pl.BlockSpec((R, RW), lambda i: (i, 3)), cw, vec, vec],
        out_specs=[rw, rw, cw, pl.BlockSpec((8, CW), lambda i: (0, 0))],
        out_shape=[jax.ShapeDtypeStruct((T, RW), BF16), jax.ShapeDtypeStruct((T, RW), BF16),
                   jax.ShapeDtypeStruct((T, CW), F32), jax.ShapeDtypeStruct((8, CW), F32)],
        compiler_params=_cp(1), name=name)(dmix, o_f, o_b, p, uc, lng, lnb)


def assemble_dp(dqf, dqb, dkf, dkb, dvf, dvb, dg, da, dbg, name):
    T, RW = dqf.shape
    CW = da.shape[1]

    def body(qf, qb, kf, kb, vf, vb, g, a, b, out):
        out[:, 0:RW] = (qf[...] + qb[...]).astype(BF16)
        out[:, RW:2 * RW] = (kf[...] + kb[...]).astype(BF16)
        out[:, 2 * RW:3 * RW] = (vf[...] + vb[...]).astype(BF16)
        out[:, 3 * RW:4 * RW] = g[...]
        out[:, 4 * RW:4 * RW + CW] = a[...]
        out[:, 4 * RW + CW:] = b[...]

    rw = pl.BlockSpec((R, RW), lambda i: (i, 0))
    cw = pl.BlockSpec((R, CW), lambda i: (i, 0))
    W = 4 * RW + 2 * CW
    return pl.pallas_call(
        body, grid=(T // R,), in_specs=[rw] * 7 + [cw, cw], out_specs=pl.BlockSpec((R, W), lambda i: (i, 0)),
        out_shape=jax.ShapeDtypeStruct((T, W), BF16), compiler_params=_cp(1), name=name)(
            dqf, dqb, dkf, dkb, dvf, dvb, dg, da, dbg)


FPAD = 8


def _fill_plain(plain, cur, prv, nxt, has_prev, has_next):
    n, cb = plain.shape[0] - 2 * FPAD, plain.shape[1]
    pv, nv = prv[...], nxt[...]
    plain[0:FPAD, :] = jnp.zeros((FPAD, cb), F32)
    plain[FPAD + n:, :] = jnp.zeros((FPAD, cb), F32)
    plain[FPAD:FPAD + GRID_W, :] = jnp.where(has_prev, pv, jnp.zeros_like(pv)).astype(F32)
    plain[FPAD + GRID_W:FPAD + GRID_W + R, :] = cur[...].astype(F32)
    plain[FPAD + GRID_W + R:FPAD + n, :] = jnp.where(has_next, nv, jnp.zeros_like(nv)).astype(F32)


def _fill_ext(bufs, cur, prv, nxt, has_prev, has_next, is_ctx):
    left, plain, right = bufs
    n, cb = left.shape
    _fill_plain(plain, cur, prv, nxt, has_prev, has_next)
    left[...] = plain[pl.ds(FPAD - 1, n), :]
    right[...] = plain[pl.ds(FPAD + 1, n), :]
    if not is_ctx:
        for r in range(0, n, GRID_W):
            left[r:r + 1, :] = jnp.zeros((1, cb), F32)
            right[r + GRID_W - 1:r + GRID_W, :] = jnp.zeros((1, cb), F32)


def _taps(is_ctx):
    return [(dr, dc) for dr in ((0,) if is_ctx else (-1, 0, 1)) for dc in (-1, 0, 1)]


def _tap_src(bufs, r0, c0, dr, dc):
    off = (FPAD if dc == 0 else 0) + GRID_W + r0 + GRID_W * dr
    return bufs[dc + 1][pl.ds(off, SUB), pl.ds(c0, FLANES)]


def _conv9(bufs, w, r0, c0, is_ctx, flip):
    acc = jnp.zeros((SUB, FLANES), F32)
    for dr, dc in _taps(is_ctx):
        widx = (dr + 1) * 3 + dc + 1
        src = _tap_src(bufs, r0, c0, -dr, -dc) if flip else _tap_src(bufs, r0, c0, dr, dc)
        acc = acc + w[widx:widx + 1, pl.ds(c0, FLANES)] * src
    return acc


def _ffn_specs(T, cb, order):
    nq = R // GRID_W
    pv = lambda i: jnp.maximum(i * nq - 1, 0)
    nx = lambda i: jnp.minimum((i + 1) * nq, T // GRID_W - 1)
    if order == 'ij':
        mk = lambda blk, rf, c0: pl.BlockSpec(blk, lambda i, j: (rf(i), c0 + j))
    else:
        mk = lambda blk, rf, c0: pl.BlockSpec(blk, lambda j, i: (rf(i), c0 + j))
    cur = lambda c0: mk((R, cb), lambda i: i, c0)
    hp = lambda c0: mk((GRID_W, cb), pv, c0)
    hn = lambda c0: mk((GRID_W, cb), nx, c0)
    vec = lambda rows: mk((rows, cb), lambda i: 0, 0)
    return cur, hp, hn, vec


def _ffn_flags(i, ncr, nT):
    return i > ncr, jnp.logical_and(i >= ncr, i != nT - 1)


def _ffn_scratch(cb, sets):
    n = R + 2 * GRID_W
    return [pltpu.VMEM((n, cb), F32), pltpu.VMEM((n + 2 * FPAD, cb), F32), pltpu.VMEM((n, cb), F32)] * sets


def ffn_act(up, w16, b, CF, ncr, name):
    T = up.shape[0]
    nT, cb = T // R, _lane_tile(CF, FFN_COLS)
    ncb = CF // cb
    cur, hp, hn, vec = _ffn_specs(T, cb, 'ij')

    def body(g, v, gp, gn, w, bb, out, gc_out, e0, e1, e2):
        i = pl.program_id(0)
        has_prev, has_next = _ffn_flags(i, ncr, nT)
        bufs = (e0, e1, e2)

        def run(is_ctx):
            _fill_ext(bufs, g, gp, gn, has_prev, has_next, is_ctx)
            for r0 in range(0, R, SUB):
                for c0 in range(0, cb, FLANES):
                    gc = _conv9(bufs, w, r0, c0, is_ctx, False) + bb[:, pl.ds(c0, FLANES)]
                    val = v[r0:r0 + SUB, c0:c0 + FLANES].astype(F32)
                    out[r0:r0 + SUB, c0:c0 + FLANES] = (gc * _sig(gc) * val).astype(BF16)
                    gc_out[r0:r0 + SUB, c0:c0 + FLANES] = gc.astype(BF16)

        pl.when(i < ncr)(lambda: run(True))
        pl.when(i >= ncr)(lambda: run(False))

    sds = jax.ShapeDtypeStruct((T, CF), BF16)
    return pl.pallas_call(
        body, grid=(nT, ncb), in_specs=[cur(0), cur(ncb), hp(0), hn(0), vec(16), vec(1)], out_specs=[cur(0), cur(0)],
        out_shape=[sds, sds],
        scratch_shapes=_ffn_scratch(cb, 1), compiler_params=_cp(2), name=name)(up, up, up, up, w16, b)


def ffn_act_bwd1(up, gcs, dact, CF, name):
    T = up.shape[0]
    cb = _lane_tile(CF, FFN_COLS)
    ncb = CF // cb

    def body(v, gc_ref, da, dgc, dup, accb):
        @pl.when(pl.program_id(1) == 0)
        def _():
            accb[...] = jnp.zeros_like(accb)

        for r0 in range(0, R, SUB):
            for c0 in range(0, cb, FLANES):
                blk = (slice(r0, r0 + SUB), slice(c0, c0 + FLANES))
                gc = gc_ref[blk].astype(F32)
                sg = _sig(gc)
                dav = da[blk].astype(F32)
                dup[blk] = (dav * gc * sg).astype(BF16)
                d = dav * v[blk].astype(F32) * (sg * (1.0 + gc * (1.0 - sg)))
                dgc[blk] = d.astype(BF16)
                accb[0:1, c0:c0 + FLANES] += jnp.sum(d, axis=0, keepdims=True)

    blk = lambda c0: pl.BlockSpec((R, cb), lambda j, i: (i, c0 + j))
    return pl.pallas_call(
        body, grid=(ncb, T // R), in_specs=[blk(ncb), blk(0), blk(0)],
        out_specs=[blk(0), blk(ncb), pl.BlockSpec((8, cb), lambda j, i: (0, j))],
        out_shape=[jax.ShapeDtypeStruct((T, CF), BF16), jax.ShapeDtypeStruct((T, 2 * CF), BF16),
                   jax.ShapeDtypeStruct((8, CF), F32)],
        compiler_params=_cp(2), name=name)(up, gcs, dact)


def ffn_act_bwd2(dgc, up, dup, w16, CF, ncr, name):
    T = up.shape[0]
    nT, cb = T // R, _lane_tile(CF, FFN_COLS)
    ncb = CF // cb
    cur, hp, hn, vec = _ffn_specs(T, cb, 'ji')

    def body(d, dp_, dn_, g, gp, gn, w, dup_in, dgate, accw, d0, d1, d2, gplain):
        i = pl.program_id(1)

        @pl.when(i == 0)
        def _():
            accw[...] = jnp.zeros_like(accw)

        has_prev, has_next = _ffn_flags(i, ncr, nT)
        dbufs = (d0, d1, d2)
        _fill_plain(gplain, g, gp, gn, has_prev, has_next)

        def run(is_ctx):
            _fill_ext(dbufs, d, dp_, dn_, has_prev, has_next, is_ctx)
            for r0 in range(0, R, SUB):
                for c0 in range(0, cb, FLANES):
                    dgate[r0:r0 + SUB, c0:c0 + FLANES] = _conv9(dbufs, w, r0, c0, is_ctx, True).astype(BF16)
                    for dr, dc in _taps(is_ctx):
                        widx = (dr + 1) * 3 + dc + 1
                        dmov = _tap_src(dbufs, r0, c0, 0, -dc)
                        gsrc = gplain[pl.ds(FPAD + GRID_W + r0 + GRID_W * dr, SUB), pl.ds(c0, FLANES)]
                        accw[widx:widx + 1, pl.ds(c0, FLANES)] += jnp.sum(dmov * gsrc, axis=0, keepdims=True)

        pl.when(i < ncr)(lambda: run(True))
        pl.when(i >= ncr)(lambda: run(False))

    return pl.pallas_call(
        body, grid=(ncb, nT),
        in_specs=[cur(0), hp(0), hn(0), cur(0), hp(0), hn(0), vec(16), pl.BlockSpec(memory_space=pl.ANY)],
        out_specs=[cur(0), vec(16)],
        out_shape=[jax.ShapeDtypeStruct((T, 2 * CF), BF16), jax.ShapeDtypeStruct((16, CF), F32)],
        input_output_aliases={7: 0},
        scratch_shapes=_ffn_scratch(cb, 1) + [pltpu.VMEM((R + 2 * GRID_W + 2 * FPAD, cb), F32)],
        compiler_params=_cp(2), name=name)(
            dgc, dgc, dgc, up, up, up, w16, dup)


def mod_fwd(cs, w_mod, name):
    L, D, Ns = w_mod.shape
    tn = _lane_tile(Ns, 768)

    def body(c_ref, w_ref, o_ref):
        cv = c_ref[...]
        o_ref[...] = _dot((cv * _sig(cv)).astype(BF16), w_ref[...].astype(BF16))

    return pl.pallas_call(
        body, grid=(L, Ns // tn),
        in_specs=[pl.BlockSpec((16, D), lambda l, j: (0, 0)), pl.BlockSpec((None, D, tn), lambda l, j: (l, 0, j))],
        out_specs=pl.BlockSpec((None, 16, tn), lambda l, j: (l, 0, j)),
        out_shape=jax.ShapeDtypeStruct((L, 16, Ns), F32), compiler_params=_cp(2), name=name)(cs, w_mod)


def mod_bwd(cs, w_mod, dmod, name):
    L, D, Ns = w_mod.shape
    tn = _lane_tile(Ns, 768)

    def body(c_ref, w_ref, dm_ref, gw_ref, ds_ref):
        @pl.when(jnp.logical_and(pl.program_id(0) == 0, pl.program_id(1) == 0))
        def _():
            ds_ref[...] = jnp.zeros_like(ds_ref)

        cv = c_ref[...]
        dm = dm_ref[...].astype(BF16)
        gw_ref[...] = _dot_tn((cv * _sig(cv)).astype(BF16), dm)
        ds_ref[...] += _dot_nt(dm, w_ref[...].astype(BF16))

    return pl.pallas_call(
        body, grid=(L, Ns // tn),
        in_specs=[pl.BlockSpec((16, D), lambda l, j: (0, 0)), pl.BlockSpec((None, D, tn), lambda l, j: (l, 0, j)),
                  pl.BlockSpec((None, 16, tn), lambda l, j: (l, 0, j))],
        out_specs=[pl.BlockSpec((None, D, tn), lambda l, j: (l, 0, j)), pl.BlockSpec((16, D), lambda l, j: (0, 0))],
        out_shape=[jax.ShapeDtypeStruct((L, D, Ns), F32), jax.ShapeDtypeStruct((16, D), F32)],
        compiler_params=_cp(2), name=name)(cs, w_mod, dmod)


def cast_bf16(w, name):
    L, Kb, Nb = w.shape
    w2 = w.reshape(L * Kb, Nb)
    tr = _row_tile(L * Kb, Nb, 1 << 19)

    def body(w_ref, o_ref):
        o_ref[...] = w_ref[...].astype(BF16)

    spec = pl.BlockSpec((tr, Nb), lambda i: (i, 0))
    out = pl.pallas_call(body, grid=(L * Kb // tr,), in_specs=[spec], out_specs=spec,
                         out_shape=jax.ShapeDtypeStruct((L * Kb, Nb), BF16), compiler_params=_cp(1), name=name)(w2)
    return out.reshape(L, Kb, Nb)


def add_half(dw, recv, c_idx, name):
    S, Kb, Nb = dw.shape
    Kh = Kb // 2
    tr = _row_tile(Kh, Nb, 1 << 19)
    nb = Kh // tr

    def body(c_ref, a_ref, b_ref, o_ref):
        o_ref[...] = (a_ref[...].astype(F32) + b_ref[...].astype(F32)).astype(BF16)

    return pl.pallas_call(
        body,
        grid_spec=pltpu.PrefetchScalarGridSpec(
            num_scalar_prefetch=1, grid=(S, nb),
            in_specs=[pl.BlockSpec((None, tr, Nb), lambda s, i, c: (s, c[0] * nb + i, 0)),
                      pl.BlockSpec((None, tr, Nb), lambda s, i, c: (s, i, 0))],
            out_specs=pl.BlockSpec((None, tr, Nb), lambda s, i, c: (s, i, 0))),
        out_shape=jax.ShapeDtypeStruct((S, Kh, Nb), BF16), compiler_params=_cp(2), name=name)(c_idx, dw, recv)


def add_shards(mine, sib, c_idx, name):
    S, Kh, Nb = mine.shape
    tr = _row_tile(Kh, Nb, 1 << 18)
    nb = Kh // tr

    def body(c_ref, m_ref, s_ref, o_ref):
        def total(p_ref):
            acc = p_ref[0].astype(F32)
            for s in range(1, S):
                acc = acc + p_ref[s].astype(F32)
            o_ref[...] = acc

        pl.when(pl.program_id(0) == 0)(lambda: total(m_ref))
        pl.when(pl.program_id(0) == 1)(lambda: total(s_ref))

    out = pl.pallas_call(
        body,
        grid_spec=pltpu.PrefetchScalarGridSpec(
            num_scalar_prefetch=1, grid=(2, nb),
            in_specs=[pl.BlockSpec((S, tr, Nb), lambda h, i, c: (0, jnp.where(h == 0, i, nb - 1), 0)),
                      pl.BlockSpec((S, tr, Nb), lambda h, i, c: (0, jnp.where(h == 0, 0, i), 0))],
            out_specs=pl.BlockSpec((None, tr, Nb), lambda h, i, c: (jnp.where(h == 0, c[0], 1 - c[0]), i, 0))),
        out_shape=jax.ShapeDtypeStruct((2, Kh, Nb), F32), compiler_params=_cp(2), name=name)(c_idx, mine, sib)
    return out.reshape(2 * Kh, Nb)


def adamw(w, g, m, v, name):
    shape = w.shape
    cols = shape[-1]
    rows = w.size // cols
    w2, g2, m2, v2 = (t.reshape(rows, cols) for t in (w, g, m, v))
    tr = _row_tile(rows, cols, 3 << 17)

    def body(w_ref, g_ref, m_ref, v_ref, d_ref, nm_ref, nv_ref):
        gv = g_ref[...]
        nm = ADAM_B1 * m_ref[...] + (1.0 - ADAM_B1) * gv
        nv = ADAM_B2 * v_ref[...] + (1.0 - ADAM_B2) * jnp.square(gv)
        m_hat = nm / (1.0 - ADAM_B1 ** ADAM_STEP)
        v_hat = nv / (1.0 - ADAM_B2 ** ADAM_STEP)
        d_ref[...] = -ADAM_LR * (m_hat / (jnp.sqrt(v_hat) + ADAM_EPS) + ADAM_WD * w_ref[...])
        nm_ref[...] = nm
        nv_ref[...] = nv

    spec = pl.BlockSpec((tr, cols), lambda i: (i, 0))
    sds = jax.ShapeDtypeStruct((rows, cols), F32)
    d, nm, nv = pl.pallas_call(body, grid=(rows // tr,), in_specs=[spec] * 4, out_specs=[spec] * 3,
                               out_shape=[sds, sds, sds], compiler_params=_cp(1), name=name)(w2, g2, m2, v2)
    return d.reshape(shape), nm.reshape(shape), nv.reshape(shape)


def gather_sum(v, name):
    r, cols = v.shape

    def body(v_ref, g_ref, s_ref, send_sems, recv_sems, local_sem):
        x, y, c = _place()
        me = 4 * x + 2 * y + c
        mine = pltpu.make_async_copy(v_ref, g_ref.at[me], local_sem)
        mine.start()
        sends, peers = [], []
        for k in range(1, 8):
            px = 1 - x if k & 4 else x
            py = 1 - y if k & 2 else y
            pc = 1 - c if k & 1 else c
            cp = pltpu.make_async_remote_copy(src_ref=v_ref, dst_ref=g_ref.at[me], send_sem=send_sems.at[k - 1],
                                              recv_sem=recv_sems.at[k - 1], device_id=(px, py, pc), device_id_type=MESH)
            cp.start()
            sends.append(cp)
            peers.append((px, py, pc))
        for k, (px, py, pc) in enumerate(peers):
            pltpu.make_async_remote_copy(src_ref=v_ref, dst_ref=g_ref.at[4 * px + 2 * py + pc], send_sem=send_sems.at[k],
                                         recv_sem=recv_sems.at[k], device_id=(px, py, pc), device_id_type=MESH).wait_recv()
        for cp in sends:
            cp.wait_send()
        mine.wait()
        acc = g_ref[0]
        for d in range(1, 8):
            acc = acc + g_ref[d]
        s_ref[...] = acc

    vm = pl.BlockSpec(memory_space=pltpu.VMEM)
    return pl.pallas_call(
        body, in_specs=[vm], out_specs=[vm, vm],
        out_shape=[jax.ShapeDtypeStruct((8, r, cols), F32), jax.ShapeDtypeStruct((r, cols), F32)],
        scratch_shapes=[pltpu.SemaphoreType.DMA((7,)), pltpu.SemaphoreType.DMA((7,)), pltpu.SemaphoreType.DMA],
        compiler_params=_cp0(), name=name)(v)


def gather_weights(wb, name):
    L, Kb, Nb = wb.shape
    Kh = Kb // 2

    def body(w_ref, *rest):
        outs, (send_sems, recv_sems, local_sems) = rest[:L], rest[L:]
        x, y, c = _place()
        jm = 2 * x + y
        sib = (x, y, 1 - c)
        chips = _other_chips(x, y)

        def cp(l, t, src, dst, to):
            return pltpu.make_async_remote_copy(src_ref=src, dst_ref=dst, send_sem=send_sems.at[7 * l + t],
                                                recv_sem=recv_sems.at[7 * l + t], device_id=to, device_id_type=MESH)

        started, local = [], []
        for l in range(L):
            src = w_ref.at[l, pl.ds(c * Kh, Kh), :]
            dst = outs[l].at[jm, c]
            lc = pltpu.make_async_copy(src, dst, local_sems.at[l])
            lc.start()
            local.append(lc)
            for t, (px, py) in enumerate(chips):
                started.append(cp(l, t, src, dst, (px, py, c)))
            started.append(cp(l, 3, src, dst, sib))
            for s in started[-4:]:
                s.start()
        for l in range(L):
            for t, (px, py) in enumerate(chips):
                blk = outs[l].at[2 * px + py, c]
                cp(l, t, blk, blk, (px, py, c)).wait_recv()
                fwd = cp(l, 4 + t, blk, blk, sib)
                fwd.start()
                started.append(fwd)
        for l in range(L):
            blk = outs[l].at[jm, 1 - c]
            cp(l, 3, blk, blk, sib).wait_recv()
            for t, (px, py) in enumerate(chips):
                blk = outs[l].at[2 * px + py, 1 - c]
                cp(l, 4 + t, blk, blk, sib).wait_recv()
        for s in started:
            s.wait_send()
        for lc in local:
            lc.wait()

    hbm = pl.BlockSpec(memory_space=pl.ANY)
    outs = pl.pallas_call(
        body, in_specs=[hbm], out_specs=[hbm] * L,
        out_shape=[jax.ShapeDtypeStruct((4, 2, Kh, Nb), BF16)] * L,
        scratch_shapes=[pltpu.SemaphoreType.DMA((7 * L,)), pltpu.SemaphoreType.DMA((7 * L,)), pltpu.SemaphoreType.DMA((L,))],
        compiler_params=_cp0(), name=name)(wb)
    return [o.reshape(4, Kb, Nb) for o in outs]


def swap_halves(dw, name):
    S, Kb, Nb = dw.shape
    Kh = Kb // 2

    def body(d_ref, o_ref, send_sems, recv_sems):
        x, y, c = _place()
        sib = (x, y, 1 - c)
        cps = [pltpu.make_async_remote_copy(src_ref=d_ref.at[s, pl.ds((1 - c) * Kh, Kh), :], dst_ref=o_ref.at[s],
                                            send_sem=send_sems.at[s], recv_sem=recv_sems.at[s], device_id=sib,
                                            device_id_type=MESH) for s in range(S)]
        for cpy in cps:
            cpy.start()
        for cpy in cps:
            cpy.wait_recv()
        for cpy in cps:
            cpy.wait_send()

    hbm = pl.BlockSpec(memory_space=pl.ANY)
    return pl.pallas_call(
        body, in_specs=[hbm], out_specs=hbm, out_shape=jax.ShapeDtypeStruct((S, Kh, Nb), dw.dtype),
        scratch_shapes=[pltpu.SemaphoreType.DMA((S,)), pltpu.SemaphoreType.DMA((S,))],
        compiler_params=_cp0(), name=name)(dw)


def share_parts(parts, name):
    S = parts.shape[0]

    def body(p_ref, o_ref, send_sems, recv_sems):
        x, y, c = _place()
        cps = [pltpu.make_async_remote_copy(src_ref=p_ref.at[s], dst_ref=o_ref.at[s], send_sem=send_sems.at[s],
                                            recv_sem=recv_sems.at[s], device_id=(x, y, 1 - c), device_id_type=MESH)
               for s in range(S)]
        for cpy in cps:
            cpy.start()
        for cpy in cps:
            cpy.wait_recv()
        for cpy in cps:
            cpy.wait_send()

    hbm = pl.BlockSpec(memory_space=pl.ANY)
    return pl.pallas_call(
        body, in_specs=[hbm], out_specs=hbm, out_shape=jax.ShapeDtypeStruct(parts.shape, parts.dtype),
        scratch_shapes=[pltpu.SemaphoreType.DMA((S,)), pltpu.SemaphoreType.DMA((S,))],
        compiler_params=_cp0(), name=name)(parts)


def chip_sums(dw, c_idx, tag):
    return add_half(dw, swap_halves(dw, f"rs_swap_{tag}"), c_idx, f"rs_add_half_{tag}")


def finish_grad(parts, c_idx, tag):
    return add_shards(parts, share_parts(parts, f"rs_share_{tag}"), c_idx, f"rs_add_shards_{tag}")


def _pack(parts):
    flat = jnp.concatenate([t.reshape(-1).astype(F32) for t in parts])
    n = flat.shape[0]
    pad = (-n) % 1024
    return jnp.pad(flat, (0, pad)).reshape(-1, 128)


def _unpack(buf, shapes):
    flat = buf.reshape(buf.shape[:-2] + (-1,))
    out, o = [], 0
    for s in shapes:
        n = 1
        for d in s:
            n *= d
        out.append(flat[..., o:o + n].reshape(buf.shape[:-2] + tuple(s)))
        o += n
    return out


def _rope_tables(seq, ctx_len):
    t = jnp.arange(seq)
    inv = 1.0 / (ROPE_THETA ** (jnp.arange(0, DH // 4, dtype=F32) / (DH // 4)))
    ar = (t // GRID_W).astype(F32)[:, None] * inv[None, :]
    ac = (t % GRID_W).astype(F32)[:, None] * inv[None, :]
    cos = jnp.concatenate([jnp.cos(ar), jnp.cos(ar), jnp.cos(ac), jnp.cos(ac)], axis=-1)
    sin = jnp.concatenate([-jnp.sin(ar), jnp.sin(ar), -jnp.sin(ac), jnp.sin(ac)], axis=-1)
    return (jnp.concatenate([jnp.ones((ctx_len, DH), F32), cos], axis=0),
            jnp.concatenate([jnp.zeros((ctx_len, DH), F32), sin], axis=0))


def kernel(x, c, ctx, c_ctx, w_mod, b_mod, norm1_g, norm2_g, w_in, ret_decay_f, ret_decay_b, conv_dw_w, conv_dw_b, conv_ln_g, conv_ln_b, w_out, ffn_w_up, ffn_dw_w, ffn_dw_b, ffn_w_down, final_norm_g, loss_target, m_c_ctx, m_w_mod, m_b_mod, m_norm1_g, m_norm2_g, m_w_in, m_ret_decay_f, m_ret_decay_b, m_conv_dw_w, m_conv_dw_b, m_conv_ln_g, m_conv_ln_b, m_w_out, m_ffn_w_up, m_ffn_dw_w, m_ffn_dw_b, m_ffn_w_down, m_final_norm_g, v_c_ctx, v_w_mod, v_b_mod, v_norm1_g, v_norm2_g, v_w_in, v_ret_decay_f, v_ret_decay_b, v_conv_dw_w, v_conv_dw_b, v_conv_ln_g, v_conv_ln_b, v_w_out, v_ffn_w_up, v_ffn_dw_w, v_ffn_dw_b, v_ffn_w_down, v_final_norm_g):
    _, SEQ, D = x.shape
    CTX = ctx.shape[1]
    L = w_in.shape[0]
    CWs = conv_dw_w.shape[2]
    CW = 4 * CWs
    RW = 4 * w_out.shape[1] - CW
    H = RW // DH
    CFs = ffn_dw_w.shape[-1]
    CF = 4 * CFs
    NMs = w_mod.shape[2]
    T = CTX + SEQ
    ncr, ncc = CTX // R, CTX // RC
    assert CTX == R and RW == CW and RW % DH == 0 and SEQ % R == 0 and R % GRID_W == 0
    assert w_in.shape[2] * 4 == 4 * RW + 2 * CW and NMs * 4 == N_MOD * D

    mx, my, mc = _place()
    me = 4 * mx + 2 * my + mc
    jm = 2 * mx + my
    c_idx = jnp.reshape(mc, (1,)).astype(jnp.int32)

    shapes0 = [(D,), (L, CONV_K, CWs), (L, 9, CFs)]
    g0, _ = gather_sum(_pack([c[0], conv_dw_w, ffn_dw_w.reshape(L, 9, CFs)]), "gather_cond")
    c_all, cw_all, fw_all = _unpack(g0, shapes0)
    conv_w = jnp.concatenate([cw_all[2 * j] for j in range(4)], axis=-1)
    ffn_w = jnp.concatenate([fw_all[2 * j] for j in range(4)], axis=-1)
    conv_w32 = jnp.pad(conv_w, ((0, 0), (0, 32 - CONV_K), (0, 0)))
    ffn_w16 = jnp.pad(ffn_w, ((0, 0), (0, 7), (0, 0)))
    cs = jnp.concatenate([c_all, c_ctx[None, :], jnp.zeros((7, D), F32)], axis=0)
    mod_shard = mod_fwd(cs, w_mod, "mod_fwd")
    g1, _ = gather_sum(mod_shard.reshape(-1, 128), "gather_mod")
    mod_all = g1.reshape(8, L, 16, NMs)
    mod_full = jnp.concatenate([mod_all[2 * j] for j in range(4)], axis=-1) + b_mod[:, None, :]
    mod_mine = lax.dynamic_index_in_dim(mod_full, me, axis=1, keepdims=False)
    modv = jnp.stack([mod_full[:, 8], mod_mine], axis=1).reshape(L, 2, N_MOD, D)

    big = {"w_in": (w_in, True), "w_out": (w_out, False), "w_up": (ffn_w_up, True), "w_down": (ffn_w_down, False)}
    wb = {k: cast_bf16(w, f"cast_{k}") for k, (w, _) in big.items()}
    wg = {k: [gather_weights(wb[k][0:1], f"gather_{k}")[0]] for k in big}
    nxt = lambda k, l: wb[k][l + 1] if l + 1 < L else None

    def wmat(k, l):
        w = wg[k][l]
        return w if big[k][1] else w.reshape(1, w.shape[0] * w.shape[1], w.shape[2])

    cosT, sinT = _rope_tables(SEQ, CTX)
    lgt = [jnp.broadcast_to(jnp.concatenate([jax.nn.log_sigmoid(ret_decay_f[l]), jax.nn.log_sigmoid(ret_decay_b[l])])[:, None],
                            (2 * H, DH)) for l in range(L)]
    row = lambda t: t.reshape(1, -1)

    def project(a, k, l, out_dtype, name, res=None):
        y, xn, gathered = mm_nn(a, wmat(k, l), out_dtype, name, nxt(k, l), res)
        if gathered is not None:
            wg[k].append(gathered)
        return y, xn

    gbig = {k: [None] * L for k in big}
    waiting = []

    def back(a, dy, k, l, out_dtype, tag):
        dw = mm_tn(a, dy, 4 if big[k][1] else 1, f"mm_{tag}_dw")
        dw = dw if big[k][1] else dw.reshape(4, dw.shape[1] // 4, dw.shape[2])
        prev = waiting.pop() if waiting else None
        da, parts, shared = mm_nt(dy, wmat(k, l), out_dtype, f"mm_{tag}_dx", chip_sums(dw, c_idx, k),
                                  prev[0] if prev else None)
        if prev:
            gbig[prev[1]][prev[2]] = add_shards(prev[0], shared, c_idx, f"rs_add_shards_{prev[1]}")
        waiting.append((parts, k, l))
        return da

    xs = jnp.concatenate([ctx[0], x[0]], axis=0)
    saved = []
    for l in range(L):
        h1 = norm_mod(xs, row(norm1_g[l]), modv[l], 0, 1, ncr, "norm_mod")
        p, _ = project(h1, "w_in", l, BF16, "mm_in")
        uc = conv31_fwd(p, conv_w32[l], row(conv_dw_b[l]), RW, CW, ncr, "conv31_fwd")
        o_f, o_b, stf, stb = ret_fwd(p, cosT, sinT, lgt[l], H, ncc, "ret_fwd")
        mix = mix_fwd(o_f, o_b, p, uc, row(conv_ln_g[l]), row(conv_ln_b[l]), H, "mix_fwd")
        y1, x2 = project(mix, "w_out", l, BF16, "mm_out", (xs, modv[l][:, 2], CTX))
        h2 = norm_mod(x2, row(norm2_g[l]), modv[l], 3, 4, ncr, "norm_mod")
        up, _ = project(h2, "w_up", l, BF16, "mm_up")
        act, gcs = ffn_act(up, ffn_w16[l], row(ffn_dw_b[l]), CF, ncr, "ffn_act")
        y2, x3 = project(act, "w_down", l, BF16, "mm_down", (x2, modv[l][:, 5], CTX))
        saved.append((xs, h1, p, uc, o_f, o_b, stf, stb, mix, y1, x2, h2, up, act, gcs, y2))
        xs = x3

    dx, acc_loss = loss_head(xs, loss_target[0], row(final_norm_g), ncr, "loss_head")
    loss = lax.psum(0.5 / D * jnp.sum(acc_loss[1]), ("x", "y", "c"))

    small = [None] * L
    for l in reversed(range(L)):
        x1, h1, p, uc, o_f, o_b, stf, stb, mix, y1, x2, h2, up, act, gcs, y2 = saved[l]
        dy2, ag2 = gate_res_bwd(dx, y2, modv[l], 5, ncr, "gate_res_bwd")
        dact = back(act, dy2, "w_down", l, BF16, "down")
        dgc, dup, accb = ffn_act_bwd1(up, gcs, dact, CF, "ffn_act_bwd1")
        dup, accfw = ffn_act_bwd2(dgc, up, dup, ffn_w16[l], CF, ncr, "ffn_act_bwd2")
        dh2 = back(h2, dup, "w_up", l, F32, "up")
        dx2, an2 = norm_mod_bwd(x2, row(norm2_g[l]), modv[l], dh2, dx, 4, ncr, "norm_mod_bwd")
        dy1, ag1 = gate_res_bwd(dx2, y1, modv[l], 2, ncr, "gate_res_bwd")
        dmix = back(mix, dy1, "w_out", l, BF16, "out")
        do, dg, duc, accln = mix_bwd(dmix, o_f, o_b, p, uc, row(conv_ln_g[l]), row(conv_ln_b[l]), H, "mix_bwd")
        da, dbg, acccw = conv31_bwd(duc, p, conv_w32[l], RW, CW, ncr, "conv31_bwd")
        dqf, dkf, dvf, dqb, dkb, dvb, glg = ret_bwd(p, do, cosT, sinT, lgt[l], stf, stb, H, ncc, "ret_bwd")
        dp = assemble_dp(dqf, dqb, dkf, dkb, dvf, dvb, dg, da, dbg, "assemble_dp")
        dh1 = back(h1, dp, "w_in", l, F32, "in")
        dx, an1 = norm_mod_bwd(x1, row(norm1_g[l]), modv[l], dh1, dx2, 1, ncr, "norm_mod_bwd")
        dmod = jnp.stack([jnp.stack([an1[0], an1[1], ag1[0], an2[0], an2[1], ag2[0]]),
                          jnp.stack([an1[2], an1[3], ag1[1], an2[2], an2[3], ag2[1]])])
        dlg = jnp.sum(glg, axis=(1, 2))
        dth = dlg * jnp.concatenate([jax.nn.sigmoid(-ret_decay_f[l]), jax.nn.sigmoid(-ret_decay_b[l])])
        small[l] = [dmod, an1[4], an2[4], acccw[31], accln[0], accln[1], acccw[:CONV_K], accfw[:9], accb[0], dth]
    parts, k, l = waiting.pop()
    gbig[k][l] = finish_grad(parts, c_idx, k)
    grad_x = dx[CTX:][None]

    shapes1 = [(2, N_MOD, D), (D,), (D,), (CW,), (CW,), (CW,), (CONV_K, CW), (9, CF), (CF,), (2 * H,)]
    flat_parts = [t for l in range(L) for t in small[l]] + [acc_loss[0]]
    g2, s2 = gather_sum(_pack(flat_parts), "gather_small_grads")
    sums = _unpack(s2, shapes1 * L + [(D,)])
    per_dev = _unpack(g2, shapes1 * L + [(D,)])
    nS = len(shapes1)
    col = lambda i: jnp.stack([sums[l * nS + i] for l in range(L)])
    dmod_sum = col(0)
    dmod_dev = jnp.stack([per_dev[l * nS] for l in range(L)], axis=0)
    g_b_mod = (dmod_sum[:, 0] + dmod_sum[:, 1]).reshape(L, N_MOD * D)
    g_norm1, g_norm2 = col(1), col(2)
    g_conv_b, g_ln_g, g_ln_b = col(3), col(4), col(5)
    g_conv_w = lax.dynamic_slice_in_dim(col(6), jm * CWs, CWs, axis=2)
    g_ffn_w = lax.dynamic_slice_in_dim(col(7), jm * CFs, CFs, axis=2).reshape(L, 3, 3, CFs)
    g_ffn_b = col(8)
    g_ret = col(9)
    g_final = sums[-1]

    dmod_rows = jnp.concatenate([dmod_dev[:, :, 1].reshape(L, 8, N_MOD * D), dmod_sum[:, 0].reshape(L, 1, N_MOD * D),
                                 jnp.zeros((L, 7, N_MOD * D), F32)], axis=1)
    dmod_shard = lax.dynamic_slice_in_dim(dmod_rows, jm * NMs, NMs, axis=2)
    g_w_mod, ds_part = mod_bwd(cs, w_mod, dmod_shard, "mod_bwd")
    _, ds_sum = gather_sum(ds_part.reshape(-1, 128), "gather_dsilu")
    ds_ctx = 0.5 * ds_sum.reshape(16, D)[8]
    sg = jax.nn.sigmoid(c_ctx)
    g_c_ctx = ds_ctx * (sg * (1.0 + c_ctx * (1.0 - sg)))

    grads = {
        "c_ctx": g_c_ctx, "w_mod": g_w_mod, "b_mod": g_b_mod, "norm1_g": g_norm1, "norm2_g": g_norm2,
        "w_in": jnp.stack(gbig["w_in"]), "ret_decay_f": g_ret[:, :H], "ret_decay_b": g_ret[:, H:],
        "conv_dw_w": g_conv_w, "conv_dw_b": g_conv_b, "conv_ln_g": g_ln_g, "conv_ln_b": g_ln_b,
        "w_out": jnp.stack(gbig["w_out"]), "ffn_w_up": jnp.stack(gbig["w_up"]), "ffn_dw_w": g_ffn_w,
        "ffn_dw_b": g_ffn_b, "ffn_w_down": jnp.stack(gbig["w_down"]), "final_norm_g": g_final,
    }
    params = {
        "c_ctx": (c_ctx, m_c_ctx, v_c_ctx), "w_mod": (w_mod, m_w_mod, v_w_mod), "b_mod": (b_mod, m_b_mod, v_b_mod),
        "norm1_g": (norm1_g, m_norm1_g, v_norm1_g), "norm2_g": (norm2_g, m_norm2_g, v_norm2_g),
        "w_in": (w_in, m_w_in, v_w_in), "ret_decay_f": (ret_decay_f, m_ret_decay_f, v_ret_decay_f),
        "ret_decay_b": (ret_decay_b, m_ret_decay_b, v_ret_decay_b),
        "conv_dw_w": (conv_dw_w, m_conv_dw_w, v_conv_dw_w), "conv_dw_b": (conv_dw_b, m_conv_dw_b, v_conv_dw_b),
        "conv_ln_g": (conv_ln_g, m_conv_ln_g, v_conv_ln_g), "conv_ln_b": (conv_ln_b, m_conv_ln_b, v_conv_ln_b),
        "w_out": (w_out, m_w_out, v_w_out), "ffn_w_up": (ffn_w_up, m_ffn_w_up, v_ffn_w_up),
        "ffn_dw_w": (ffn_dw_w, m_ffn_dw_w, v_ffn_dw_w), "ffn_dw_b": (ffn_dw_b, m_ffn_dw_b, v_ffn_dw_b),
        "ffn_w_down": (ffn_w_down, m_ffn_w_down, v_ffn_w_down),
        "final_norm_g": (final_norm_g, m_final_norm_g, v_final_norm_g),
    }
    names = list(params)
    upd = {n: adamw(params[n][0], grads[n], params[n][1], params[n][2], f"adamw_{n}") for n in names}
    return (loss, grad_x, *[grads[n] for n in names], *[upd[n][0] for n in names],
            *[upd[n][1] for n in names], *[upd[n][2] for n in names])
```

```python
import jax
import jax.numpy as jnp
from jax import lax
from jax.experimental import pallas as pl
from jax.experimental.pallas import tpu as pltpu

F32 = jnp.float32
BF16 = jnp.bfloat16
EPS = 1e-6
DH = 128
RC = 128
GRID_W = 64
ROPE_THETA = 10000.0
N_MOD = 6
R = 256
HALO = 16
CONV_K = 31
SUB = 64
LANES = 256
FLANES = 128
FFN_COLS = 1408
VMEM_LIMIT = 48 * 1024 * 1024
MESH = pl.DeviceIdType.MESH
ADAM_LR, ADAM_B1, ADAM_B2, ADAM_EPS, ADAM_WD, ADAM_STEP = 0.001, 0.9, 0.999, 1e-08, 0.01, 10


def _cp(n):
    return pltpu.CompilerParams(dimension_semantics=("arbitrary",) * n, vmem_limit_bytes=VMEM_LIMIT)


def _cp0():
    return pltpu.CompilerParams(vmem_limit_bytes=VMEM_LIMIT)


def _sig(v):
    return 1.0 / (1.0 + jnp.exp(-v))


def _dot(a, b):
    return jnp.dot(a, b, preferred_element_type=F32)


def _dot_nt(a, b):
    return lax.dot_general(a, b, (((1,), (1,)), ((), ())), preferred_element_type=F32)


def _dot_tn(a, b):
    return lax.dot_general(a, b, (((0,), (0,)), ((), ())), preferred_element_type=F32)


def _lane_tile(n, cap):
    t = (min(n, cap) // 128) * 128
    while t >= 128:
        if n % t == 0:
            return t
        t -= 128
    raise ValueError(f"no lane tile for {n}")


def _row_tile(rows, cols, max_elems):
    if rows * cols <= max_elems:
        return rows
    t = (min(rows, max(8, max_elems // cols)) // 8) * 8
    while t >= 8:
        if rows % t == 0:
            return t
        t -= 8
    raise ValueError(f"no row tile for {rows}x{cols}")


def _place():
    return lax.axis_index("x"), lax.axis_index("y"), lax.axis_index("c")


def _other_chips(x, y):
    return [(1 - x, y), (x, 1 - y), (1 - x, 1 - y)]


def _first_last(grid):
    ids = [pl.program_id(k) for k in range(len(grid))]
    first, last = ids[0] == 0, ids[0] == grid[0] - 1
    for k in range(1, len(grid)):
        first = jnp.logical_and(first, ids[k] == 0)
        last = jnp.logical_and(last, ids[k] == grid[k] - 1)
    return first, last


def _side_copies(src_slab, dst_ref, send_sems, recv_sems, local_sem):
    x, y, c = _place()
    jm = 2 * x + y
    chips = _other_chips(x, y)
    mine = pltpu.make_async_copy(src_slab(jm), dst_ref.at[jm], local_sem)
    sends = [pltpu.make_async_remote_copy(src_ref=src_slab(2 * px + py), dst_ref=dst_ref.at[jm], send_sem=send_sems.at[t],
                                          recv_sem=recv_sems.at[t], device_id=(px, py, c), device_id_type=MESH)
             for t, (px, py) in enumerate(chips)]
    recvs = [pltpu.make_async_remote_copy(src_ref=dst_ref.at[2 * px + py], dst_ref=dst_ref.at[2 * px + py],
                                          send_sem=send_sems.at[t], recv_sem=recv_sems.at[t], device_id=(px, py, c),
                                          device_id_type=MESH) for t, (px, py) in enumerate(chips)]
    return mine, sends, recvs


def _side_start(mine, sends, recvs):
    mine.start()
    for s in sends:
        s.start()


def _side_finish(mine, sends, recvs):
    for r in recvs:
        r.wait_recv()
    for s in sends:
        s.wait_send()
    mine.wait()


_SIDE_SEMS = [pltpu.SemaphoreType.DMA((3,)), pltpu.SemaphoreType.DMA((3,)), pltpu.SemaphoreType.DMA]


def mm_nn(a, b, out_dtype, name, bcast=None, res=None):
    M, K = a.shape
    S, Kb, Nb = b.shape
    N = S * Nb
    assert K == Kb
    tm = _lane_tile(M, 768 if res is not None else 1408)
    tn, tk = _lane_tile(Nb, 1536), _lane_tile(K, 2816)
    nk = K // tk
    grid = (M // tm, N // tn, nk)
    tps = Nb // tn
    b_map = lambda i, j, k: (j // tps, k, j % tps)
    n_in = 2 + (2 if res is not None else 0) + (1 if bcast is not None else 0)
    n_out = 1 + (1 if res is not None else 0) + (1 if bcast is not None else 0)

    def body(*refs):
        ins, outs, scr = list(refs[:n_in]), list(refs[n_in:n_in + n_out]), list(refs[n_in + n_out:])
        a_ref, b_ref = ins[0], ins[1]
        o_ref = outs[0]
        if nk > 1:
            acc_ref = scr.pop(0)
        if bcast is not None:
            first, last = _first_last(grid)
            side = _side_copies(lambda j: ins[-1], outs[-1], *scr)
            pl.when(first)(lambda: _side_start(*side))
        k = pl.program_id(2)

        def finish(acc):
            o_ref[...] = acc.astype(o_ref.dtype)
            if res is not None:
                x_ref, g_ref = ins[2], ins[3]
                rows = pl.program_id(0) * tm + lax.broadcasted_iota(jnp.int32, (tm, 1), 0)
                gate = jnp.where(rows < res[2], g_ref[0:1, :], g_ref[1:2, :])
                outs[1][...] = x_ref[...] + gate * acc

        if nk == 1:
            finish(_dot(a_ref[...], b_ref[...]))
        else:
            @pl.when(k == 0)
            def _():
                acc_ref[...] = jnp.zeros_like(acc_ref)

            acc_ref[...] += _dot(a_ref[...], b_ref[...])
            pl.when(k == nk - 1)(lambda: finish(acc_ref[...]))

        if bcast is not None:
            pl.when(last)(lambda: _side_finish(*side))

    hbm = pl.BlockSpec(memory_space=pl.ANY)
    tile = pl.BlockSpec((tm, tn), lambda i, j, k: (i, j))
    in_specs = [pl.BlockSpec((tm, tk), lambda i, j, k: (i, k)), pl.BlockSpec((None, tk, tn), b_map)]
    out_specs = [tile]
    out_shape = [jax.ShapeDtypeStruct((M, N), out_dtype)]
    scratch = [pltpu.VMEM((tm, tn), F32)] if nk > 1 else []
    args = [a, b]
    if res is not None:
        in_specs += [tile, pl.BlockSpec((2, tn), lambda i, j, k: (0, j))]
        out_specs.append(tile)
        out_shape.append(jax.ShapeDtypeStruct((M, N), F32))
        args += [res[0], res[1]]
    if bcast is not None:
        in_specs.append(hbm)
        out_specs.append(hbm)
        out_shape.append(jax.ShapeDtypeStruct((4,) + bcast.shape, bcast.dtype))
        scratch += _SIDE_SEMS
        args.append(bcast)
    got = pl.pallas_call(body, grid=grid, in_specs=in_specs, out_specs=out_specs, out_shape=out_shape,
                         scratch_shapes=scratch, compiler_params=_cp(3), name=name)(*args)
    return got[0], (got[1] if res is not None else None), (got[-1] if bcast is not None else None)


def mm_nt(a, b, out_dtype, name, exch, share=None):
    M, N = a.shape
    S, K, Nb = b.shape
    assert N == S * Nb
    tm, tko, tnr = _lane_tile(M, 1408), _lane_tile(K, 1536), _lane_tile(Nb, 1536)
    nr = N // tnr
    grid = (M // tm, K // tko, nr)
    tps = Nb // tnr
    b_map = lambda i, j, r: (r // tps, j, r % tps)
    n_side = 1 if share is None else 2

    def body(*refs):
        a_ref, b_ref, e_ref = refs[:3]
        o_ref, p_ref = refs[2 + n_side], refs[3 + n_side]
        scr = refs[2 + 2 * n_side + 1:]
        acc_ref, send_sems, recv_sems, local_sem = scr[:4]
        first, last = _first_last(grid)
        side = _side_copies(lambda j: e_ref.at[j], p_ref, send_sems, recv_sems, local_sem)
        if share is not None:
            s_ref, q_ref, ssend, srecv = refs[3], refs[4 + n_side], scr[4], scr[5]
            x, y, c = _place()
            to_sib = [pltpu.make_async_remote_copy(src_ref=s_ref.at[s], dst_ref=q_ref.at[s], send_sem=ssend.at[s],
                                                   recv_sem=srecv.at[s], device_id=(x, y, 1 - c), device_id_type=MESH)
                      for s in range(share.shape[0])]

        @pl.when(first)
        def _():
            _side_start(*side)
            if share is not None:
                for cpy in to_sib:
                    cpy.start()

        r = pl.program_id(2)

        @pl.when(r == 0)
        def _():
            acc_ref[...] = jnp.zeros_like(acc_ref)

        acc_ref[...] += _dot_nt(a_ref[...], b_ref[...])

        @pl.when(r == nr - 1)
        def _():
            o_ref[...] = acc_ref[...].astype(o_ref.dtype)

        @pl.when(last)
        def _():
            _side_finish(*side)
            if share is not None:
                for cpy in to_sib:
                    cpy.wait_recv()
                for cpy in to_sib:
                    cpy.wait_send()

    hbm = pl.BlockSpec(memory_space=pl.ANY)
    sides = [exch] + ([] if share is None else [share])
    in_specs = [pl.BlockSpec((tm, tnr), lambda i, j, r: (i, r)), pl.BlockSpec((None, tko, tnr), b_map)] + [hbm] * n_side
    out_specs = [pl.BlockSpec((tm, tko), lambda i, j, r: (i, j))] + [hbm] * n_side
    out_shape = [jax.ShapeDtypeStruct((M, K), out_dtype)] + [jax.ShapeDtypeStruct(t.shape, t.dtype) for t in sides]
    scratch = [pltpu.VMEM((tm, tko), F32)] + _SIDE_SEMS
    if share is not None:
        scratch += [pltpu.SemaphoreType.DMA((share.shape[0],)), pltpu.SemaphoreType.DMA((share.shape[0],))]
    got = pl.pallas_call(body, grid=grid, in_specs=in_specs, out_specs=out_specs, out_shape=out_shape,
                         scratch_shapes=scratch, compiler_params=_cp(3), name=name)(a, b, *sides)
    return got[0], got[1], (got[2] if share is not None else None)


def mm_tn(a, c, S, name):
    M, K = a.shape
    N = c.shape[1]
    Kb, Nb = K, N // S
    tm, tk, tn = _lane_tile(M, 1408), _lane_tile(Kb, 1536), _lane_tile(Nb, 1536)
    nm = M // tm
    tps = Nb // tn
    o_map = lambda i, j, m: (j // tps, i, j % tps)

    def body(a_ref, c_ref, o_ref, acc_ref):
        m = pl.program_id(2)

        @pl.when(m == 0)
        def _():
            acc_ref[...] = jnp.zeros_like(acc_ref)

        acc_ref[...] += _dot_tn(a_ref[...], c_ref[...])

        @pl.when(m == nm - 1)
        def _():
            o_ref[...] = acc_ref[...].astype(BF16)

    return pl.pallas_call(
        body, grid=(K // tk, N // tn, nm),
        in_specs=[pl.BlockSpec((tm, tk), lambda i, j, m: (m, i)), pl.BlockSpec((tm, tn), lambda i, j, m: (m, j))],
        out_specs=pl.BlockSpec((None, tk, tn), o_map),
        out_shape=jax.ShapeDtypeStruct((S, Kb, Nb), BF16),
        scratch_shapes=[pltpu.VMEM((tk, tn), F32)], compiler_params=_cp(3), name=name)(a, c)


def _mod_spec(D, ncr):
    return pl.BlockSpec((None, N_MOD, D), lambda i: (jnp.where(i < ncr, 0, 1), 0, 0))


def norm_mod(x, g, modv, i_sh, i_sc, ncr, name):
    T, D = x.shape

    def body(x_ref, g_ref, m_ref, h_ref):
        xv = x_ref[...]
        r = lax.rsqrt(jnp.mean(xv * xv, axis=-1, keepdims=True) + EPS)
        n = xv * r * g_ref[...]
        h_ref[...] = (n * (1.0 + m_ref[i_sc:i_sc + 1, :]) + m_ref[i_sh:i_sh + 1, :]).astype(BF16)

    return pl.pallas_call(
        body, grid=(T // R,),
        in_specs=[pl.BlockSpec((R, D), lambda i: (i, 0)), pl.BlockSpec((1, D), lambda i: (0, 0)), _mod_spec(D, ncr)],
        out_specs=pl.BlockSpec((R, D), lambda i: (i, 0)),
        out_shape=jax.ShapeDtypeStruct((T, D), BF16), compiler_params=_cp(1), name=name)(x, g, modv)


def norm_mod_bwd(x, g, modv, dh, dres, i_sc, ncr, name):
    T, D = x.shape

    def body(x_ref, g_ref, m_ref, dh_ref, dr_ref, dx_ref, acc_ref):
        i = pl.program_id(0)

        @pl.when(i == 0)
        def _():
            acc_ref[...] = jnp.zeros_like(acc_ref)

        xv = x_ref[...]
        r = lax.rsqrt(jnp.mean(xv * xv, axis=-1, keepdims=True) + EPS)
        xh = xv * r
        gv = g_ref[...]
        dhv = dh_ref[...]
        dn = dhv * (1.0 + m_ref[i_sc:i_sc + 1, :])
        s_sh = jnp.sum(dhv, axis=0, keepdims=True)
        s_sc = jnp.sum(dhv * (xh * gv), axis=0, keepdims=True)
        acc_ref[4:5, :] += jnp.sum(dn * xh, axis=0, keepdims=True)
        dxh = dn * gv
        dx_ref[...] = dr_ref[...] + r * (dxh - xh * jnp.mean(dxh * xh, axis=-1, keepdims=True))

        @pl.when(i < ncr)
        def _():
            acc_ref[0:1, :] += s_sh
            acc_ref[1:2, :] += s_sc

        @pl.when(i >= ncr)
        def _():
            acc_ref[2:3, :] += s_sh
            acc_ref[3:4, :] += s_sc

    row = pl.BlockSpec((R, D), lambda i: (i, 0))
    return pl.pallas_call(
        body, grid=(T // R,),
        in_specs=[row, pl.BlockSpec((1, D), lambda i: (0, 0)), _mod_spec(D, ncr), row, row],
        out_specs=[row, pl.BlockSpec((8, D), lambda i: (0, 0))],
        out_shape=[jax.ShapeDtypeStruct((T, D), F32), jax.ShapeDtypeStruct((8, D), F32)],
        compiler_params=_cp(1), name=name)(x, g, modv, dh, dres)


def gate_res_bwd(dx, y, modv, i_g, ncr, name):
    T, D = dx.shape

    def body(dx_ref, y_ref, m_ref, dy_ref, acc_ref):
        i = pl.program_id(0)

        @pl.when(i == 0)
        def _():
            acc_ref[...] = jnp.zeros_like(acc_ref)

        dxv = dx_ref[...]
        dy_ref[...] = (m_ref[i_g:i_g + 1, :] * dxv).astype(BF16)
        s = jnp.sum(dxv * y_ref[...], axis=0, keepdims=True)

        @pl.when(i < ncr)
        def _():
            acc_ref[0:1, :] += s

        @pl.when(i >= ncr)
        def _():
            acc_ref[1:2, :] += s

    row = pl.BlockSpec((R, D), lambda i: (i, 0))
    return pl.pallas_call(
        body, grid=(T // R,), in_specs=[row, row, _mod_spec(D, ncr)],
        out_specs=[row, pl.BlockSpec((8, D), lambda i: (0, 0))],
        out_shape=[jax.ShapeDtypeStruct((T, D), BF16), jax.ShapeDtypeStruct((8, D), F32)],
        compiler_params=_cp(1), name=name)(dx, y, modv)


def loss_head(x, target, g, ncr, name):
    T, D = x.shape

    def body(x_ref, t_ref, g_ref, dx_ref, acc_ref):
        i = pl.program_id(0)

        @pl.when(i == 0)
        def _():
            acc_ref[...] = jnp.zeros_like(acc_ref)

        @pl.when(i < ncr)
        def _():
            dx_ref[...] = jnp.zeros_like(dx_ref)

        @pl.when(i >= ncr)
        def _():
            xv = x_ref[...]
            r = lax.rsqrt(jnp.mean(xv * xv, axis=-1, keepdims=True) + EPS)
            xh = xv * r
            gv = g_ref[...]
            e = xh * gv - t_ref[...]
            acc_ref[1:2, :] += jnp.sum(e * e, axis=0, keepdims=True)
            dy = e * (1.0 / D)
            acc_ref[0:1, :] += jnp.sum(dy * xh, axis=0, keepdims=True)
            dxh = dy * gv
            dx_ref[...] = r * (dxh - xh * jnp.mean(dxh * xh, axis=-1, keepdims=True))

    row = pl.BlockSpec((R, D), lambda i: (i, 0))
    return pl.pallas_call(
        body, grid=(T // R,),
        in_specs=[row, pl.BlockSpec((R, D), lambda i: (jnp.maximum(i - ncr, 0), 0)), pl.BlockSpec((1, D), lambda i: (0, 0))],
        out_specs=[row, pl.BlockSpec((8, D), lambda i: (0, 0))],
        out_shape=[jax.ShapeDtypeStruct((T, D), F32), jax.ShapeDtypeStruct((8, D), F32)],
        compiler_params=_cp(1), name=name)(x, target, g)


def _conv31_specs(T):
    nh = R // HALO
    pv = lambda i: jnp.maximum(i * nh - 1, 0)
    nx = lambda i: jnp.minimum((i + 1) * nh, T // HALO - 1)
    return pv, nx


def _shifted_copies(E):
    n = E.shape[1]
    for s in range(1, 8):
        E[s, 0:n - 8, :] = E[0, pl.ds(s, n - 8), :]


def _fold8(v):
    acc = v[0:8]
    for r in range(8, v.shape[0], 8):
        acc = acc + v[r:r + 8]
    return acc


def _tap31(E, r0, c0, o):
    return E[o % 8, pl.ds(r0 + 8 * (o // 8), SUB), pl.ds(c0, LANES)]


def conv31_fwd(p, w32, b, RW, CW, ncr, name):
    T = p.shape[0]
    nT, cbk = T // R, _lane_tile(CW, 1024)
    n = R + 2 * HALO
    a0, g0 = 4 * RW // cbk, (4 * RW + CW) // cbk
    pv, nx = _conv31_specs(T)

    def body(a, g, ap, gp, an, gn, w, bb, uc, E):
        i = pl.program_id(0)
        has_prev = jnp.logical_and(i != 0, i != ncr)
        has_next = jnp.logical_and(i != ncr - 1, i != nT - 1)
        glu = lambda u, v: u.astype(F32) * _sig(v.astype(F32))
        E[0, 0:HALO, :] = jnp.where(has_prev, glu(ap[...], gp[...]), 0.0)
        E[0, HALO:HALO + R, :] = glu(a[...], g[...])
        E[0, HALO + R:, :] = jnp.where(has_next, glu(an[...], gn[...]), 0.0)
        _shifted_copies(E)
        for r0 in range(0, R, SUB):
            for c0 in range(0, cbk, LANES):
                cols = pl.ds(c0, LANES)
                acc = jnp.broadcast_to(bb[:, cols], (SUB, LANES))
                for k in range(CONV_K):
                    acc = acc + w[k:k + 1, cols] * _tap31(E, r0, c0, 1 + k)
                uc[r0:r0 + SUB, c0:c0 + LANES] = acc

    cur = lambda c0: pl.BlockSpec((R, cbk), lambda i, j: (i, c0 + j))
    hp = lambda c0: pl.BlockSpec((HALO, cbk), lambda i, j: (pv(i), c0 + j))
    hn = lambda c0: pl.BlockSpec((HALO, cbk), lambda i, j: (nx(i), c0 + j))
    return pl.pallas_call(
        body, grid=(nT, CW // cbk),
        in_specs=[cur(a0), cur(g0), hp(a0), hp(g0), hn(a0), hn(g0),
                  pl.BlockSpec((32, cbk), lambda i, j: (0, j)), pl.BlockSpec((1, cbk), lambda i, j: (0, j))],
        out_specs=pl.BlockSpec((R, cbk), lambda i, j: (i, j)),
        out_shape=jax.ShapeDtypeStruct((T, CW), F32),
        scratch_shapes=[pltpu.VMEM((8, n, cbk), F32)],
        compiler_params=_cp(2), name=name)(p, p, p, p, p, p, w32, b)


def conv31_bwd(duc, p, w32, RW, CW, ncr, name):
    T = p.shape[0]
    nT, cbk = T // R, _lane_tile(CW, 1024)
    n = R + 2 * HALO
    a0, g0 = 4 * RW // cbk, (4 * RW + CW) // cbk
    pv, nx = _conv31_specs(T)

    def body(d, dp_, dn_, a, g, ap, gp, an, gn, w, da, dg, accw, U, Dd):
        i = pl.program_id(1)

        @pl.when(i == 0)
        def _():
            accw[...] = jnp.zeros_like(accw)

        has_prev = jnp.logical_and(i != 0, i != ncr)
        has_next = jnp.logical_and(i != ncr - 1, i != nT - 1)
        glu = lambda u, v: u.astype(F32) * _sig(v.astype(F32))
        U[0, 0:HALO, :] = jnp.where(has_prev, glu(ap[...], gp[...]), 0.0)
        U[0, HALO:HALO + R, :] = glu(a[...], g[...])
        U[0, HALO + R:, :] = jnp.where(has_next, glu(an[...], gn[...]), 0.0)
        Dd[0, 0:HALO, :] = jnp.where(has_prev, dp_[...], 0.0)
        Dd[0, HALO:HALO + R, :] = d[...]
        Dd[0, HALO + R:, :] = jnp.where(has_next, dn_[...], 0.0)
        _shifted_copies(U)
        _shifted_copies(Dd)
        for r0 in range(0, R, SUB):
            for c0 in range(0, cbk, LANES):
                cols, blk = pl.ds(c0, LANES), (slice(r0, r0 + SUB), slice(c0, c0 + LANES))
                du = jnp.zeros((SUB, LANES), F32)
                for k in range(CONV_K):
                    du = du + w[k:k + 1, cols] * _tap31(Dd, r0, c0, HALO + 15 - k)
                av = a[blk].astype(F32)
                sg = _sig(g[blk].astype(F32))
                da[blk] = (du * sg).astype(BF16)
                dg[blk] = (du * av * sg * (1.0 - sg)).astype(BF16)
                dcur = d[blk]
                for k in range(CONV_K):
                    accw[8 * k:8 * k + 8, cols] += _fold8(dcur * _tap31(U, r0, c0, 1 + k))
                accw[248:256, cols] += _fold8(dcur)

    cur = lambda c0: pl.BlockSpec((R, cbk), lambda j, i: (i, c0 + j))
    hp = lambda c0: pl.BlockSpec((HALO, cbk), lambda j, i: (pv(i), c0 + j))
    hn = lambda c0: pl.BlockSpec((HALO, cbk), lambda j, i: (nx(i), c0 + j))
    out = pl.BlockSpec((R, cbk), lambda j, i: (i, j))
    ext = pltpu.VMEM((8, n, cbk), F32)
    da, dg, acc = pl.pallas_call(
        body, grid=(CW // cbk, nT),
        in_specs=[cur(0), hp(0), hn(0), cur(a0), cur(g0), hp(a0), hp(g0), hn(a0), hn(g0),
                  pl.BlockSpec((32, cbk), lambda j, i: (0, j))],
        out_specs=[out, out, pl.BlockSpec((256, cbk), lambda j, i: (0, j))],
        out_shape=[jax.ShapeDtypeStruct((T, CW), BF16), jax.ShapeDtypeStruct((T, CW), BF16),
                   jax.ShapeDtypeStruct((256, CW), F32)],
        scratch_shapes=[ext, ext],
        compiler_params=_cp(2), name=name)(duc, duc, duc, p, p, p, p, p, p, w32)
    return da, dg, jnp.sum(acc.reshape(32, 8, CW), axis=1)


def _rope(v, cosv, sinv, first):
    swapped = jnp.where(first, pltpu.roll(v, 96, 1), pltpu.roll(v, 32, 1))
    return v * cosv + swapped * sinv


def _unrope(v, cosv, sinv, first):
    z = v * sinv
    return v * cosv + jnp.where(first, pltpu.roll(z, 96, 1), pltpu.roll(z, 32, 1))


def _decay_tables(lg_ref, H, DM, QD, KD, CD):
    n = lax.broadcasted_iota(jnp.int32, (RC, DH), 0).astype(F32)
    m = lax.broadcasted_iota(jnp.int32, (RC, DH), 1).astype(F32)
    for d in range(2):
        for h in range(H):
            i = d * H + h
            lg = lg_ref[i:i + 1, :]
            diff = (n - m) if d == 0 else (m - n)
            DM[i] = jnp.where(diff >= 0, jnp.exp(lg * jnp.maximum(diff, 0.0)), 0.0)
            QD[i] = jnp.exp(lg * ((n + 1.0) if d == 0 else (RC - n)))
            KD[i] = jnp.exp(lg * ((RC - 1.0 - n) if d == 0 else n))
            CD[i] = jnp.exp(lg * float(RC)) + jnp.zeros((RC, DH), F32)


def _chunk_orders(NC, ncc):
    cf = lambda s: s
    cb = lambda s: jnp.where(s < ncc, ncc - 1 - s, NC - 1 - (s - ncc))
    return cf, cb


def ret_fwd(p, cosT, sinT, lgt, H, ncc, name):
    T = p.shape[0]
    RW, NC = H * DH, T // RC
    cf, cb = _chunk_orders(NC, ncc)
    scale = DH ** -0.5

    def body(qf, kf, vf, qb, kb, vb, cosf, sinf, cosb, sinb, lg_ref, of_ref, ob_ref, sf_ref, sb_ref, S, DM, QD, KD, CD):
        s = pl.program_id(0)

        @pl.when(s == 0)
        def _():
            S[...] = jnp.zeros_like(S)
            _decay_tables(lg_ref, H, DM, QD, KD, CD)

        first = (lax.broadcasted_iota(jnp.int32, (RC, DH), 1) % 64) < 32
        for d, (q_ref, k_ref, v_ref, c_ref, s_ref, o_ref, st_ref) in enumerate(
                ((qf, kf, vf, cosf, sinf, of_ref, sf_ref), (qb, kb, vb, cosb, sinb, ob_ref, sb_ref))):
            cosv, sinv = c_ref[...], s_ref[...]
            for h in range(H):
                hs, i = slice(h * DH, (h + 1) * DH), d * H + h
                q16 = _rope(q_ref[:, hs].astype(F32), cosv, sinv, first).astype(BF16)
                k = _rope(k_ref[:, hs].astype(F32), cosv, sinv, first) * scale
                k16 = k.astype(BF16)
                v = v_ref[:, hs]
                s_in = S[i]
                s16 = s_in.astype(BF16)
                st_ref[h] = s16
                sc = _dot_nt(q16, k16) * DM[i]
                o_ref[:, hs] = _dot(sc.astype(BF16), v) + _dot(q16, s16) * QD[i]
                S[i] = s_in * CD[i] + _dot_tn((k * KD[i]).astype(BF16), v)

    pspec = lambda col, cm: pl.BlockSpec((RC, RW), lambda s: (cm(s), col))
    tspec = lambda cm: pl.BlockSpec((RC, DH), lambda s: (cm(s), 0))
    ospec = lambda cm: pl.BlockSpec((RC, RW), lambda s: (cm(s), 0))
    stspec = pl.BlockSpec((None, H, DH, DH), lambda s: (s, 0, 0, 0))
    tab = pltpu.VMEM((2 * H, RC, DH), F32)
    return pl.pallas_call(
        body, grid=(NC,),
        in_specs=[pspec(0, cf), pspec(1, cf), pspec(2, cf), pspec(0, cb), pspec(1, cb), pspec(2, cb),
                  tspec(cf), tspec(cf), tspec(cb), tspec(cb), pl.BlockSpec((2 * H, DH), lambda s: (0, 0))],
        out_specs=[ospec(cf), ospec(cb), stspec, stspec],
        out_shape=[jax.ShapeDtypeStruct((T, RW), F32), jax.ShapeDtypeStruct((T, RW), F32),
                   jax.ShapeDtypeStruct((NC, H, DH, DH), BF16), jax.ShapeDtypeStruct((NC, H, DH, DH), BF16)],
        scratch_shapes=[tab, tab, tab, tab, tab], compiler_params=_cp(1), name=name)(
            p, p, p, p, p, p, cosT, sinT, cosT, sinT, lgt)


def ret_bwd(p, do, cosT, sinT, lgt, stf, stb, H, ncc, name):
    T = p.shape[0]
    RW, NC = H * DH, T // RC
    cf0, cb0 = _chunk_orders(NC, ncc)
    cf = lambda sp: cf0(NC - 1 - sp)
    cb = lambda sp: cb0(NC - 1 - sp)
    scale = DH ** -0.5

    def body(qf, kf, vf, qb, kb, vb, dof, dob, cosf, sinf, cosb, sinb, lg_ref, stf_ref, stb_ref,
             dqf, dkf, dvf, dqb, dkb, dvb, glg, dS, DM, QD, KD, CD):
        sp = pl.program_id(0)

        @pl.when(sp == 0)
        def _():
            dS[...] = jnp.zeros_like(dS)
            glg[...] = jnp.zeros_like(glg)
            _decay_tables(lg_ref, H, DM, QD, KD, CD)

        first = (lax.broadcasted_iota(jnp.int32, (RC, DH), 1) % 64) < 32
        n = lax.broadcasted_iota(jnp.int32, (RC, DH), 0).astype(F32)
        m = lax.broadcasted_iota(jnp.int32, (RC, DH), 1).astype(F32)
        for d, (q_ref, k_ref, v_ref, do_ref, c_ref, s_ref, st_ref, dq_ref, dk_ref, dv_ref) in enumerate(
                ((qf, kf, vf, dof, cosf, sinf, stf_ref, dqf, dkf, dvf),
                 (qb, kb, vb, dob, cosb, sinb, stb_ref, dqb, dkb, dvb))):
            cosv, sinv = c_ref[...], s_ref[...]
            diff = (n - m) if d == 0 else (m - n)
            posq = (n + 1.0) if d == 0 else (RC - n)
            posk = (RC - 1.0 - n) if d == 0 else n
            for h in range(H):
                hs, i = slice(h * DH, (h + 1) * DH), d * H + h
                q = _rope(q_ref[:, hs].astype(F32), cosv, sinv, first)
                k = _rope(k_ref[:, hs].astype(F32), cosv, sinv, first) * scale
                q16, k16 = q.astype(BF16), k.astype(BF16)
                v = v_ref[:, hs]
                s_in = st_ref[h]
                ds_out = dS[i]
                ds16 = ds_out.astype(BF16)
                do16 = do_ref[:, hs]
                doq = (do16.astype(F32) * QD[i]).astype(BF16)
                a = _dot_nt(q16, k16) * DM[i]
                da_raw = _dot_nt(do16, v)
                da16 = (da_raw * DM[i]).astype(BF16)
                dq_state = _dot_nt(doq, s_in)
                dk_state = _dot_nt(v, ds16) * KD[i]
                dqr = _dot(da16, k16) + dq_state
                dkr = _dot_tn(da16, q16) + dk_state
                dv_ref[:, hs] = _dot_tn(a.astype(BF16), do16) + _dot((k * KD[i]).astype(BF16), ds16)
                dS[i] = ds_out * CD[i] + _dot_tn(q16, doq)
                glg[i] += (da_raw * a * diff + posq * q * dq_state + posk * k * dk_state
                           + float(RC) * CD[i] * ds_out * s_in.astype(F32))
                dq_ref[:, hs] = _unrope(dqr, cosv, sinv, first)
                dk_ref[:, hs] = _unrope(dkr, cosv, sinv, first) * scale

    pspec = lambda col, cm: pl.BlockSpec((RC, RW), lambda s: (cm(s), col))
    tspec = lambda cm: pl.BlockSpec((RC, DH), lambda s: (cm(s), 0))
    ospec = lambda cm: pl.BlockSpec((RC, RW), lambda s: (cm(s), 0))
    stspec = pl.BlockSpec((None, H, DH, DH), lambda s: (NC - 1 - s, 0, 0, 0))
    tab = pltpu.VMEM((2 * H, RC, DH), F32)
    big = jax.ShapeDtypeStruct((T, RW), F32)
    return pl.pallas_call(
        body, grid=(NC,),
        in_specs=[pspec(0, cf), pspec(1, cf), pspec(2, cf), pspec(0, cb), pspec(1, cb), pspec(2, cb),
                  ospec(cf), ospec(cb), tspec(cf), tspec(cf), tspec(cb), tspec(cb),
                  pl.BlockSpec((2 * H, DH), lambda s: (0, 0)), stspec, stspec],
        out_specs=[ospec(cf), ospec(cf), ospec(cf), ospec(cb), ospec(cb), ospec(cb),
                   pl.BlockSpec((2 * H, RC, DH), lambda s: (0, 0, 0))],
        out_shape=[big, big, big, big, big, big, jax.ShapeDtypeStruct((2 * H, RC, DH), F32)],
        scratch_shapes=[tab, tab, tab, tab, tab], compiler_params=_cp(1), name=name)(
            p, p, p, p, p, p, do, do, cosT, sinT, cosT, sinT, lgt, stf, stb)


def mix_fwd(o_f, o_b, p, uc, lng, lnb, H, name):
    T, RW = o_f.shape
    CW = uc.shape[1]

    def body(of_ref, ob_ref, g_ref, uc_ref, lg_ref, lb_ref, out_ref):
        for h in range(H):
            hs = slice(h * DH, (h + 1) * DH)
            o = of_ref[:, hs] + ob_ref[:, hs]
            on = o * lax.rsqrt(jnp.mean(o * o, axis=-1, keepdims=True) + EPS)
            gv = g_ref[:, hs].astype(F32)
            out_ref[:, hs] = (gv * _sig(gv) * on).astype(BF16)
        u = uc_ref[...]
        mu = jnp.mean(u, axis=-1, keepdims=True)
        var = jnp.mean(jnp.square(u - mu), axis=-1, keepdims=True)
        z = (u - mu) * lax.rsqrt(var + EPS) * lg_ref[...] + lb_ref[...]
        out_ref[:, RW:] = (z * _sig(z)).astype(BF16)

    rw = pl.BlockSpec((R, RW), lambda i: (i, 0))
    vec = pl.BlockSpec((1, CW), lambda i: (0, 0))
    return pl.pallas_call(
        body, grid=(T // R,),
        in_specs=[rw, rw, pl.BlockSpec((R, RW), lambda i: (i, 3)), pl.BlockSpec((R, CW), lambda i: (i, 0)), vec, vec],
        out_specs=pl.BlockSpec((R, RW + CW), lambda i: (i, 0)),
        out_shape=jax.ShapeDtypeStruct((T, RW + CW), BF16), compiler_params=_cp(1), name=name)(o_f, o_b, p, uc, lng, lnb)


def mix_bwd(dmix, o_f, o_b, p, uc, lng, lnb, H, name):
    T, RW = o_f.shape
    CW = uc.shape[1]

    def body(dm_ref, of_ref, ob_ref, g_ref, uc_ref, lg_ref, lb_ref, do_ref, dg_ref, duc_ref, acc_ref):
        i = pl.program_id(0)

        @pl.when(i == 0)
        def _():
            acc_ref[...] = jnp.zeros_like(acc_ref)

        for h in range(H):
            hs = slice(h * DH, (h + 1) * DH)
            o = of_ref[:, hs] + ob_ref[:, hs]
            r = lax.rsqrt(jnp.mean(o * o, axis=-1, keepdims=True) + EPS)
            on = o * r
            gv = g_ref[:, hs].astype(F32)
            sg = _sig(gv)
            dmr = dm_ref[:, hs].astype(F32)
            dg_ref[:, hs] = (dmr * on * (sg * (1.0 + gv * (1.0 - sg)))).astype(BF16)
            don = dmr * (gv * sg)
            do_ref[:, hs] = (r * (don - on * jnp.mean(don * on, axis=-1, keepdims=True))).astype(BF16)
        u = uc_ref[...]
        mu = jnp.mean(u, axis=-1, keepdims=True)
        rs = lax.rsqrt(jnp.mean(jnp.square(u - mu), axis=-1, keepdims=True) + EPS)
        zh = (u - mu) * rs
        lg = lg_ref[...]
        z = zh * lg + lb_ref[...]
        sz = _sig(z)
        dz = dm_ref[:, RW:].astype(F32) * (sz * (1.0 + z * (1.0 - sz)))
        acc_ref[0:1, :] += jnp.sum(dz * zh, axis=0, keepdims=True)
        acc_ref[1:2, :] += jnp.sum(dz, axis=0, keepdims=True)
        dzh = dz * lg
        duc_ref[...] = rs * (dzh - jnp.mean(dzh, axis=-1, keepdims=True)
                             - zh * jnp.mean(dzh * zh, axis=-1, keepdims=True))

    rw = pl.BlockSpec((R, RW), lambda i: (i, 0))
    cw = pl.BlockSpec((R, CW), lambda i: (i, 0))
    vec = pl.BlockSpec((1, CW), lambda i: (0, 0))
    return pl.pallas_call(
        body, grid=(T // R,),
        in_specs=[pl.BlockSpec((R, RW + CW), lambda i: (i, 0)), rw, rw, pl.BlockSpec((R, RW), lambda i: (i, 3)), cw, vec, vec],
        out_specs=[rw, rw, cw, pl.BlockSpec((8, CW), lambda i: (0, 0))],
        out_shape=[jax.ShapeDtypeStruct((T, RW), BF16), jax.ShapeDtypeStruct((T, RW), BF16),
                   jax.ShapeDtypeStruct((T, CW), F32), jax.ShapeDtypeStruct((8, CW), F32)],
        compiler_params=_cp(1), name=name)(dmix, o_f, o_b, p, uc, lng, lnb)


def assemble_dp(dqf, dqb, dkf, dkb, dvf, dvb, dg, da, dbg, name):
    T, RW = dqf.shape
    CW = da.shape[1]

    def body(qf, qb, kf, kb, vf, vb, g, a, b, out):
        out[:, 0:RW] = (qf[...] + qb[...]).astype(BF16)
        out[:, RW:2 * RW] = (kf[...] + kb[...]).astype(BF16)
        out[:, 2 * RW:3 * RW] = (vf[...] + vb[...]).astype(BF16)
        out[:, 3 * RW:4 * RW] = g[...]
        out[:, 4 * RW:4 * RW + CW] = a[...]
        out[:, 4 * RW + CW:] = b[...]

    rw = pl.BlockSpec((R, RW), lambda i: (i, 0))
    cw = pl.BlockSpec((R, CW), lambda i: (i, 0))
    W = 4 * RW + 2 * CW
    return pl.pallas_call(
        body, grid=(T // R,), in_specs=[rw] * 7 + [cw, cw], out_specs=pl.BlockSpec((R, W), lambda i: (i, 0)),
        out_shape=jax.ShapeDtypeStruct((T, W), BF16), compiler_params=_cp(1), name=name)(
            dqf, dqb, dkf, dkb, dvf, dvb, dg, da, dbg)


FPAD = 8


def _fill_plain(plain, cur, prv, nxt, has_prev, has_next):
    n, cb = plain.shape[0] - 2 * FPAD, plain.shape[1]
    pv, nv = prv[...], nxt[...]
    plain[0:FPAD, :] = jnp.zeros((FPAD, cb), F32)
    plain[FPAD + n:, :] = jnp.zeros((FPAD, cb), F32)
    plain[FPAD:FPAD + GRID_W, :] = jnp.where(has_prev, pv, jnp.zeros_like(pv)).astype(F32)
    plain[FPAD + GRID_W:FPAD + GRID_W + R, :] = cur[...].astype(F32)
    plain[FPAD + GRID_W + R:FPAD + n, :] = jnp.where(has_next, nv, jnp.zeros_like(nv)).astype(F32)


def _fill_ext(bufs, cur, prv, nxt, has_prev, has_next, is_ctx):
    left, plain, right = bufs
    n, cb = left.shape
    _fill_plain(plain, cur, prv, nxt, has_prev, has_next)
    left[...] = plain[pl.ds(FPAD - 1, n), :]
    right[...] = plain[pl.ds(FPAD + 1, n), :]
    if not is_ctx:
        for r in range(0, n, GRID_W):
            left[r:r + 1, :] = jnp.zeros((1, cb), F32)
            right[r + GRID_W - 1:r + GRID_W, :] = jnp.zeros((1, cb), F32)


def _taps(is_ctx):
    return [(dr, dc) for dr in ((0,) if is_ctx else (-1, 0, 1)) for dc in (-1, 0, 1)]


def _tap_src(bufs, r0, c0, dr, dc):
    off = (FPAD if dc == 0 else 0) + GRID_W + r0 + GRID_W * dr
    return bufs[dc + 1][pl.ds(off, SUB), pl.ds(c0, FLANES)]


def _conv9(bufs, w, r0, c0, is_ctx, flip):
    acc = jnp.zeros((SUB, FLANES), F32)
    for dr, dc in _taps(is_ctx):
        widx = (dr + 1) * 3 + dc + 1
        src = _tap_src(bufs, r0, c0, -dr, -dc) if flip else _tap_src(bufs, r0, c0, dr, dc)
        acc = acc + w[widx:widx + 1, pl.ds(c0, FLANES)] * src
    return acc


def _ffn_specs(T, cb, order):
    nq = R // GRID_W
    pv = lambda i: jnp.maximum(i * nq - 1, 0)
    nx = lambda i: jnp.minimum((i + 1) * nq, T // GRID_W - 1)
    if order == 'ij':
        mk = lambda blk, rf, c0: pl.BlockSpec(blk, lambda i, j: (rf(i), c0 + j))
    else:
        mk = lambda blk, rf, c0: pl.BlockSpec(blk, lambda j, i: (rf(i), c0 + j))
    cur = lambda c0: mk((R, cb), lambda i: i, c0)
    hp = lambda c0: mk((GRID_W, cb), pv, c0)
    hn = lambda c0: mk((GRID_W, cb), nx, c0)
    vec = lambda rows: mk((rows, cb), lambda i: 0, 0)
    return cur, hp, hn, vec


def _ffn_flags(i, ncr, nT):
    return i > ncr, jnp.logical_and(i >= ncr, i != nT - 1)


def _ffn_scratch(cb, sets):
    n = R + 2 * GRID_W
    return [pltpu.VMEM((n, cb), F32), pltpu.VMEM((n + 2 * FPAD, cb), F32), pltpu.VMEM((n, cb), F32)] * sets


def ffn_act(up, w16, b, CF, ncr, name):
    T = up.shape[0]
    nT, cb = T // R, _lane_tile(CF, FFN_COLS)
    ncb = CF // cb
    cur, hp, hn, vec = _ffn_specs(T, cb, 'ij')

    def body(g, v, gp, gn, w, bb, out, gc_out, e0, e1, e2):
        i = pl.program_id(0)
        has_prev, has_next = _ffn_flags(i, ncr, nT)
        bufs = (e0, e1, e2)

        def run(is_ctx):
            _fill_ext(bufs, g, gp, gn, has_prev, has_next, is_ctx)
            for r0 in range(0, R, SUB):
                for c0 in range(0, cb, FLANES):
                    gc = _conv9(bufs, w, r0, c0, is_ctx, False) + bb[:, pl.ds(c0, FLANES)]
                    val = v[r0:r0 + SUB, c0:c0 + FLANES].astype(F32)
                    out[r0:r0 + SUB, c0:c0 + FLANES] = (gc * _sig(gc) * val).astype(BF16)
                    gc_out[r0:r0 + SUB, c0:c0 + FLANES] = gc.astype(BF16)

        pl.when(i < ncr)(lambda: run(True))
        pl.when(i >= ncr)(lambda: run(False))

    sds = jax.ShapeDtypeStruct((T, CF), BF16)
    return pl.pallas_call(
        body, grid=(nT, ncb), in_specs=[cur(0), cur(ncb), hp(0), hn(0), vec(16), vec(1)], out_specs=[cur(0), cur(0)],
        out_shape=[sds, sds],
        scratch_shapes=_ffn_scratch(cb, 1), compiler_params=_cp(2), name=name)(up, up, up, up, w16, b)


def ffn_act_bwd1(up, gcs, dact, CF, name):
    T = up.shape[0]
    cb = _lane_tile(CF, FFN_COLS)
    ncb = CF // cb

    def body(v, gc_ref, da, dgc, dup, accb):
        @pl.when(pl.program_id(1) == 0)
        def _():
            accb[...] = jnp.zeros_like(accb)

        for r0 in range(0, R, SUB):
            for c0 in range(0, cb, FLANES):
                blk = (slice(r0, r0 + SUB), slice(c0, c0 + FLANES))
                gc = gc_ref[blk].astype(F32)
                sg = _sig(gc)
                dav = da[blk].astype(F32)
                dup[blk] = (dav * gc * sg).astype(BF16)
                d = dav * v[blk].astype(F32) * (sg * (1.0 + gc * (1.0 - sg)))
                dgc[blk] = d.astype(BF16)
                accb[:, c0:c0 + FLANES] += _fold8(d)

    blk = lambda c0: pl.BlockSpec((R, cb), lambda j, i: (i, c0 + j))
    return pl.pallas_call(
        body, grid=(ncb, T // R), in_specs=[blk(ncb), blk(0), blk(0)],
        out_specs=[blk(0), blk(ncb), pl.BlockSpec((8, cb), lambda j, i: (0, j))],
        out_shape=[jax.ShapeDtypeStruct((T, CF), BF16), jax.ShapeDtypeStruct((T, 2 * CF), BF16),
                   jax.ShapeDtypeStruct((8, CF), F32)],
        compiler_params=_cp(2), name=name)(up, gcs, dact)


def ffn_act_bwd2(dgc, up, dup, w16, CF, ncr, name):
    T = up.shape[0]
    nT, cb = T // R, _lane_tile(CF, FFN_COLS)
    ncb = CF // cb
    cur, hp, hn, vec = _ffn_specs(T, cb, 'ji')

    def body(d, dp_, dn_, g, gp, gn, w, dup_in, dgate, accw, d0, d1, d2, gplain):
        i = pl.program_id(1)

        @pl.when(i == 0)
        def _():
            accw[...] = jnp.zeros_like(accw)

        has_prev, has_next = _ffn_flags(i, ncr, nT)
        dbufs = (d0, d1, d2)
        _fill_plain(gplain, g, gp, gn, has_prev, has_next)

        def run(is_ctx):
            _fill_ext(dbufs, d, dp_, dn_, has_prev, has_next, is_ctx)
            for r0 in range(0, R, SUB):
                for c0 in range(0, cb, FLANES):
                    dgate[r0:r0 + SUB, c0:c0 + FLANES] = _conv9(dbufs, w, r0, c0, is_ctx, True).astype(BF16)
                    for dr, dc in _taps(is_ctx):
                        widx = (dr + 1) * 3 + dc + 1
                        dmov = _tap_src(dbufs, r0, c0, 0, -dc)
                        gsrc = gplain[pl.ds(FPAD + GRID_W + r0 + GRID_W * dr, SUB), pl.ds(c0, FLANES)]
                        accw[8 * widx:8 * widx + 8, pl.ds(c0, FLANES)] += _fold8(dmov * gsrc)

        pl.when(i < ncr)(lambda: run(True))
        pl.when(i >= ncr)(lambda: run(False))

    dup, acc = pl.pallas_call(
        body, grid=(ncb, nT),
        in_specs=[cur(0), hp(0), hn(0), cur(0), hp(0), hn(0), vec(16), pl.BlockSpec(memory_space=pl.ANY)],
        out_specs=[cur(0), vec(128)],
        out_shape=[jax.ShapeDtypeStruct((T, 2 * CF), BF16), jax.ShapeDtypeStruct((128, CF), F32)],
        input_output_aliases={7: 0},
        scratch_shapes=_ffn_scratch(cb, 1) + [pltpu.VMEM((R + 2 * GRID_W + 2 * FPAD, cb), F32)],
        compiler_params=_cp(2), name=name)(
            dgc, dgc, dgc, up, up, up, w16, dup)
    return dup, jnp.sum(acc.reshape(16, 8, CF), axis=1)


def mod_fwd(cs, w_mod, name):
    L, D, Ns = w_mod.shape
    tn = _lane_tile(Ns, 768)

    def body(c_ref, w_ref, o_ref):
        cv = c_ref[...]
        o_ref[...] = _dot((cv * _sig(cv)).astype(BF16), w_ref[...].astype(BF16))

    return pl.pallas_call(
        body, grid=(L, Ns // tn),
        in_specs=[pl.BlockSpec((16, D), lambda l, j: (0, 0)), pl.BlockSpec((None, D, tn), lambda l, j: (l, 0, j))],
        out_specs=pl.BlockSpec((None, 16, tn), lambda l, j: (l, 0, j)),
        out_shape=jax.ShapeDtypeStruct((L, 16, Ns), F32), compiler_params=_cp(2), name=name)(cs, w_mod)


def mod_bwd(cs, w_mod, dmod, name):
    L, D, Ns = w_mod.shape
    tn = _lane_tile(Ns, 768)

    def body(c_ref, w_ref, dm_ref, gw_ref, ds_ref):
        @pl.when(jnp.logical_and(pl.program_id(0) == 0, pl.program_id(1) == 0))
        def _():
            ds_ref[...] = jnp.zeros_like(ds_ref)

        cv = c_ref[...]
        dm = dm_ref[...].astype(BF16)
        gw_ref[...] = _dot_tn((cv * _sig(cv)).astype(BF16), dm)
        ds_ref[...] += _dot_nt(dm, w_ref[...].astype(BF16))

    return pl.pallas_call(
        body, grid=(L, Ns // tn),
        in_specs=[pl.BlockSpec((16, D), lambda l, j: (0, 0)), pl.BlockSpec((None, D, tn), lambda l, j: (l, 0, j)),
                  pl.BlockSpec((None, 16, tn), lambda l, j: (l, 0, j))],
        out_specs=[pl.BlockSpec((None, D, tn), lambda l, j: (l, 0, j)), pl.BlockSpec((16, D), lambda l, j: (0, 0))],
        out_shape=[jax.ShapeDtypeStruct((L, D, Ns), F32), jax.ShapeDtypeStruct((16, D), F32)],
        compiler_params=_cp(2), name=name)(cs, w_mod, dmod)


def cast_bf16(w, name):
    L, Kb, Nb = w.shape
    w2 = w.reshape(L * Kb, Nb)
    tr = _row_tile(L * Kb, Nb, 1 << 19)

    def body(w_ref, o_ref):
        o_ref[...] = w_ref[...].astype(BF16)

    spec = pl.BlockSpec((tr, Nb), lambda i: (i, 0))
    out = pl.pallas_call(body, grid=(L * Kb // tr,), in_specs=[spec], out_specs=spec,
                         out_shape=jax.ShapeDtypeStruct((L * Kb, Nb), BF16), compiler_params=_cp(1), name=name)(w2)
    return out.reshape(L, Kb, Nb)


def add_half(dw, recv, c_idx, name):
    S, Kb, Nb = dw.shape
    Kh = Kb // 2
    tr = _row_tile(Kh, Nb, 1 << 19)
    nb = Kh // tr

    def body(c_ref, a_ref, b_ref, o_ref):
        o_ref[...] = (a_ref[...].astype(F32) + b_ref[...].astype(F32)).astype(BF16)

    return pl.pallas_call(
        body,
        grid_spec=pltpu.PrefetchScalarGridSpec(
            num_scalar_prefetch=1, grid=(S, nb),
            in_specs=[pl.BlockSpec((None, tr, Nb), lambda s, i, c: (s, c[0] * nb + i, 0)),
                      pl.BlockSpec((None, tr, Nb), lambda s, i, c: (s, i, 0))],
            out_specs=pl.BlockSpec((None, tr, Nb), lambda s, i, c: (s, i, 0))),
        out_shape=jax.ShapeDtypeStruct((S, Kh, Nb), BF16), compiler_params=_cp(2), name=name)(c_idx, dw, recv)


def add_shards(mine, sib, c_idx, name):
    S, Kh, Nb = mine.shape
    tr = _row_tile(Kh, Nb, 1 << 18)
    nb = Kh // tr

    def body(c_ref, m_ref, s_ref, o_ref):
        def total(p_ref):
            acc = p_ref[0].astype(F32)
            for s in range(1, S):
                acc = acc + p_ref[s].astype(F32)
            o_ref[...] = acc

        pl.when(pl.program_id(0) == 0)(lambda: total(m_ref))
        pl.when(pl.program_id(0) == 1)(lambda: total(s_ref))

    out = pl.pallas_call(
        body,
        grid_spec=pltpu.PrefetchScalarGridSpec(
            num_scalar_prefetch=1, grid=(2, nb),
            in_specs=[pl.BlockSpec((S, tr, Nb), lambda h, i, c: (0, jnp.where(h == 0, i, nb - 1), 0)),
                      pl.BlockSpec((S, tr, Nb), lambda h, i, c: (0, jnp.where(h == 0, 0, i), 0))],
            out_specs=pl.BlockSpec((None, tr, Nb), lambda h, i, c: (jnp.where(h == 0, c[0], 1 - c[0]), i, 0))),
        out_shape=jax.ShapeDtypeStruct((2, Kh, Nb), F32), compiler_params=_cp(2), name=name)(c_idx, mine, sib)
    return out.reshape(2 * Kh, Nb)


def adamw(w, g, m, v, name):
    shape = w.shape
    cols = shape[-1]
    rows = w.size // cols
    w2, g2, m2, v2 = (t.reshape(rows, cols) for t in (w, g, m, v))
    tr = _row_tile(rows, cols, 3 << 17)

    def body(w_ref, g_ref, m_ref, v_ref, d_ref, nm_ref, nv_ref):
        gv = g_ref[...]
        nm = ADAM_B1 * m_ref[...] + (1.0 - ADAM_B1) * gv
        nv = ADAM_B2 * v_ref[...] + (1.0 - ADAM_B2) * jnp.square(gv)
        m_hat = nm / (1.0 - ADAM_B1 ** ADAM_STEP)
        v_hat = nv / (1.0 - ADAM_B2 ** ADAM_STEP)
        d_ref[...] = -ADAM_LR * (m_hat / (jnp.sqrt(v_hat) + ADAM_EPS) + ADAM_WD * w_ref[...])
        nm_ref[...] = nm
        nv_ref[...] = nv

    spec = pl.BlockSpec((tr, cols), lambda i: (i, 0))
    sds = jax.ShapeDtypeStruct((rows, cols), F32)
    d, nm, nv = pl.pallas_call(body, grid=(rows // tr,), in_specs=[spec] * 4, out_specs=[spec] * 3,
                               out_shape=[sds, sds, sds], compiler_params=_cp(1), name=name)(w2, g2, m2, v2)
    return d.reshape(shape), nm.reshape(shape), nv.reshape(shape)


def gather_sum(v, name):
    r, cols = v.shape

    def body(v_ref, g_ref, s_ref, send_sems, recv_sems, local_sem):
        x, y, c = _place()
        me = 4 * x + 2 * y + c
        mine = pltpu.make_async_copy(v_ref, g_ref.at[me], local_sem)
        mine.start()
        sends, peers = [], []
        for k in range(1, 8):
            px = 1 - x if k & 4 else x
            py = 1 - y if k & 2 else y
            pc = 1 - c if k & 1 else c
            cp = pltpu.make_async_remote_copy(src_ref=v_ref, dst_ref=g_ref.at[me], send_sem=send_sems.at[k - 1],
                                              recv_sem=recv_sems.at[k - 1], device_id=(px, py, pc), device_id_type=MESH)
            cp.start()
            sends.append(cp)
            peers.append((px, py, pc))
        for k, (px, py, pc) in enumerate(peers):
            pltpu.make_async_remote_copy(src_ref=v_ref, dst_ref=g_ref.at[4 * px + 2 * py + pc], send_sem=send_sems.at[k],
                                         recv_sem=recv_sems.at[k], device_id=(px, py, pc), device_id_type=MESH).wait_recv()
        for cp in sends:
            cp.wait_send()
        mine.wait()
        acc = g_ref[0]
        for d in range(1, 8):
            acc = acc + g_ref[d]
        s_ref[...] = acc

    vm = pl.BlockSpec(memory_space=pltpu.VMEM)
    return pl.pallas_call(
        body, in_specs=[vm], out_specs=[vm, vm],
        out_shape=[jax.ShapeDtypeStruct((8, r, cols), F32), jax.ShapeDtypeStruct((r, cols), F32)],
        scratch_shapes=[pltpu.SemaphoreType.DMA((7,)), pltpu.SemaphoreType.DMA((7,)), pltpu.SemaphoreType.DMA],
        compiler_params=_cp0(), name=name)(v)


def gather_weights(wb, name):
    L, Kb, Nb = wb.shape
    Kh = Kb // 2

    def body(w_ref, *rest):
        outs, (send_sems, recv_sems, local_sems) = rest[:L], rest[L:]
        x, y, c = _place()
        jm = 2 * x + y
        sib = (x, y, 1 - c)
        chips = _other_chips(x, y)

        def cp(l, t, src, dst, to):
            return pltpu.make_async_remote_copy(src_ref=src, dst_ref=dst, send_sem=send_sems.at[7 * l + t],
                                                recv_sem=recv_sems.at[7 * l + t], device_id=to, device_id_type=MESH)

        started, local = [], []
        for l in range(L):
            src = w_ref.at[l, pl.ds(c * Kh, Kh), :]
            dst = outs[l].at[jm, c]
            lc = pltpu.make_async_copy(src, dst, local_sems.at[l])
            lc.start()
            local.append(lc)
            for t, (px, py) in enumerate(chips):
                started.append(cp(l, t, src, dst, (px, py, c)))
            started.append(cp(l, 3, src, dst, sib))
            for s in started[-4:]:
                s.start()
        for l in range(L):
            for t, (px, py) in enumerate(chips):
                blk = outs[l].at[2 * px + py, c]
                cp(l, t, blk, blk, (px, py, c)).wait_recv()
                fwd = cp(l, 4 + t, blk, blk, sib)
                fwd.start()
                started.append(fwd)
        for l in range(L):
            blk = outs[l].at[jm, 1 - c]
            cp(l, 3, blk, blk, sib).wait_recv()
            for t, (px, py) in enumerate(chips):
                blk = outs[l].at[2 * px + py, 1 - c]
                cp(l, 4 + t, blk, blk, sib).wait_recv()
        for s in started:
            s.wait_send()
        for lc in local:
            lc.wait()

    hbm = pl.BlockSpec(memory_space=pl.ANY)
    outs = pl.pallas_call(
        body, in_specs=[hbm], out_specs=[hbm] * L,
        out_shape=[jax.ShapeDtypeStruct((4, 2, Kh, Nb), BF16)] * L,
        scratch_shapes=[pltpu.SemaphoreType.DMA((7 * L,)), pltpu.SemaphoreType.DMA((7 * L,)), pltpu.SemaphoreType.DMA((L,))],
        compiler_params=_cp0(), name=name)(wb)
    return [o.reshape(4, Kb, Nb) for o in outs]


def swap_halves(dw, name):
    S, Kb, Nb = dw.shape
    Kh = Kb // 2

    def body(d_ref, o_ref, send_sems, recv_sems):
        x, y, c = _place()
        sib = (x, y, 1 - c)
        cps = [pltpu.make_async_remote_copy(src_ref=d_ref.at[s, pl.ds((1 - c) * Kh, Kh), :], dst_ref=o_ref.at[s],
                                            send_sem=send_sems.at[s], recv_sem=recv_sems.at[s], device_id=sib,
                                            device_id_type=MESH) for s in range(S)]
        for cpy in cps:
            cpy.start()
        for cpy in cps:
            cpy.wait_recv()
        for cpy in cps:
            cpy.wait_send()

    hbm = pl.BlockSpec(memory_space=pl.ANY)
    return pl.pallas_call(
        body, in_specs=[hbm], out_specs=hbm, out_shape=jax.ShapeDtypeStruct((S, Kh, Nb), dw.dtype),
        scratch_shapes=[pltpu.SemaphoreType.DMA((S,)), pltpu.SemaphoreType.DMA((S,))],
        compiler_params=_cp0(), name=name)(dw)


def share_parts(parts, name):
    S = parts.shape[0]

    def body(p_ref, o_ref, send_sems, recv_sems):
        x, y, c = _place()
        cps = [pltpu.make_async_remote_copy(src_ref=p_ref.at[s], dst_ref=o_ref.at[s], send_sem=send_sems.at[s],
                                            recv_sem=recv_sems.at[s], device_id=(x, y, 1 - c), device_id_type=MESH)
               for s in range(S)]
        for cpy in cps:
            cpy.start()
        for cpy in cps:
            cpy.wait_recv()
        for cpy in cps:
            cpy.wait_send()

    hbm = pl.BlockSpec(memory_space=pl.ANY)
    return pl.pallas_call(
        body, in_specs=[hbm], out_specs=hbm, out_shape=jax.ShapeDtypeStruct(parts.shape, parts.dtype),
        scratch_shapes=[pltpu.SemaphoreType.DMA((S,)), pltpu.SemaphoreType.DMA((S,))],
        compiler_params=_cp0(), name=name)(parts)


def chip_sums(dw, c_idx, tag):
    return add_half(dw, swap_halves(dw, f"rs_swap_{tag}"), c_idx, f"rs_add_half_{tag}")


def finish_grad(parts, c_idx, tag):
    return add_shards(parts, share_parts(parts, f"rs_share_{tag}"), c_idx, f"rs_add_shards_{tag}")


def _pack(parts):
    flat = jnp.concatenate([t.reshape(-1).astype(F32) for t in parts])
    n = flat.shape[0]
    pad = (-n) % 1024
    return jnp.pad(flat, (0, pad)).reshape(-1, 128)


def _unpack(buf, shapes):
    flat = buf.reshape(buf.shape[:-2] + (-1,))
    out, o = [], 0
    for s in shapes:
        n = 1
        for d in s:
            n *= d
        out.append(flat[..., o:o + n].reshape(buf.shape[:-2] + tuple(s)))
        o += n
    return out


def _rope_tables(seq, ctx_len):
    t = jnp.arange(seq)
    inv = 1.0 / (ROPE_THETA ** (jnp.arange(0, DH // 4, dtype=F32) / (DH // 4)))
    ar = (t // GRID_W).astype(F32)[:, None] * inv[None, :]
    ac = (t % GRID_W).astype(F32)[:, None] * inv[None, :]
    cos = jnp.concatenate([jnp.cos(ar), jnp.cos(ar), jnp.cos(ac), jnp.cos(ac)], axis=-1)
    sin = jnp.concatenate([-jnp.sin(ar), jnp.sin(ar), -jnp.sin(ac), jnp.sin(ac)], axis=-1)
    return (jnp.concatenate([jnp.ones((ctx_len, DH), F32), cos], axis=0),
            jnp.concatenate([jnp.zeros((ctx_len, DH), F32), sin], axis=0))


def kernel(x, c, ctx, c_ctx, w_mod, b_mod, norm1_g, norm2_g, w_in, ret_decay_f, ret_decay_b, conv_dw_w, conv_dw_b, conv_ln_g, conv_ln_b, w_out, ffn_w_up, ffn_dw_w, ffn_dw_b, ffn_w_down, final_norm_g, loss_target, m_c_ctx, m_w_mod, m_b_mod, m_norm1_g, m_norm2_g, m_w_in, m_ret_decay_f, m_ret_decay_b, m_conv_dw_w, m_conv_dw_b, m_conv_ln_g, m_conv_ln_b, m_w_out, m_ffn_w_up, m_ffn_dw_w, m_ffn_dw_b, m_ffn_w_down, m_final_norm_g, v_c_ctx, v_w_mod, v_b_mod, v_norm1_g, v_norm2_g, v_w_in, v_ret_decay_f, v_ret_decay_b, v_conv_dw_w, v_conv_dw_b, v_conv_ln_g, v_conv_ln_b, v_w_out, v_ffn_w_up, v_ffn_dw_w, v_ffn_dw_b, v_ffn_w_down, v_final_norm_g):
    _, SEQ, D = x.shape
    CTX = ctx.shape[1]
    L = w_in.shape[0]
    CWs = conv_dw_w.shape[2]
    CW = 4 * CWs
    RW = 4 * w_out.shape[1] - CW
    H = RW // DH
    CFs = ffn_dw_w.shape[-1]
    CF = 4 * CFs
    NMs = w_mod.shape[2]
    T = CTX + SEQ
    ncr, ncc = CTX // R, CTX // RC
    assert CTX == R and RW == CW and RW % DH == 0 and SEQ % R == 0 and R % GRID_W == 0
    assert w_in.shape[2] * 4 == 4 * RW + 2 * CW and NMs * 4 == N_MOD * D

    mx, my, mc = _place()
    me = 4 * mx + 2 * my + mc
    jm = 2 * mx + my
    c_idx = jnp.reshape(mc, (1,)).astype(jnp.int32)

    shapes0 = [(D,), (L, CONV_K, CWs), (L, 9, CFs)]
    g0, _ = gather_sum(_pack([c[0], conv_dw_w, ffn_dw_w.reshape(L, 9, CFs)]), "gather_cond")
    c_all, cw_all, fw_all = _unpack(g0, shapes0)
    conv_w = jnp.concatenate([cw_all[2 * j] for j in range(4)], axis=-1)
    ffn_w = jnp.concatenate([fw_all[2 * j] for j in range(4)], axis=-1)
    conv_w32 = jnp.pad(conv_w, ((0, 0), (0, 32 - CONV_K), (0, 0)))
    ffn_w16 = jnp.pad(ffn_w, ((0, 0), (0, 7), (0, 0)))
    cs = jnp.concatenate([c_all, c_ctx[None, :], jnp.zeros((7, D), F32)], axis=0)
    mod_shard = mod_fwd(cs, w_mod, "mod_fwd")
    g1, _ = gather_sum(mod_shard.reshape(-1, 128), "gather_mod")
    mod_all = g1.reshape(8, L, 16, NMs)
    mod_full = jnp.concatenate([mod_all[2 * j] for j in range(4)], axis=-1) + b_mod[:, None, :]
    mod_mine = lax.dynamic_index_in_dim(mod_full, me, axis=1, keepdims=False)
    modv = jnp.stack([mod_full[:, 8], mod_mine], axis=1).reshape(L, 2, N_MOD, D)

    big = {"w_in": (w_in, True), "w_out": (w_out, False), "w_up": (ffn_w_up, True), "w_down": (ffn_w_down, False)}
    wb = {k: cast_bf16(w, f"cast_{k}") for k, (w, _) in big.items()}
    wg = {k: [gather_weights(wb[k][0:1], f"gather_{k}")[0]] for k in big}
    nxt = lambda k, l: wb[k][l + 1] if l + 1 < L else None

    def wmat(k, l):
        w = wg[k][l]
        return w if big[k][1] else w.reshape(1, w.shape[0] * w.shape[1], w.shape[2])

    cosT, sinT = _rope_tables(SEQ, CTX)
    lgt = [jnp.broadcast_to(jnp.concatenate([jax.nn.log_sigmoid(ret_decay_f[l]), jax.nn.log_sigmoid(ret_decay_b[l])])[:, None],
                            (2 * H, DH)) for l in range(L)]
    row = lambda t: t.reshape(1, -1)

    def project(a, k, l, out_dtype, name, res=None):
        y, xn, gathered = mm_nn(a, wmat(k, l), out_dtype, name, nxt(k, l), res)
        if gathered is not None:
            wg[k].append(gathered)
        return y, xn

    gbig = {k: [None] * L for k in big}
    waiting = []

    def back(a, dy, k, l, out_dtype, tag):
        dw = mm_tn(a, dy, 4 if big[k][1] else 1, f"mm_{tag}_dw")
        dw = dw if big[k][1] else dw.reshape(4, dw.shape[1] // 4, dw.shape[2])
        prev = waiting.pop() if waiting else None
        da, parts, shared = mm_nt(dy, wmat(k, l), out_dtype, f"mm_{tag}_dx", chip_sums(dw, c_idx, k),
                                  prev[0] if prev else None)
        if prev:
            gbig[prev[1]][prev[2]] = add_shards(prev[0], shared, c_idx, f"rs_add_shards_{prev[1]}")
        waiting.append((parts, k, l))
        return da

    xs = jnp.concatenate([ctx[0], x[0]], axis=0)
    saved = []
    for l in range(L):
        h1 = norm_mod(xs, row(norm1_g[l]), modv[l], 0, 1, ncr, "norm_mod")
        p, _ = project(h1, "w_in", l, BF16, "mm_in")
        uc = conv31_fwd(p, conv_w32[l], row(conv_dw_b[l]), RW, CW, ncr, "conv31_fwd")
        o_f, o_b, stf, stb = ret_fwd(p, cosT, sinT, lgt[l], H, ncc, "ret_fwd")
        mix = mix_fwd(o_f, o_b, p, uc, row(conv_ln_g[l]), row(conv_ln_b[l]), H, "mix_fwd")
        y1, x2 = project(mix, "w_out", l, BF16, "mm_out", (xs, modv[l][:, 2], CTX))
        h2 = norm_mod(x2, row(norm2_g[l]), modv[l], 3, 4, ncr, "norm_mod")
        up, _ = project(h2, "w_up", l, BF16, "mm_up")
        act, gcs = ffn_act(up, ffn_w16[l], row(ffn_dw_b[l]), CF, ncr, "ffn_act")
        y2, x3 = project(act, "w_down", l, BF16, "mm_down", (x2, modv[l][:, 5], CTX))
        saved.append((xs, h1, p, uc, o_f, o_b, stf, stb, mix, y1, x2, h2, up, act, gcs, y2))
        xs = x3

    dx, acc_loss = loss_head(xs, loss_target[0], row(final_norm_g), ncr, "loss_head")
    loss = lax.psum(0.5 / D * jnp.sum(acc_loss[1]), ("x", "y", "c"))

    small = [None] * L
    for l in reversed(range(L)):
        x1, h1, p, uc, o_f, o_b, stf, stb, mix, y1, x2, h2, up, act, gcs, y2 = saved[l]
        dy2, ag2 = gate_res_bwd(dx, y2, modv[l], 5, ncr, "gate_res_bwd")
        dact = back(act, dy2, "w_down", l, BF16, "down")
        dgc, dup, accb = ffn_act_bwd1(up, gcs, dact, CF, "ffn_act_bwd1")
        dup, accfw = ffn_act_bwd2(dgc, up, dup, ffn_w16[l], CF, ncr, "ffn_act_bwd2")
        dh2 = back(h2, dup, "w_up", l, F32, "up")
        dx2, an2 = norm_mod_bwd(x2, row(norm2_g[l]), modv[l], dh2, dx, 4, ncr, "norm_mod_bwd")
        dy1, ag1 = gate_res_bwd(dx2, y1, modv[l], 2, ncr, "gate_res_bwd")
        dmix = back(mix, dy1, "w_out", l, BF16, "out")
        do, dg, duc, accln = mix_bwd(dmix, o_f, o_b, p, uc, row(conv_ln_g[l]), row(conv_ln_b[l]), H, "mix_bwd")
        da, dbg, acccw = conv31_bwd(duc, p, conv_w32[l], RW, CW, ncr, "conv31_bwd")
        dqf, dkf, dvf, dqb, dkb, dvb, glg = ret_bwd(p, do, cosT, sinT, lgt[l], stf, stb, H, ncc, "ret_bwd")
        dp = assemble_dp(dqf, dqb, dkf, dkb, dvf, dvb, dg, da, dbg, "assemble_dp")
        dh1 = back(h1, dp, "w_in", l, F32, "in")
        dx, an1 = norm_mod_bwd(x1, row(norm1_g[l]), modv[l], dh1, dx2, 1, ncr, "norm_mod_bwd")
        dmod = jnp.stack([jnp.stack([an1[0], an1[1], ag1[0], an2[0], an2[1], ag2[0]]),
                          jnp.stack([an1[2], an1[3], ag1[1], an2[2], an2[3], ag2[1]])])
        dlg = jnp.sum(glg, axis=(1, 2))
        dth = dlg * jnp.concatenate([jax.nn.sigmoid(-ret_decay_f[l]), jax.nn.sigmoid(-ret_decay_b[l])])
        small[l] = [dmod, an1[4], an2[4], acccw[31], accln[0], accln[1], acccw[:CONV_K], accfw[:9], jnp.sum(accb, axis=0), dth]
    parts, k, l = waiting.pop()
    gbig[k][l] = finish_grad(parts, c_idx, k)
    grad_x = dx[CTX:][None]

    shapes1 = [(2, N_MOD, D), (D,), (D,), (CW,), (CW,), (CW,), (CONV_K, CW), (9, CF), (CF,), (2 * H,)]
    flat_parts = [t for l in range(L) for t in small[l]] + [acc_loss[0]]
    g2, s2 = gather_sum(_pack(flat_parts), "gather_small_grads")
    sums = _unpack(s2, shapes1 * L + [(D,)])
    per_dev = _unpack(g2, shapes1 * L + [(D,)])
    nS = len(shapes1)
    col = lambda i: jnp.stack([sums[l * nS + i] for l in range(L)])
    dmod_sum = col(0)
    dmod_dev = jnp.stack([per_dev[l * nS] for l in range(L)], axis=0)
    g_b_mod = (dmod_sum[:, 0] + dmod_sum[:, 1]).reshape(L, N_MOD * D)
    g_norm1, g_norm2 = col(1), col(2)
    g_conv_b, g_ln_g, g_ln_b = col(3), col(4), col(5)
    g_conv_w = lax.dynamic_slice_in_dim(col(6), jm * CWs, CWs, axis=2)
    g_ffn_w = lax.dynamic_slice_in_dim(col(7), jm * CFs, CFs, axis=2).reshape(L, 3, 3, CFs)
    g_ffn_b = col(8)
    g_ret = col(9)
    g_final = sums[-1]

    dmod_rows = jnp.concatenate([dmod_dev[:, :, 1].reshape(L, 8, N_MOD * D), dmod_sum[:, 0].reshape(L, 1, N_MOD * D),
                                 jnp.zeros((L, 7, N_MOD * D), F32)], axis=1)
    dmod_shard = lax.dynamic_slice_in_dim(dmod_rows, jm * NMs, NMs, axis=2)
    g_w_mod, ds_part = mod_bwd(cs, w_mod, dmod_shard, "mod_bwd")
    _, ds_sum = gather_sum(ds_part.reshape(-1, 128), "gather_dsilu")
    ds_ctx = 0.5 * ds_sum.reshape(16, D)[8]
    sg = jax.nn.sigmoid(c_ctx)
    g_c_ctx = ds_ctx * (sg * (1.0 + c_ctx * (1.0 - sg)))

    grads = {
        "c_ctx": g_c_ctx, "w_mod": g_w_mod, "b_mod": g_b_mod, "norm1_g": g_norm1, "norm2_g": g_norm2,
        "w_in": jnp.stack(gbig["w_in"]), "ret_decay_f": g_ret[:, :H], "ret_decay_b": g_ret[:, H:],
        "conv_dw_w": g_conv_w, "conv_dw_b": g_conv_b, "conv_ln_g": g_ln_g, "conv_ln_b": g_ln_b,
        "w_out": jnp.stack(gbig["w_out"]), "ffn_w_up": jnp.stack(gbig["w_up"]), "ffn_dw_w": g_ffn_w,
        "ffn_dw_b": g_ffn_b, "ffn_w_down": jnp.stack(gbig["w_down"]), "final_norm_g": g_final,
    }
    params = {
        "c_ctx": (c_ctx, m_c_ctx, v_c_ctx), "w_mod": (w_mod, m_w_mod, v_w_mod), "b_mod": (b_mod, m_b_mod, v_b_mod),
        "norm1_g": (norm1_g, m_norm1_g, v_norm1_g), "norm2_g": (norm2_g, m_norm2_g, v_norm2_g),
        "w_in": (w_in, m_w_in, v_w_in), "ret_decay_f": (ret_decay_f, m_ret_decay_f, v_ret_decay_f),
        "ret_decay_b": (ret_decay_b, m_ret_decay_b, v_ret_decay_b),
        "conv_dw_w": (conv_dw_w, m_conv_dw_w, v_conv_dw_w), "conv_dw_b": (conv_dw_b, m_conv_dw_b, v_conv_dw_b),
        "conv_ln_g": (conv_ln_g, m_conv_ln_g, v_conv_ln_g), "conv_ln_b": (conv_ln_b, m_conv_ln_b, v_conv_ln_b),
        "w_out": (w_out, m_w_out, v_w_out), "ffn_w_up": (ffn_w_up, m_ffn_w_up, v_ffn_w_up),
        "ffn_dw_w": (ffn_dw_w, m_ffn_dw_w, v_ffn_dw_w), "ffn_dw_b": (ffn_dw_b, m_ffn_dw_b, v_ffn_dw_b),
        "ffn_w_down": (ffn_w_down, m_ffn_w_down, v_ffn_w_down),
        "final_norm_g": (final_norm_g, m_final_norm_g, v_final_norm_g),
    }
    names = list(params)
    upd = {n: adamw(params[n][0], grads[n], params[n][1], params[n][2], f"adamw_{n}") for n in names}
    return (loss, grad_x, *[grads[n] for n in names], *[upd[n][0] for n in names],
            *[upd[n][1] for n in names], *[upd[n][2] for n in names])
```
